```python
import jax, jax.numpy as jnp
from jax import lax
import numpy as np

D_MODEL = 1024
BATCH = 2
SEQ = 16384
DEPTH = 4

GRID_W = 64
CTX_LEN = 256
HEAD_DIM = 64
A_Q_HEADS = 8
A_KV_HEADS = 2
C_Q_HEADS = 8
C_KV_HEADS = 2
WINDOW = 128
BLOCK = 128
CONV_CH = D_MODEL // 2
CONV_K = 3
D_FF = 2816
N_BRANCH = 3
N_MOD = 9
ROPE_THETA = 10000.0
EPS = 1e-6
NEG_INF = -1e30
SCALE = HEAD_DIM ** -0.5
A_Q = A_Q_HEADS * HEAD_DIM
A_KV = A_KV_HEADS * HEAD_DIM
C_Q = C_Q_HEADS * HEAD_DIM
C_KV = C_KV_HEADS * HEAD_DIM
KV_SIZES = (A_KV, A_KV, C_KV, C_KV)
IN_SIZES = KV_SIZES + (A_Q, C_Q, CONV_CH, CONV_CH, CONV_CH)
KV_COLS = sum(KV_SIZES)
IN_COLS = sum(IN_SIZES)

kernel_name = 'hybrid_dit_parallel_mixers'


def _split(z, sizes):
    return jnp.split(z, [int(v) for v in np.cumsum(sizes)[:-1]], axis=-1)


def _rmsnorm(x, g):
    xf = x.astype(jnp.float32)
    y = xf * lax.rsqrt(jnp.mean(xf * xf, axis=-1, keepdims=True) + EPS)
    return (y * g.astype(jnp.float32)).astype(x.dtype)


def _modulate(x, g, shift, scale):
    return _rmsnorm(x, g) * (1 + scale) + shift


def _ada(cvec, w, b):
    m = jax.nn.silu(cvec) @ w + b
    m = m.reshape(m.shape[:-1] + (N_MOD, 1, D_MODEL))
    return [m[..., j, :, :] for j in range(N_MOD)]


def _swiglu(h, wg, wu, wd):
    return (jax.nn.silu(h @ wg) * (h @ wu)) @ wd


def _heads(z, n):
    return z.reshape(z.shape[:-1] + (n, HEAD_DIM))


def _rope_tables(rows, dtype):
    row = jnp.repeat(jnp.arange(rows), GRID_W).astype(jnp.float32)
    col = jnp.tile(jnp.arange(GRID_W), rows).astype(jnp.float32)
    half = HEAD_DIM // 2
    inv = ROPE_THETA ** (-jnp.arange(0, half, 2, dtype=jnp.float32) / half)
    ang_r = row[:, None] * inv
    ang_c = col[:, None] * inv
    tabs = (jnp.cos(ang_r), jnp.sin(ang_r), jnp.cos(ang_c), jnp.sin(ang_c))
    return tuple(t.astype(dtype)[None, :, None, :] for t in tabs)


def _rot_half(u, cos, sin):
    u1, u2 = jnp.split(u, 2, axis=-1)
    return jnp.concatenate([u1 * cos - u2 * sin, u2 * cos + u1 * sin], axis=-1)


def _rope2d(x, tabs):
    cr, sr, cc, sc = tabs
    xr, xc = jnp.split(x, 2, axis=-1)
    return jnp.concatenate([_rot_half(xr, cr, sr), _rot_half(xc, cc, sc)], axis=-1)


def _qk(z, n, g, tabs):
    y = _rmsnorm(_heads(z, n), g)
    return y if tabs is None else _rope2d(y, tabs)


def _softmax(s, sink_g):
    if sink_g is None:
        return jax.nn.softmax(s, axis=-1)
    col = jnp.broadcast_to(sink_g.astype(jnp.float32)[None, :, :, None, None], s.shape[:-1] + (1,))
    return jax.nn.softmax(jnp.concatenate([s, col], axis=-1), axis=-1)[..., :-1]


def _ctx_attn(q, k, v, sink):
    B, L, HQ, _ = q.shape
    KV = k.shape[2]
    G = HQ // KV
    qg = q.reshape(B, L, KV, G, HEAD_DIM)
    s = jnp.einsum('bqkgd,bnkd->bkgqn', qg, k).astype(jnp.float32) * SCALE
    p = _softmax(s, None if sink is None else sink.reshape(KV, G))
    o = jnp.einsum('bkgqn,bnkd->bqkgd', p.astype(v.dtype), v)
    return o.reshape(B, L, HQ * HEAD_DIM)


def _window_attn(q, k, v, k_ctx, v_ctx, sink):
    B, S, HQ, _ = q.shape
    KV = k.shape[2]
    G = HQ // KV
    nb = S // BLOCK
    span = BLOCK + 2 * WINDOW
    qb = jnp.moveaxis(q.reshape(B, nb, BLOCK, KV, G, HEAD_DIM), 1, 0)
    pad = ((0, 0), (WINDOW, WINDOW), (0, 0), (0, 0))
    kp = jnp.pad(k, pad)
    vp = jnp.pad(v, pad)
    sink_g = sink.reshape(KV, G)
    ctx_ok = jnp.ones((BLOCK, k_ctx.shape[1]), dtype=bool)

    def block(args):
        i, qi = args
        start = i * BLOCK
        keys = jnp.concatenate([k_ctx, lax.dynamic_slice_in_dim(kp, start, span, axis=1)], axis=1)
        vals = jnp.concatenate([v_ctx, lax.dynamic_slice_in_dim(vp, start, span, axis=1)], axis=1)
        qpos = start + jnp.arange(BLOCK)
        kpos = start - WINDOW + jnp.arange(span)
        ok = (jnp.abs(qpos[:, None] - kpos[None, :]) <= WINDOW) & (kpos >= 0)[None, :] & (kpos < S)[None, :]
        ok = jnp.concatenate([ctx_ok, ok], axis=1)
        s = jnp.einsum('bqkgd,bnkd->bkgqn', qi, keys).astype(jnp.float32) * SCALE
        p = _softmax(jnp.where(ok, s, NEG_INF), sink_g)
        return jnp.einsum('bkgqn,bnkd->bqkgd', p.astype(vals.dtype), vals)

    o = lax.map(block, (jnp.arange(nb), qb))
    return jnp.moveaxis(o, 0, 1).reshape(B, S, HQ * HEAD_DIM)


def _global_attn(q, k, v, k_ctx, v_ctx):
    B, S, HQ, _ = q.shape
    KV = k.shape[2]
    G = HQ // KV
    nb = S // BLOCK
    keys = jnp.concatenate([k_ctx, k], axis=1)
    vals = jnp.concatenate([v_ctx, v], axis=1)
    qb = jnp.moveaxis(q.reshape(B, nb, BLOCK, KV, G, HEAD_DIM), 1, 0)

    def block(qi):
        s = jnp.einsum('bqkgd,bnkd->bkgqn', qi, keys).astype(jnp.float32) * SCALE
        p = jax.nn.softmax(s, axis=-1)
        return jnp.einsum('bkgqn,bnkd->bqkgd', p.astype(vals.dtype), vals)

    o = lax.map(block, qb)
    return jnp.moveaxis(o, 0, 1).reshape(B, S, HQ * HEAD_DIM)


def _gated_conv(b, c, u, w):
    y = lax.conv_general_dilated(c * u, w.astype(u.dtype)[:, None, :], window_strides=(1,),
                                 padding=((CONV_K // 2, CONV_K // 2),),
                                 dimension_numbers=('NWC', 'WIO', 'NWC'),
                                 feature_group_count=CONV_CH)
    return b * y


def _merge(h, o_a, o_b, o_c, w_pa, w_pb, w_pc, w_gate, b_gate, w_o):
    g_a, g_b, g_c = jnp.split(jax.nn.sigmoid(h @ w_gate + b_gate), N_BRANCH, axis=-1)
    return (g_a * (o_a @ w_pa) + g_b * (o_b @ w_pb) + g_c * (o_c @ w_pc)) @ w_o


def setup_inputs(seed: int = 0) -> dict:
    key = jax.random.key(seed)
    ks = jax.random.split(key, 20)

    def nrm(k, shape, scale):
        return jax.random.normal(k, shape, jnp.float32) * scale

    return {
        'x': nrm(ks[0], (BATCH, SEQ, D_MODEL), 1.0),
        'c': nrm(ks[1], (BATCH, D_MODEL), 1.0),
        'ctx': nrm(ks[2], (BATCH, CTX_LEN, D_MODEL), 1.0),
        'c_ctx': nrm(ks[3], (D_MODEL,), 1.0),
        'w_ada': nrm(ks[4], (DEPTH, D_MODEL, N_MOD * D_MODEL), 0.5 * D_MODEL ** -0.5),
        'b_ada': nrm(ks[5], (DEPTH, N_MOD * D_MODEL), 0.02),
        'norm_g': 1.0 + nrm(ks[6], (DEPTH, 3, D_MODEL), 0.05),
        'ffn_w_gate': nrm(ks[7], (DEPTH, 2, D_MODEL, D_FF), D_MODEL ** -0.5),
        'ffn_w_up': nrm(ks[8], (DEPTH, 2, D_MODEL, D_FF), D_MODEL ** -0.5),
        'ffn_w_down': nrm(ks[9], (DEPTH, 2, D_FF, D_MODEL), D_FF ** -0.5),
        'w_in': nrm(ks[10], (DEPTH, D_MODEL, IN_COLS), D_MODEL ** -0.5),
        'qk_g': 1.0 + nrm(ks[11], (DEPTH, 4, HEAD_DIM), 0.05),
        'sink_a': nrm(ks[12], (DEPTH, A_Q_HEADS), 0.5),
        'conv_w': nrm(ks[13], (DEPTH, CONV_K, CONV_CH), CONV_K ** -0.5),
        'w_pa': nrm(ks[14], (DEPTH, A_Q, D_MODEL), A_Q ** -0.5),
        'w_pb': nrm(ks[15], (DEPTH, CONV_CH, D_MODEL), CONV_CH ** -0.5),
        'w_pc': nrm(ks[16], (DEPTH, C_Q, D_MODEL), C_Q ** -0.5),
        'w_gate': nrm(ks[17], (DEPTH, D_MODEL, N_BRANCH * D_MODEL), D_MODEL ** -0.5),
        'b_gate': nrm(ks[18], (DEPTH, N_BRANCH * D_MODEL), 0.02),
        'w_o': nrm(ks[19], (DEPTH, D_MODEL, D_MODEL), D_MODEL ** -0.5),
    }


def reference(x, c, ctx, c_ctx, w_ada, b_ada, norm_g, ffn_w_gate, ffn_w_up, ffn_w_down, w_in, qk_g,
              sink_a, conv_w, w_pa, w_pb, w_pc, w_gate, b_gate, w_o):
    S = x.shape[1]
    ROWS = S // GRID_W
    tabs = _rope_tables(ROWS, x.dtype)
    xc = ctx
    for i in range(DEPTH):
        last = i == DEPTH - 1
        m = _ada(c, w_ada[i], b_ada[i])
        mc = _ada(c_ctx, w_ada[i], b_ada[i])

        x = x + 0.5 * m[2] * _swiglu(_modulate(x, norm_g[i, 0], m[0], m[1]),
                                     ffn_w_gate[i, 0], ffn_w_up[i, 0], ffn_w_down[i, 0])
        xc = xc + 0.5 * mc[2] * _swiglu(_modulate(xc, norm_g[i, 0], mc[0], mc[1]),
                                        ffn_w_gate[i, 0], ffn_w_up[i, 0], ffn_w_down[i, 0])

        h = _modulate(x, norm_g[i, 1], m[3], m[4])
        hc = _modulate(xc, norm_g[i, 1], mc[3], mc[4])
        if last:
            zc = _split(hc @ w_in[i][:, :KV_COLS], KV_SIZES)
        else:
            zc = _split(hc @ w_in[i], IN_SIZES)
        k_a_c = _qk(zc[0], A_KV_HEADS, qk_g[i, 1], None)
        v_a_c = _heads(zc[1], A_KV_HEADS)
        k_c_c = _qk(zc[2], C_KV_HEADS, qk_g[i, 3], None)
        v_c_c = _heads(zc[3], C_KV_HEADS)

        z = _split(h @ w_in[i], IN_SIZES)
        k_a = _qk(z[0], A_KV_HEADS, qk_g[i, 1], tabs)
        v_a = _heads(z[1], A_KV_HEADS)
        k_c = _qk(z[2], C_KV_HEADS, qk_g[i, 3], tabs)
        v_c = _heads(z[3], C_KV_HEADS)
        q_a = _qk(z[4], A_Q_HEADS, qk_g[i, 0], tabs)
        q_c = _qk(z[5], C_Q_HEADS, qk_g[i, 2], tabs)
        o_a = _window_attn(q_a, k_a, v_a, k_a_c, v_a_c, sink_a[i])
        o_c = _global_attn(q_c, k_c, v_c, k_c_c, v_c_c)
        o_b = _gated_conv(z[6], z[7], z[8], conv_w[i])
        x = x + m[5] * _merge(h, o_a, o_b, o_c, w_pa[i], w_pb[i], w_pc[i], w_gate[i], b_gate[i], w_o[i])

        if not last:
            q_a_c = _qk(zc[4], A_Q_HEADS, qk_g[i, 0], None)
            q_c_c = _qk(zc[5], C_Q_HEADS, qk_g[i, 2], None)
            o_a_c = _ctx_attn(q_a_c, k_a_c, v_a_c, sink_a[i])
            o_c_c = _ctx_attn(q_c_c, k_c_c, v_c_c, None)
            o_b_c = _gated_conv(zc[6], zc[7], zc[8], conv_w[i])
            xc = xc + mc[5] * _merge(hc, o_a_c, o_b_c, o_c_c, w_pa[i], w_pb[i], w_pc[i],
                                     w_gate[i], b_gate[i], w_o[i])

        x = x + 0.5 * m[8] * _swiglu(_modulate(x, norm_g[i, 2], m[6], m[7]),
                                     ffn_w_gate[i, 1], ffn_w_up[i, 1], ffn_w_down[i, 1])
        if not last:
            xc = xc + 0.5 * mc[8] * _swiglu(_modulate(xc, norm_g[i, 2], mc[6], mc[7]),
                                            ffn_w_gate[i, 1], ffn_w_up[i, 1], ffn_w_down[i, 1])
    return x
```

```python
import functools

import jax
import jax.numpy as jnp
from jax import lax
from jax.experimental import pallas as pl
from jax.experimental.pallas import tpu as pltpu

GRID_W = 64
HEAD_DIM = 64
Q_HEADS = 8
KV_HEADS = 2
GROUP = Q_HEADS // KV_HEADS
WINDOW = 128
N_MOD = 9
ROPE_THETA = 10000.0
EPS = 1e-6
NEG_INF = -1e30
SCALE = HEAD_DIM ** -0.5
Q_COLS = Q_HEADS * HEAD_DIM
KV_COLS = KV_HEADS * HEAD_DIM

LANES = 128
TM = 256
TK = 256
VMEM_LIMIT = 56 * 1024 * 1024

F32 = jnp.float32
BF16 = jnp.bfloat16


def _params(n_axes):
    return pltpu.CompilerParams(dimension_semantics=("arbitrary",) * n_axes,
                                vmem_limit_bytes=VMEM_LIMIT)


def _resident(shape, index_map):
    return pl.BlockSpec(shape, index_map, pipeline_mode=pl.Buffered(1))


def _sigmoid(v):
    return 1.0 / (1.0 + jnp.exp(-v))


def _rms_mod(x, g, shift, scale):
    y = x * lax.rsqrt(jnp.mean(x * x, axis=-1, keepdims=True) + EPS)
    return (y * g) * (1.0 + scale) + shift


def _ada_kernel(c_ref, w_ref, b_ref, o_ref):
    c = c_ref[...]
    s = (c * _sigmoid(c)).astype(BF16)
    o_ref[...] = jnp.dot(s, w_ref[...].astype(BF16), preferred_element_type=F32) + b_ref[...]


def _ada(cvec, w_ada, b_ada):
    depth, d, n = w_ada.shape
    tn = 1024
    return pl.pallas_call(
        _ada_kernel,
        grid=(depth, n // tn),
        in_specs=[pl.BlockSpec((8, d), lambda l, j: (0, 0)),
                  pl.BlockSpec((None, d, tn), lambda l, j: (l, 0, j)),
                  pl.BlockSpec((None, 1, tn), lambda l, j: (l, 0, j))],
        out_specs=pl.BlockSpec((None, 8, tn), lambda l, j: (l, 0, j)),
        out_shape=jax.ShapeDtypeStruct((depth, 8, n), F32),
        compiler_params=_params(2),
        name="ada",
    )(cvec, w_ada, b_ada.reshape(depth, 1, n))


def _ffn_kernel(x_ref, mod_ref, g_ref, wg_ref, wu_ref, wd_ref, o_ref, *, j0):
    x = x_ref[...]
    h = _rms_mod(x, g_ref[...], mod_ref[j0:j0 + 1, :], mod_ref[j0 + 1:j0 + 2, :]).astype(BF16)
    a = jnp.dot(h, wg_ref[...], preferred_element_type=F32)
    u = jnp.dot(h, wu_ref[...], preferred_element_type=F32)
    act = (a * _sigmoid(a) * u).astype(BF16)
    y = jnp.dot(act, wd_ref[...], preferred_element_type=F32)
    o_ref[...] = x + (0.5 * mod_ref[j0 + 2:j0 + 3, :]) * y


def _ffn(xs, mod, norm_g, wg, wu, wd, *, layer, which, j0):
    b, l, d = xs.shape
    dff = wg.shape[-1]
    x_spec = pl.BlockSpec((None, TM, d), lambda bi, i: (bi, i, 0))
    return pl.pallas_call(
        functools.partial(_ffn_kernel, j0=j0),
        grid=(b, l // TM),
        in_specs=[x_spec,
                  pl.BlockSpec((None, None, N_MOD, d), lambda bi, i: (bi, jnp.minimum(i, 1), 0, 0)),
                  _resident((None, None, 1, d), lambda bi, i: (layer, 2 * which, 0, 0)),
                  _resident((None, None, d, dff), lambda bi, i: (layer, which, 0, 0)),
                  _resident((None, None, d, dff), lambda bi, i: (layer, which, 0, 0)),
                  _resident((None, None, dff, d), lambda bi, i: (layer, which, 0, 0))],
        out_specs=x_spec,
        out_shape=jax.ShapeDtypeStruct(xs.shape, F32),
        input_output_aliases={0: 0},
        compiler_params=_params(2),
        name="ffn",
    )(xs, mod, norm_g, wg, wu, wd)


def _norm_rope(t, gain, cos, sa, sb, lane):
    s = t * t
    for sh in (1, 2, 4, 8, 16, 32):
        up = pltpu.roll(s, sh, 1)
        dn = pltpu.roll(s, LANES - sh, 1)
        s = s + jnp.where((lane & sh) != 0, up, dn)
    y = t * lax.rsqrt(s * (1.0 / HEAD_DIM) + EPS) * gain
    quarter = HEAD_DIM // 4
    return y * cos + pltpu.roll(y, LANES - quarter, 1) * sa + pltpu.roll(y, quarter, 1) * sb


def _inproj_kernel(x_ref, mod_ref, g_ref, w_ref, qkg_ref, rope_ref,
                   qta_ref, qtc_ref, ka_ref, kc_ref, vta_ref, vtc_ref, b_ref, cu_ref):
    x = x_ref[...]
    h = _rms_mod(x, g_ref[...], mod_ref[3:4, :], mod_ref[4:5, :]).astype(BF16)
    z = jnp.dot(h, w_ref[...], preferred_element_type=F32)
    cos, sa, sb = rope_ref[0], rope_ref[1], rope_ref[2]
    lane = lax.broadcasted_iota(jnp.int32, (x.shape[0], LANES), 1)
    nr = functools.partial(_norm_rope, cos=cos, sa=sa, sb=sb, lane=lane)

    ka_ref[...] = nr(z[:, 0:128], qkg_ref[1:2, :]).astype(BF16)
    vta_ref[...] = z[:, 128:256].T.astype(BF16)
    kc_ref[...] = nr(z[:, 256:384], qkg_ref[3:4, :]).astype(BF16)
    vtc_ref[...] = z[:, 384:512].T.astype(BF16)
    q0 = 4 * KV_COLS
    for c in range(Q_COLS // LANES):
        lo = q0 + c * LANES
        qa = nr(z[:, lo:lo + LANES], qkg_ref[0:1, :] * SCALE)
        qta_ref[c * LANES:(c + 1) * LANES, :] = qa.T.astype(BF16)
        lo = q0 + Q_COLS + c * LANES
        qc = nr(z[:, lo:lo + LANES], qkg_ref[2:3, :] * SCALE)
        qtc_ref[c * LANES:(c + 1) * LANES, :] = qc.T.astype(BF16)
    c0 = q0 + 2 * Q_COLS
    cw = b_ref.shape[-1]
    b_ref[...] = z[:, c0:c0 + cw]
    cu_ref[...] = z[:, c0 + cw:c0 + 2 * cw] * z[:, c0 + 2 * cw:c0 + 3 * cw]


def _inproj(xs, mod, norm_g, w_in, qkg, rope, *, layer):
    b, l, d = xs.shape
    ncol = w_in.shape[-1]
    cw = (ncol - 4 * KV_COLS - 2 * Q_COLS) // 3
    tok = lambda w: pl.BlockSpec((None, TM, w), lambda bi, i: (bi, i, 0))
    tr = lambda w: pl.BlockSpec((None, w, TM), lambda bi, i: (bi, 0, i))
    return pl.pallas_call(
        _inproj_kernel,
        grid=(b, l // TM),
        in_specs=[tok(d),
                  pl.BlockSpec((None, None, N_MOD, d), lambda bi, i: (bi, jnp.minimum(i, 1), 0, 0)),
                  _resident((None, None, 1, d), lambda bi, i: (layer, 1, 0, 0)),
                  _resident((None, d, ncol), lambda bi, i: (layer, 0, 0)),
                  _resident((None, 4, LANES), lambda bi, i: (layer, 0, 0)),
                  pl.BlockSpec((3, TM, LANES), lambda bi, i: (0, i, 0))],
        out_specs=[tr(Q_COLS), tr(Q_COLS), tok(KV_COLS), tok(KV_COLS), tr(KV_COLS), tr(KV_COLS),
                   tok(cw), tok(cw)],
        out_shape=[jax.ShapeDtypeStruct((b, Q_COLS, l), BF16),
                   jax.ShapeDtypeStruct((b, Q_COLS, l), BF16),
                   jax.ShapeDtypeStruct((b, l, KV_COLS), BF16),
                   jax.ShapeDtypeStruct((b, l, KV_COLS), BF16),
                   jax.ShapeDtypeStruct((b, KV_COLS, l), BF16),
                   jax.ShapeDtypeStruct((b, KV_COLS, l), BF16),
                   jax.ShapeDtypeStruct((b, l, cw), F32),
                   jax.ShapeDtypeStruct((b, l, cw), F32)],
        compiler_params=_params(2),
        name="inproj",
    )(xs, mod, norm_g, w_in, qkg, rope)


def _pad_q(q, g):
    z = jnp.zeros_like(q)
    return jnp.concatenate([q, z] if g == 0 else [z, q], axis=0)


def _global_attn_kernel(qt_ref, k_ref, vt_ref, o_ref, acc_ref):
    i = pl.program_id(1)
    tq = qt_ref.shape[-1]
    n_tiles = jnp.where(i == 0, 1, k_ref.shape[0] // TK)
    for g in range(KV_HEADS):
        qs = [_pad_q(qt_ref[(g * GROUP + j) * HEAD_DIM:(g * GROUP + j + 1) * HEAD_DIM, :], g)
              for j in range(GROUP)]

        def body(t, carry, g=g, qs=qs):
            start = pl.multiple_of(t * TK, TK)
            kt = k_ref[pl.ds(start, TK), :]
            vt = vt_ref[g * HEAD_DIM:(g + 1) * HEAD_DIM, pl.ds(start, TK)]
            out = []
            for j in range(GROUP):
                m, l, acc = carry[j]
                s = jnp.dot(kt, qs[j], preferred_element_type=F32)
                m_new = jnp.maximum(m, jnp.max(s, axis=0, keepdims=True))
                alpha = jnp.exp(m - m_new)
                p = jnp.exp(s - m_new)
                l = alpha * l + jnp.sum(p, axis=0, keepdims=True)
                acc = alpha * acc + jnp.dot(vt, p.astype(BF16), preferred_element_type=F32)
                out.append((m_new, l, acc))
            return tuple(out)

        init = tuple((jnp.full((1, tq), NEG_INF, F32), jnp.zeros((1, tq), F32),
                      jnp.zeros((HEAD_DIM, tq), F32)) for _ in range(GROUP))
        res = lax.fori_loop(0, n_tiles, body, init)
        for j in range(GROUP):
            _, l, acc = res[j]
            h = g * GROUP + j
            acc_ref[h * HEAD_DIM:(h + 1) * HEAD_DIM, :] = acc / l
    o_ref[...] = acc_ref[...].T


def _global_attn(qt, k, vt):
    b, _, l = qt.shape
    return pl.pallas_call(
        _global_attn_kernel,
        grid=(b, l // TM),
        in_specs=[pl.BlockSpec((None, Q_COLS, TM), lambda bi, i: (bi, 0, i)),
                  pl.BlockSpec((None, l, KV_COLS), lambda bi, i: (bi, 0, 0)),
                  pl.BlockSpec((None, KV_COLS, l), lambda bi, i: (bi, 0, 0))],
        out_specs=pl.BlockSpec((None, TM, Q_COLS), lambda bi, i: (bi, i, 0)),
        out_shape=jax.ShapeDtypeStruct((b, l, Q_COLS), F32),
        scratch_shapes=[pltpu.VMEM((Q_COLS, TM), F32)],
        compiler_params=_params(2),
        name="global_attn",
    )(qt, k, vt)


def _window_attn_kernel(qt_ref, k_ref, vt_ref, sink_ref, o_ref, acc_ref):
    i = pl.program_id(1)
    tq = qt_ref.shape[-1]
    l_all = k_ref.shape[0]
    span = tq + 2 * WINDOW
    start = pl.multiple_of(jnp.clip(i * tq - WINDOW, 0, l_all - span), LANES)
    kpos = start - TM + lax.broadcasted_iota(jnp.int32, (span, 1), 0)
    qbase = jnp.where(i >= 1, i * tq - TM, -(1 << 20))
    qpos = qbase + lax.broadcasted_iota(jnp.int32, (1, tq), 1)
    ok = jnp.logical_and(kpos >= 0, jnp.abs(qpos - kpos) <= WINDOW)
    bias = jnp.where(ok, 0.0, NEG_INF).astype(F32)
    k_ctx = k_ref[0:TM, :]
    k_win = k_ref[pl.ds(start, span), :]
    for h in range(Q_HEADS):
        g = h // GROUP
        q = _pad_q(qt_ref[h * HEAD_DIM:(h + 1) * HEAD_DIM, :], g)
        s1 = jnp.dot(k_ctx, q, preferred_element_type=F32)
        s2 = jnp.dot(k_win, q, preferred_element_type=F32) + bias
        sink = sink_ref[h:h + 1, :]
        m = jnp.maximum(jnp.maximum(jnp.max(s1, axis=0, keepdims=True),
                                    jnp.max(s2, axis=0, keepdims=True)), sink)
        p1 = jnp.exp(s1 - m)
        p2 = jnp.exp(s2 - m)
        l = (jnp.sum(p1, axis=0, keepdims=True) + jnp.sum(p2, axis=0, keepdims=True)
             + jnp.exp(sink - m))
        rows = slice(g * HEAD_DIM, (g + 1) * HEAD_DIM)
        acc = (jnp.dot(vt_ref[rows, 0:TM], p1.astype(BF16), preferred_element_type=F32)
               + jnp.dot(vt_ref[rows, pl.ds(start, span)], p2.astype(BF16),
                         preferred_element_type=F32))
        acc_ref[h * HEAD_DIM:(h + 1) * HEAD_DIM, :] = acc / l
    o_ref[...] = acc_ref[...].T


def _window_attn(qt, k, vt, sink):
    b, _, l = qt.shape
    sink_b = jnp.broadcast_to(sink.astype(F32)[:, None], (Q_HEADS, TM))
    return pl.pallas_call(
        _window_attn_kernel,
        grid=(b, l // TM),
        in_specs=[pl.BlockSpec((None, Q_COLS, TM), lambda bi, i: (bi, 0, i)),
                  pl.BlockSpec((None, l, KV_COLS), lambda bi, i: (bi, 0, 0)),
                  pl.BlockSpec((None, KV_COLS, l), lambda bi, i: (bi, 0, 0)),
                  pl.BlockSpec((Q_HEADS, TM), lambda bi, i: (0, 0))],
        out_specs=pl.BlockSpec((None, TM, Q_COLS), lambda bi, i: (bi, i, 0)),
        out_shape=jax.ShapeDtypeStruct((b, l, Q_COLS), F32),
        scratch_shapes=[pltpu.VMEM((Q_COLS, TM), F32)],
        compiler_params=_params(2),
        name="window_attn",
    )(qt, k, vt, sink_b)


def _merge_kernel(x_ref, mod_ref, g_ref, oa_ref, oc_ref, b_ref, cu_ref, cup_ref, cun_ref, cw_ref,
                  wgate_ref, bgate_ref, wpa_ref, wpb_ref, wpc_ref, wo_ref, o_ref):
    i = pl.program_id(1)
    n_tiles = pl.num_programs(1)
    x = x_ref[...]
    tm, d = x.shape
    h = _rms_mod(x, g_ref[...], mod_ref[3:4, :], mod_ref[4:5, :]).astype(BF16)
    gates = _sigmoid(jnp.dot(h, wgate_ref[...], preferred_element_type=F32) + bgate_ref[...])

    cu = cu_ref[...]
    has_prev = (i >= 2).astype(F32)
    has_next = jnp.logical_and(i >= 1, i < n_tiles - 1).astype(F32)
    prev_row = cup_ref[7:8, :] * has_prev
    next_row = cun_ref[0:1, :] * has_next
    row = lax.broadcasted_iota(jnp.int32, (tm, 1), 0)
    cu_dn = jnp.where(row == 0, prev_row, pltpu.roll(cu, 1, 0))
    cu_up = jnp.where(row == tm - 1, next_row, pltpu.roll(cu, tm - 1, 0))
    y = cw_ref[0:1, :] * cu_dn + cw_ref[1:2, :] * cu + cw_ref[2:3, :] * cu_up
    o_b = (b_ref[...] * y).astype(BF16)

    pa = jnp.dot(oa_ref[...].astype(BF16), wpa_ref[...], preferred_element_type=F32)
    pb = jnp.dot(o_b, wpb_ref[...], preferred_element_type=F32)
    pc = jnp.dot(oc_ref[...].astype(BF16), wpc_ref[...], preferred_element_type=F32)
    mix = gates[:, 0:d] * pa + gates[:, d:2 * d] * pb + gates[:, 2 * d:3 * d] * pc
    out = jnp.dot(mix.astype(BF16), wo_ref[...], preferred_element_type=F32)
    o_ref[...] = x + mod_ref[5:6, :] * out


def _merge(xs, mod, norm_g, o_a, o_c, bb, cu, conv_w, w_gate, b_gate, w_pa, w_pb, w_pc, w_o,
           *, layer):
    b, l, d = xs.shape
    cw = cu.shape[-1]
    sub = 8
    tok = lambda w: pl.BlockSpec((None, TM, w), lambda bi, i: (bi, i, 0))
    lay = lambda *s: _resident((None,) + s, lambda bi, i: (layer,) + (0,) * len(s))
    return pl.pallas_call(
        _merge_kernel,
        grid=(b, l // TM),
        in_specs=[tok(d),
                  pl.BlockSpec((None, None, N_MOD, d), lambda bi, i: (bi, jnp.minimum(i, 1), 0, 0)),
                  _resident((None, None, 1, d), lambda bi, i: (layer, 1, 0, 0)),
                  tok(Q_COLS), tok(Q_COLS), tok(cw), tok(cw),
                  pl.BlockSpec((None, sub, cw),
                               lambda bi, i: (bi, jnp.maximum(i * (TM // sub) - 1, 0), 0)),
                  pl.BlockSpec((None, sub, cw),
                               lambda bi, i: (bi, jnp.minimum((i + 1) * (TM // sub), l // sub - 1), 0)),
                  lay(3, cw), lay(d, 3 * d), lay(1, 3 * d), lay(Q_COLS, d), lay(cw, d),
                  lay(Q_COLS, d), lay(d, d)],
        out_specs=tok(d),
        out_shape=jax.ShapeDtypeStruct(xs.shape, F32),
        input_output_aliases={0: 0},
        compiler_params=_params(2),
        name="merge",
    )(xs, mod, norm_g, o_a, o_c, bb, cu, cu, cu, conv_w, w_gate, b_gate, w_pa, w_pb, w_pc, w_o)


def _rope_tables(s, n_ctx):
    rows = s // GRID_W
    row = jnp.repeat(jnp.arange(rows), GRID_W).astype(F32)
    col = jnp.tile(jnp.arange(GRID_W), rows).astype(F32)
    half = HEAD_DIM // 2
    inv = ROPE_THETA ** (-jnp.arange(0, half, 2, dtype=F32) / half)
    ang_r = row[:, None] * inv
    ang_c = col[:, None] * inv
    cos = jnp.concatenate([jnp.cos(ang_r)] * 2 + [jnp.cos(ang_c)] * 2, axis=-1)
    sin = jnp.concatenate([jnp.sin(ang_r)] * 2 + [jnp.sin(ang_c)] * 2, axis=-1)
    q = HEAD_DIM // 4
    first = (jnp.arange(HEAD_DIM) // q) % 2 == 0
    sa = jnp.where(first, -sin, 0.0)
    sb = jnp.where(first, 0.0, sin)
    tabs = jnp.stack([cos, sa, sb])
    ident = jnp.stack([jnp.ones((n_ctx, HEAD_DIM), F32), jnp.zeros((n_ctx, HEAD_DIM), F32),
                       jnp.zeros((n_ctx, HEAD_DIM), F32)])
    tabs = jnp.concatenate([ident, tabs], axis=1)
    return jnp.tile(tabs, (1, 1, LANES // HEAD_DIM))


def kernel(x, c, ctx, c_ctx, w_ada, b_ada, norm_g, ffn_w_gate, ffn_w_up, ffn_w_down, w_in, qk_g,
           sink_a, conv_w, w_pa, w_pb, w_pc, w_gate, b_gate, w_o):
    bsz, s, d = x.shape
    n_ctx = ctx.shape[1]
    depth = w_ada.shape[0]
    assert n_ctx == TM and s % TM == 0 and s % GRID_W == 0 and bsz < 8

    cvec = jnp.zeros((8, d), F32).at[:bsz].set(c).at[bsz].set(c_ctx)
    mods = _ada(cvec, w_ada, b_ada).reshape(depth, 8, N_MOD, d)
    lat = mods[:, :bsz]
    con = jnp.broadcast_to(mods[:, bsz:bsz + 1], lat.shape)
    mod_all = jnp.stack([con, lat], axis=2)

    rope = _rope_tables(s, n_ctx)
    qkg = jnp.tile(qk_g.astype(F32), (1, 1, LANES // HEAD_DIM))
    ng = norm_g.astype(F32)[:, :, None, :]
    wg, wu, wd = (w.astype(BF16) for w in (ffn_w_gate, ffn_w_up, ffn_w_down))
    w_in_b, w_gate_b, w_pa_b, w_pb_b, w_pc_b, w_o_b = (
        w.astype(BF16) for w in (w_in, w_gate, w_pa, w_pb, w_pc, w_o))
    b_gate3 = b_gate.astype(F32)[:, None, :]

    xs = jnp.concatenate([ctx, x], axis=1)
    for i in range(depth):
        mod = mod_all[i]
        xs = _ffn(xs, mod, ng, wg, wu, wd, layer=i, which=0, j0=0)
        qta, qtc, ka, kc, vta, vtc, bb, cu = _inproj(xs, mod, ng, w_in_b, qkg, rope, layer=i)
        o_a = _window_attn(qta, ka, vta, sink_a[i])
        o_c = _global_attn(qtc, kc, vtc)
        xs = _merge(xs, mod, ng, o_a, o_c, bb, cu, conv_w.astype(F32), w_gate_b, b_gate3,
                    w_pa_b, w_pb_b, w_pc_b, w_o_b, layer=i)
        xs = _ffn(xs, mod, ng, wg, wu, wd, layer=i, which=1, j0=6)
    return xs[:, n_ctx:, :]
```

```python
import functools

import jax
import jax.numpy as jnp
from jax import lax
from jax.experimental import pallas as pl
from jax.experimental.pallas import tpu as pltpu

GRID_W = 64
HEAD_DIM = 64
Q_HEADS = 8
KV_HEADS = 2
GROUP = Q_HEADS // KV_HEADS
WINDOW = 128
N_MOD = 9
ROPE_THETA = 10000.0
EPS = 1e-6
NEG_INF = -1e30
SCALE = HEAD_DIM ** -0.5
LOG2E = 1.4426950408889634
Q_SCALE = SCALE * LOG2E
Q_COLS = Q_HEADS * HEAD_DIM
KV_COLS = KV_HEADS * HEAD_DIM

LANES = 128
TM = 256
VMEM_LIMIT = 56 * 1024 * 1024

F32 = jnp.float32
BF16 = jnp.bfloat16


def _params(n_axes):
    return pltpu.CompilerParams(dimension_semantics=("arbitrary",) * n_axes,
                                vmem_limit_bytes=VMEM_LIMIT)


def _resident(shape, index_map):
    return pl.BlockSpec(shape, index_map, pipeline_mode=pl.Buffered(1))


def _sigmoid(v):
    return 1.0 / (1.0 + jnp.exp(-v))


def _rms_mod(x, g, shift, scale):
    y = x * lax.rsqrt(jnp.mean(x * x, axis=-1, keepdims=True) + EPS)
    return (y * g) * (1.0 + scale) + shift


def _ada_kernel(c_ref, w_ref, b_ref, o_ref):
    c = c_ref[...]
    s = (c * _sigmoid(c)).astype(BF16)
    o_ref[...] = jnp.dot(s, w_ref[...].astype(BF16), preferred_element_type=F32) + b_ref[...]


def _ada(cvec, w_ada, b_ada):
    depth, d, n = w_ada.shape
    tn = 1024
    return pl.pallas_call(
        _ada_kernel,
        grid=(depth, n // tn),
        in_specs=[pl.BlockSpec((8, d), lambda l, j: (0, 0)),
                  pl.BlockSpec((None, d, tn), lambda l, j: (l, 0, j)),
                  pl.BlockSpec((None, 1, tn), lambda l, j: (l, 0, j))],
        out_specs=pl.BlockSpec((None, 8, tn), lambda l, j: (l, 0, j)),
        out_shape=jax.ShapeDtypeStruct((depth, 8, n), F32),
        compiler_params=_params(2),
        name="ada",
    )(cvec, w_ada, b_ada.reshape(depth, 1, n))


def _ffn_kernel(x_ref, mod_ref, g_ref, wg_ref, wu_ref, wd_ref, o_ref, *, j0):
    x = x_ref[...]
    h = _rms_mod(x, g_ref[...], mod_ref[j0:j0 + 1, :], mod_ref[j0 + 1:j0 + 2, :]).astype(BF16)
    a = jnp.dot(h, wg_ref[...], preferred_element_type=F32)
    u = jnp.dot(h, wu_ref[...], preferred_element_type=F32)
    act = (a * _sigmoid(a) * u).astype(BF16)
    y = jnp.dot(act, wd_ref[...], preferred_element_type=F32)
    o_ref[...] = x + (0.5 * mod_ref[j0 + 2:j0 + 3, :]) * y


def _ffn(xs, mod, norm_g, wg, wu, wd, *, layer, which, j0):
    b, l, d = xs.shape
    dff = wg.shape[-1]
    x_spec = pl.BlockSpec((None, TM, d), lambda bi, i: (bi, i, 0))
    return pl.pallas_call(
        functools.partial(_ffn_kernel, j0=j0),
        grid=(b, l // TM),
        in_specs=[x_spec,
                  pl.BlockSpec((None, None, N_MOD, d), lambda bi, i: (bi, jnp.minimum(i, 1), 0, 0)),
                  _resident((None, None, 1, d), lambda bi, i: (layer, 2 * which, 0, 0)),
                  _resident((None, None, d, dff), lambda bi, i: (layer, which, 0, 0)),
                  _resident((None, None, d, dff), lambda bi, i: (layer, which, 0, 0)),
                  _resident((None, None, dff, d), lambda bi, i: (layer, which, 0, 0))],
        out_specs=x_spec,
        out_shape=jax.ShapeDtypeStruct(xs.shape, F32),
        input_output_aliases={0: 0},
        compiler_params=_params(2),
        name="ffn",
    )(xs, mod, norm_g, wg, wu, wd)


def _norm_rope(t, gain, cos, sa, sb, lane):
    s = t * t
    for sh in (1, 2, 4, 8, 16, 32):
        up = pltpu.roll(s, sh, 1)
        dn = pltpu.roll(s, LANES - sh, 1)
        s = s + jnp.where((lane & sh) != 0, up, dn)
    y = t * lax.rsqrt(s * (1.0 / HEAD_DIM) + EPS) * gain
    quarter = HEAD_DIM // 4
    return y * cos + pltpu.roll(y, LANES - quarter, 1) * sa + pltpu.roll(y, quarter, 1) * sb


def _inproj_kernel(x_ref, mod_ref, g_ref, w_ref, qkg_ref, rope_ref,
                   qta_ref, qtc_ref, ka_ref, kc_ref, vta_ref, vtc_ref, b_ref, cu_ref):
    x = x_ref[...]
    h = _rms_mod(x, g_ref[...], mod_ref[3:4, :], mod_ref[4:5, :]).astype(BF16)
    z = jnp.dot(h, w_ref[...], preferred_element_type=F32)
    cos, sa, sb = rope_ref[0], rope_ref[1], rope_ref[2]
    lane = lax.broadcasted_iota(jnp.int32, (x.shape[0], LANES), 1)
    nr = functools.partial(_norm_rope, cos=cos, sa=sa, sb=sb, lane=lane)

    ka_ref[...] = nr(z[:, 0:128], qkg_ref[1:2, :]).astype(BF16)
    vta_ref[...] = z[:, 128:256].T.astype(BF16)
    kc_ref[...] = nr(z[:, 256:384], qkg_ref[3:4, :]).astype(BF16)
    vtc_ref[...] = z[:, 384:512].T.astype(BF16)
    q0 = 4 * KV_COLS
    for c in range(Q_COLS // LANES):
        lo = q0 + c * LANES
        qa = nr(z[:, lo:lo + LANES], qkg_ref[0:1, :] * Q_SCALE)
        qta_ref[c * LANES:(c + 1) * LANES, :] = qa.T.astype(BF16)
        lo = q0 + Q_COLS + c * LANES
        qc = nr(z[:, lo:lo + LANES], qkg_ref[2:3, :] * Q_SCALE)
        qtc_ref[c * LANES:(c + 1) * LANES, :] = qc.T.astype(BF16)
    c0 = q0 + 2 * Q_COLS
    cw = b_ref.shape[-1]
    b_ref[...] = z[:, c0:c0 + cw]
    cu_ref[...] = z[:, c0 + cw:c0 + 2 * cw] * z[:, c0 + 2 * cw:c0 + 3 * cw]


def _inproj(xs, mod, norm_g, w_in, qkg, rope, *, layer):
    b, l, d = xs.shape
    ncol = w_in.shape[-1]
    cw = (ncol - 4 * KV_COLS - 2 * Q_COLS) // 3
    tok = lambda w: pl.BlockSpec((None, TM, w), lambda bi, i: (bi, i, 0))
    tr = lambda w: pl.BlockSpec((None, w, TM), lambda bi, i: (bi, 0, i))
    return pl.pallas_call(
        _inproj_kernel,
        grid=(b, l // TM),
        in_specs=[tok(d),
                  pl.BlockSpec((None, None, N_MOD, d), lambda bi, i: (bi, jnp.minimum(i, 1), 0, 0)),
                  _resident((None, None, 1, d), lambda bi, i: (layer, 1, 0, 0)),
                  _resident((None, d, ncol), lambda bi, i: (layer, 0, 0)),
                  _resident((None, 4, LANES), lambda bi, i: (layer, 0, 0)),
                  pl.BlockSpec((3, TM, LANES), lambda bi, i: (0, i, 0))],
        out_specs=[tr(Q_COLS), tr(Q_COLS), tok(KV_COLS), tok(KV_COLS), tr(KV_COLS), tr(KV_COLS),
                   tok(cw), tok(cw)],
        out_shape=[jax.ShapeDtypeStruct((b, Q_COLS, l), BF16),
                   jax.ShapeDtypeStruct((b, Q_COLS, l), BF16),
                   jax.ShapeDtypeStruct((b, l, KV_COLS), BF16),
                   jax.ShapeDtypeStruct((b, l, KV_COLS), BF16),
                   jax.ShapeDtypeStruct((b, KV_COLS, l), BF16),
                   jax.ShapeDtypeStruct((b, KV_COLS, l), BF16),
                   jax.ShapeDtypeStruct((b, l, cw), F32),
                   jax.ShapeDtypeStruct((b, l, cw), F32)],
        compiler_params=_params(2),
        name="inproj",
    )(xs, mod, norm_g, w_in, qkg, rope)


def _pad_q(q, g):
    z = jnp.zeros_like(q)
    return jnp.concatenate([q, z] if g == 0 else [z, q], axis=0)


def _attend(qt_ref, k_ref, vt_ref, acc_ref, m_ref, l_ref, *, n_keys, tk):
    tq = qt_ref.shape[-1]
    rows = lambda h: slice(h * HEAD_DIM, (h + 1) * HEAD_DIM)
    qs = [_pad_q(qt_ref[rows(h), :], h // GROUP) for h in range(Q_HEADS)]
    m_ref[...] = jnp.full(m_ref.shape, NEG_INF, F32)
    l_ref[...] = jnp.zeros(l_ref.shape, F32)
    acc_ref[...] = jnp.zeros(acc_ref.shape, F32)

    def body(t, _):
        start = pl.multiple_of(t * tk, tk)
        kt = k_ref[pl.ds(start, tk), :]
        s_next = jnp.dot(kt, qs[0], preferred_element_type=F32)
        for h in range(Q_HEADS):
            s = s_next
            if h + 1 < Q_HEADS:
                s_next = jnp.dot(kt, qs[h + 1], preferred_element_type=F32)
            m = m_ref[h:h + 1, :]
            m_new = jnp.maximum(m, jnp.max(s, axis=0, keepdims=True))
            alpha = jnp.exp2(m - m_new)
            p = jnp.exp2(s - m_new)
            m_ref[h:h + 1, :] = m_new
            l_ref[h:h + 1, :] = alpha * l_ref[h:h + 1, :] + jnp.sum(p, axis=0, keepdims=True)
            vt = vt_ref[rows(h // GROUP), pl.ds(start, tk)]
            acc_ref[rows(h), :] = alpha * acc_ref[rows(h), :] + jnp.dot(
                vt, p.astype(BF16), preferred_element_type=F32)
        return 0

    n_kt = n_keys // tk
    if n_kt == 1:
        body(0, 0)
    else:
        lax.fori_loop(0, n_kt, body, 0)
    for h in range(Q_HEADS):
        acc_ref[rows(h), :] = acc_ref[rows(h), :] / l_ref[h:h + 1, :]


def _global_attn_kernel(qt_ref, k_ref, vt_ref, o_ref, acc_ref, m_ref, l_ref, *, tk):
    i = pl.program_id(1)
    attend = functools.partial(_attend, qt_ref, k_ref, vt_ref, acc_ref, m_ref, l_ref)

    @pl.when(i == 0)
    def _():
        attend(n_keys=TM, tk=TM)

    @pl.when(i > 0)
    def _():
        attend(n_keys=k_ref.shape[0], tk=tk)

    o_ref[...] = acc_ref[...].T


def _key_tile(l):
    return next(t for t in (1280, 1024, 768, 512, 256) if l % t == 0)


def _global_attn(qt, k, vt):
    b, _, l = qt.shape
    return pl.pallas_call(
        functools.partial(_global_attn_kernel, tk=_key_tile(l)),
        grid=(b, l // TM),
        in_specs=[pl.BlockSpec((None, Q_COLS, TM), lambda bi, i: (bi, 0, i)),
                  pl.BlockSpec((None, l, KV_COLS), lambda bi, i: (bi, 0, 0)),
                  pl.BlockSpec((None, KV_COLS, l), lambda bi, i: (bi, 0, 0))],
        out_specs=pl.BlockSpec((None, TM, Q_COLS), lambda bi, i: (bi, i, 0)),
        out_shape=jax.ShapeDtypeStruct((b, l, Q_COLS), F32),
        scratch_shapes=[pltpu.VMEM((Q_COLS, TM), F32), pltpu.VMEM((Q_HEADS, TM), F32),
                        pltpu.VMEM((Q_HEADS, TM), F32)],
        compiler_params=_params(2),
        name="global_attn",
    )(qt, k, vt)


def _window_attn_kernel(qt_ref, k_ref, vt_ref, sink_ref, o_ref, acc_ref):
    i = pl.program_id(1)
    tq = qt_ref.shape[-1]
    l_all = k_ref.shape[0]
    span = tq + 2 * WINDOW
    start = pl.multiple_of(jnp.clip(i * tq - WINDOW, 0, l_all - span), LANES)
    kpos = start - TM + lax.broadcasted_iota(jnp.int32, (span, 1), 0)
    qbase = jnp.where(i >= 1, i * tq - TM, -(1 << 20))
    qpos = qbase + lax.broadcasted_iota(jnp.int32, (1, tq), 1)
    ok = jnp.logical_and(kpos >= 0, jnp.abs(qpos - kpos) <= WINDOW)
    bias = jnp.where(ok, 0.0, NEG_INF).astype(F32)
    k_ctx = k_ref[0:TM, :]
    k_win = k_ref[pl.ds(start, span), :]
    for h in range(Q_HEADS):
        g = h // GROUP
        q = _pad_q(qt_ref[h * HEAD_DIM:(h + 1) * HEAD_DIM, :], g)
        s1 = jnp.dot(k_ctx, q, preferred_element_type=F32)
        s2 = jnp.dot(k_win, q, preferred_element_type=F32) + bias
        sink = sink_ref[h:h + 1, :]
        m = jnp.maximum(jnp.maximum(jnp.max(s1, axis=0, keepdims=True),
                                    jnp.max(s2, axis=0, keepdims=True)), sink)
        p1 = jnp.exp2(s1 - m)
        p2 = jnp.exp2(s2 - m)
        l = (jnp.sum(p1, axis=0, keepdims=True) + jnp.sum(p2, axis=0, keepdims=True)
             + jnp.exp2(sink - m))
        rows = slice(g * HEAD_DIM, (g + 1) * HEAD_DIM)
        acc = (jnp.dot(vt_ref[rows, 0:TM], p1.astype(BF16), preferred_element_type=F32)
               + jnp.dot(vt_ref[rows, pl.ds(start, span)], p2.astype(BF16),
                         preferred_element_type=F32))
        acc_ref[h * HEAD_DIM:(h + 1) * HEAD_DIM, :] = acc / l
    o_ref[...] = acc_ref[...].T


def _window_attn(qt, k, vt, sink):
    b, _, l = qt.shape
    sink_b = jnp.broadcast_to((sink.astype(F32) * LOG2E)[:, None], (Q_HEADS, TM))
    return pl.pallas_call(
        _window_attn_kernel,
        grid=(b, l // TM),
        in_specs=[pl.BlockSpec((None, Q_COLS, TM), lambda bi, i: (bi, 0, i)),
                  pl.BlockSpec((None, l, KV_COLS), lambda bi, i: (bi, 0, 0)),
                  pl.BlockSpec((None, KV_COLS, l), lambda bi, i: (bi, 0, 0)),
                  pl.BlockSpec((Q_HEADS, TM), lambda bi, i: (0, 0))],
        out_specs=pl.BlockSpec((None, TM, Q_COLS), lambda bi, i: (bi, i, 0)),
        out_shape=jax.ShapeDtypeStruct((b, l, Q_COLS), F32),
        scratch_shapes=[pltpu.VMEM((Q_COLS, TM), F32)],
        compiler_params=_params(2),
        name="window_attn",
    )(qt, k, vt, sink_b)


def _merge_kernel(x_ref, mod_ref, g_ref, oa_ref, oc_ref, b_ref, cu_ref, cup_ref, cun_ref, cw_ref,
                  wgate_ref, bgate_ref, wpa_ref, wpb_ref, wpc_ref, wo_ref, o_ref):
    i = pl.program_id(1)
    n_tiles = pl.num_programs(1)
    x = x_ref[...]
    tm, d = x.shape
    h = _rms_mod(x, g_ref[...], mod_ref[3:4, :], mod_ref[4:5, :]).astype(BF16)
    gates = _sigmoid(jnp.dot(h, wgate_ref[...], preferred_element_type=F32) + bgate_ref[...])

    cu = cu_ref[...]
    has_prev = (i >= 2).astype(F32)
    has_next = jnp.logical_and(i >= 1, i < n_tiles - 1).astype(F32)
    prev_row = cup_ref[7:8, :] * has_prev
    next_row = cun_ref[0:1, :] * has_next
    row = lax.broadcasted_iota(jnp.int32, (tm, 1), 0)
    cu_dn = jnp.where(row == 0, prev_row, pltpu.roll(cu, 1, 0))
    cu_up = jnp.where(row == tm - 1, next_row, pltpu.roll(cu, tm - 1, 0))
    y = cw_ref[0:1, :] * cu_dn + cw_ref[1:2, :] * cu + cw_ref[2:3, :] * cu_up
    o_b = (b_ref[...] * y).astype(BF16)

    pa = jnp.dot(oa_ref[...].astype(BF16), wpa_ref[...], preferred_element_type=F32)
    pb = jnp.dot(o_b, wpb_ref[...], preferred_element_type=F32)
    pc = jnp.dot(oc_ref[...].astype(BF16), wpc_ref[...], preferred_element_type=F32)
    mix = gates[:, 0:d] * pa + gates[:, d:2 * d] * pb + gates[:, 2 * d:3 * d] * pc
    out = jnp.dot(mix.astype(BF16), wo_ref[...], preferred_element_type=F32)
    o_ref[...] = x + mod_ref[5:6, :] * out


def _merge(xs, mod, norm_g, o_a, o_c, bb, cu, conv_w, w_gate, b_gate, w_pa, w_pb, w_pc, w_o,
           *, layer):
    b, l, d = xs.shape
    cw = cu.shape[-1]
    sub = 8
    tok = lambda w: pl.BlockSpec((None, TM, w), lambda bi, i: (bi, i, 0))
    lay = lambda *s: _resident((None,) + s, lambda bi, i: (layer,) + (0,) * len(s))
    return pl.pallas_call(
        _merge_kernel,
        grid=(b, l // TM),
        in_specs=[tok(d),
                  pl.BlockSpec((None, None, N_MOD, d), lambda bi, i: (bi, jnp.minimum(i, 1), 0, 0)),
                  _resident((None, None, 1, d), lambda bi, i: (layer, 1, 0, 0)),
                  tok(Q_COLS), tok(Q_COLS), tok(cw), tok(cw),
                  pl.BlockSpec((None, sub, cw),
                               lambda bi, i: (bi, jnp.maximum(i * (TM // sub) - 1, 0), 0)),
                  pl.BlockSpec((None, sub, cw),
                               lambda bi, i: (bi, jnp.minimum((i + 1) * (TM // sub), l // sub - 1), 0)),
                  lay(3, cw), lay(d, 3 * d), lay(1, 3 * d), lay(Q_COLS, d), lay(cw, d),
                  lay(Q_COLS, d), lay(d, d)],
        out_specs=tok(d),
        out_shape=jax.ShapeDtypeStruct(xs.shape, F32),
        input_output_aliases={0: 0},
        compiler_params=_params(2),
        name="merge",
    )(xs, mod, norm_g, o_a, o_c, bb, cu, cu, cu, conv_w, w_gate, b_gate, w_pa, w_pb, w_pc, w_o)


def _rope_tables(s, n_ctx):
    rows = s // GRID_W
    row = jnp.repeat(jnp.arange(rows), GRID_W).astype(F32)
    col = jnp.tile(jnp.arange(GRID_W), rows).astype(F32)
    half = HEAD_DIM // 2
    inv = ROPE_THETA ** (-jnp.arange(0, half, 2, dtype=F32) / half)
    ang_r = row[:, None] * inv
    ang_c = col[:, None] * inv
    cos = jnp.concatenate([jnp.cos(ang_r)] * 2 + [jnp.cos(ang_c)] * 2, axis=-1)
    sin = jnp.concatenate([jnp.sin(ang_r)] * 2 + [jnp.sin(ang_c)] * 2, axis=-1)
    q = HEAD_DIM // 4
    first = (jnp.arange(HEAD_DIM) // q) % 2 == 0
    sa = jnp.where(first, -sin, 0.0)
    sb = jnp.where(first, 0.0, sin)
    tabs = jnp.stack([cos, sa, sb])
    ident = jnp.stack([jnp.ones((n_ctx, HEAD_DIM), F32), jnp.zeros((n_ctx, HEAD_DIM), F32),
                       jnp.zeros((n_ctx, HEAD_DIM), F32)])
    tabs = jnp.concatenate([ident, tabs], axis=1)
    return jnp.tile(tabs, (1, 1, LANES // HEAD_DIM))


def kernel(x, c, ctx, c_ctx, w_ada, b_ada, norm_g, ffn_w_gate, ffn_w_up, ffn_w_down, w_in, qk_g,
           sink_a, conv_w, w_pa, w_pb, w_pc, w_gate, b_gate, w_o):
    bsz, s, d = x.shape
    n_ctx = ctx.shape[1]
    depth = w_ada.shape[0]
    assert n_ctx == TM and s % TM == 0 and s % GRID_W == 0 and bsz < 8

    cvec = jnp.zeros((8, d), F32).at[:bsz].set(c).at[bsz].set(c_ctx)
    mods = _ada(cvec, w_ada, b_ada).reshape(depth, 8, N_MOD, d)
    lat = mods[:, :bsz]
    con = jnp.broadcast_to(mods[:, bsz:bsz + 1], lat.shape)
    mod_all = jnp.stack([con, lat], axis=2)

    rope = _rope_tables(s, n_ctx)
    qkg = jnp.tile(qk_g.astype(F32), (1, 1, LANES // HEAD_DIM))
    ng = norm_g.astype(F32)[:, :, None, :]
    wg, wu, wd = (w.astype(BF16) for w in (ffn_w_gate, ffn_w_up, ffn_w_down))
    w_in_b, w_gate_b, w_pa_b, w_pb_b, w_pc_b, w_o_b = (
        w.astype(BF16) for w in (w_in, w_gate, w_pa, w_pb, w_pc, w_o))
    b_gate3 = b_gate.astype(F32)[:, None, :]

    xs = jnp.concatenate([ctx, x], axis=1)
    for i in range(depth):
        mod = mod_all[i]
        xs = _ffn(xs, mod, ng, wg, wu, wd, layer=i, which=0, j0=0)
        qta, qtc, ka, kc, vta, vtc, bb, cu = _inproj(xs, mod, ng, w_in_b, qkg, rope, layer=i)
        o_a = _window_attn(qta, ka, vta, sink_a[i])
        o_c = _global_attn(qtc, kc, vtc)
        xs = _merge(xs, mod, ng, o_a, o_c, bb, cu, conv_w.astype(F32), w_gate_b, b_gate3,
                    w_pa_b, w_pb_b, w_pc_b, w_o_b, layer=i)
        xs = _ffn(xs, mod, ng, wg, wu, wd, layer=i, which=1, j0=6)
    return xs[:, n_ctx:, :]
```

```python
import functools

import jax
import jax.numpy as jnp
from jax import lax
from jax.experimental import pallas as pl
from jax.experimental.pallas import tpu as pltpu

GRID_W = 64
HEAD_DIM = 64
Q_HEADS = 8
KV_HEADS = 2
GROUP = Q_HEADS // KV_HEADS
WINDOW = 128
N_MOD = 9
ROPE_THETA = 10000.0
EPS = 1e-6
NEG_INF = -1e30
SCALE = HEAD_DIM ** -0.5
LOG2E = 1.4426950408889634
Q_SCALE = SCALE * LOG2E
Q_COLS = Q_HEADS * HEAD_DIM
KV_COLS = KV_HEADS * HEAD_DIM
VT_ROWS = HEAD_DIM + 16

LANES = 128
TM = 256
CHUNK = 256
VMEM_LIMIT = 56 * 1024 * 1024

F32 = jnp.float32
BF16 = jnp.bfloat16


def _params(n_axes, flags=None):
    return pltpu.CompilerParams(dimension_semantics=("arbitrary",) * n_axes,
                                vmem_limit_bytes=VMEM_LIMIT, flags=flags)


def _resident(shape, index_map):
    return pl.BlockSpec(shape, index_map, pipeline_mode=pl.Buffered(1))


def _sigmoid(v):
    return 1.0 / (1.0 + jnp.exp(-v))


def _rms_mod(x, g, shift, scale):
    y = x * lax.rsqrt(jnp.mean(x * x, axis=-1, keepdims=True) + EPS)
    return (y * g) * (1.0 + scale) + shift


def _ada_kernel(c_ref, w_ref, b_ref, o_ref):
    c = c_ref[...]
    s = (c * _sigmoid(c)).astype(BF16)
    o_ref[...] = jnp.dot(s, w_ref[...].astype(BF16), preferred_element_type=F32) + b_ref[...]


def _ada(cvec, w_ada, b_ada):
    depth, d, n = w_ada.shape
    tn = 1024
    return pl.pallas_call(
        _ada_kernel,
        grid=(depth, n // tn),
        in_specs=[pl.BlockSpec((8, d), lambda l, j: (0, 0)),
                  pl.BlockSpec((None, d, tn), lambda l, j: (l, 0, j)),
                  pl.BlockSpec((None, 1, tn), lambda l, j: (l, 0, j))],
        out_specs=pl.BlockSpec((None, 8, tn), lambda l, j: (l, 0, j)),
        out_shape=jax.ShapeDtypeStruct((depth, 8, n), F32),
        compiler_params=_params(2),
        name="ada",
    )(cvec, w_ada, b_ada.reshape(depth, 1, n))


def _ffn_kernel(x_ref, mod_ref, g_ref, wg_ref, wu_ref, wd_ref, o_ref, *, j0):
    x = x_ref[...]
    h = _rms_mod(x, g_ref[...], mod_ref[j0:j0 + 1, :], mod_ref[j0 + 1:j0 + 2, :]).astype(BF16)
    a = jnp.dot(h, wg_ref[...], preferred_element_type=F32)
    u = jnp.dot(h, wu_ref[...], preferred_element_type=F32)
    act = (a * _sigmoid(a) * u).astype(BF16)
    y = jnp.dot(act, wd_ref[...], preferred_element_type=F32)
    o_ref[...] = x + (0.5 * mod_ref[j0 + 2:j0 + 3, :]) * y


def _ffn(xs, mod, norm_g, wg, wu, wd, *, layer, which, j0):
    b, l, d = xs.shape
    dff = wg.shape[-1]
    x_spec = pl.BlockSpec((None, TM, d), lambda bi, i: (bi, i, 0))
    return pl.pallas_call(
        functools.partial(_ffn_kernel, j0=j0),
        grid=(b, l // TM),
        in_specs=[x_spec,
                  pl.BlockSpec((None, None, N_MOD, d), lambda bi, i: (bi, jnp.minimum(i, 1), 0, 0)),
                  _resident((None, None, 1, d), lambda bi, i: (layer, 2 * which, 0, 0)),
                  _resident((None, None, d, dff), lambda bi, i: (layer, which, 0, 0)),
                  _resident((None, None, d, dff), lambda bi, i: (layer, which, 0, 0)),
                  _resident((None, None, dff, d), lambda bi, i: (layer, which, 0, 0))],
        out_specs=x_spec,
        out_shape=jax.ShapeDtypeStruct(xs.shape, F32),
        input_output_aliases={0: 0},
        compiler_params=_params(2),
        name="ffn",
    )(xs, mod, norm_g, wg, wu, wd)


def _norm_rope(t, gain, cos, sa, sb, lane):
    s = t * t
    for sh in (1, 2, 4, 8, 16, 32):
        up = pltpu.roll(s, sh, 1)
        dn = pltpu.roll(s, LANES - sh, 1)
        s = s + jnp.where((lane & sh) != 0, up, dn)
    y = t * lax.rsqrt(s * (1.0 / HEAD_DIM) + EPS) * gain
    quarter = HEAD_DIM // 4
    return y * cos + pltpu.roll(y, LANES - quarter, 1) * sa + pltpu.roll(y, quarter, 1) * sb


def _inproj_kernel(x_ref, mod_ref, g_ref, w_ref, qkg_ref, rope_ref,
                   qta_ref, qtc_ref, ka_ref, kc_ref, vta_ref, vtc_ref, b_ref, cu_ref):
    x = x_ref[...]
    h = _rms_mod(x, g_ref[...], mod_ref[3:4, :], mod_ref[4:5, :]).astype(BF16)
    z = jnp.dot(h, w_ref[...], preferred_element_type=F32)
    cos, sa, sb = rope_ref[0], rope_ref[1], rope_ref[2]
    lane = lax.broadcasted_iota(jnp.int32, (x.shape[0], LANES), 1)
    nr = functools.partial(_norm_rope, cos=cos, sa=sa, sb=sb, lane=lane)

    ka_ref[...] = nr(z[:, 0:128], qkg_ref[1:2, :]).astype(BF16)
    vta_ref[...] = z[:, 128:256].T.astype(BF16)
    kc_ref[...] = nr(z[:, 256:384], qkg_ref[3:4, :]).astype(BF16)
    vtc = z[:, 384:512].T.astype(BF16)
    ones = jnp.ones((VT_ROWS - HEAD_DIM, x.shape[0]), BF16)
    for g in range(KV_HEADS):
        vtc_ref[g * VT_ROWS:g * VT_ROWS + HEAD_DIM, :] = vtc[g * HEAD_DIM:(g + 1) * HEAD_DIM, :]
        vtc_ref[g * VT_ROWS + HEAD_DIM:(g + 1) * VT_ROWS, :] = ones
    q0 = 4 * KV_COLS
    for c in range(Q_COLS // LANES):
        lo = q0 + c * LANES
        qa = nr(z[:, lo:lo + LANES], qkg_ref[0:1, :] * Q_SCALE)
        qta_ref[c * LANES:(c + 1) * LANES, :] = qa.T.astype(BF16)
        lo = q0 + Q_COLS + c * LANES
        qc = nr(z[:, lo:lo + LANES], qkg_ref[2:3, :] * Q_SCALE)
        qtc_ref[c * LANES:(c + 1) * LANES, :] = qc.T.astype(BF16)
    c0 = q0 + 2 * Q_COLS
    cw = b_ref.shape[-1]
    b_ref[...] = z[:, c0:c0 + cw]
    cu_ref[...] = z[:, c0 + cw:c0 + 2 * cw] * z[:, c0 + 2 * cw:c0 + 3 * cw]


def _inproj(xs, mod, norm_g, w_in, qkg, rope, *, layer):
    b, l, d = xs.shape
    ncol = w_in.shape[-1]
    cw = (ncol - 4 * KV_COLS - 2 * Q_COLS) // 3
    tok = lambda w: pl.BlockSpec((None, TM, w), lambda bi, i: (bi, i, 0))
    tr = lambda w: pl.BlockSpec((None, w, TM), lambda bi, i: (bi, 0, i))
    return pl.pallas_call(
        _inproj_kernel,
        grid=(b, l // TM),
        in_specs=[tok(d),
                  pl.BlockSpec((None, None, N_MOD, d), lambda bi, i: (bi, jnp.minimum(i, 1), 0, 0)),
                  _resident((None, None, 1, d), lambda bi, i: (layer, 1, 0, 0)),
                  _resident((None, d, ncol), lambda bi, i: (layer, 0, 0)),
                  _resident((None, 4, LANES), lambda bi, i: (layer, 0, 0)),
                  pl.BlockSpec((3, TM, LANES), lambda bi, i: (0, i, 0))],
        out_specs=[tr(Q_COLS), tr(Q_COLS), tok(KV_COLS), tok(KV_COLS), tr(KV_COLS),
                   tr(KV_HEADS * VT_ROWS), tok(cw), tok(cw)],
        out_shape=[jax.ShapeDtypeStruct((b, Q_COLS, l), BF16),
                   jax.ShapeDtypeStruct((b, Q_COLS, l), BF16),
                   jax.ShapeDtypeStruct((b, l, KV_COLS), BF16),
                   jax.ShapeDtypeStruct((b, l, KV_COLS), BF16),
                   jax.ShapeDtypeStruct((b, KV_COLS, l), BF16),
                   jax.ShapeDtypeStruct((b, KV_HEADS * VT_ROWS, l), BF16),
                   jax.ShapeDtypeStruct((b, l, cw), F32),
                   jax.ShapeDtypeStruct((b, l, cw), F32)],
        compiler_params=_params(2),
        name="inproj",
    )(xs, mod, norm_g, w_in, qkg, rope)


def _pad_q(q, g):
    z = jnp.zeros_like(q)
    return jnp.concatenate([q, z] if g == 0 else [z, q], axis=0)


def _attend(qt_ref, k_ref, vt_ref, acc_ref, m_ref, al_ref, smax_ref, s_bufs, p_bufs, *, n_keys, tk):
    n_kt = n_keys // tk
    qrows = lambda h: slice(h * HEAD_DIM, (h + 1) * HEAD_DIM)
    arows = lambda h: slice(h * VT_ROWS, (h + 1) * VT_ROWS)
    qs = [_pad_q(qt_ref[qrows(h), :], h // GROUP) for h in range(Q_HEADS)]
    m_ref[...] = jnp.full(m_ref.shape, NEG_INF, F32)
    acc_ref[...] = jnp.zeros(acc_ref.shape, F32)

    ch = min(tk, CHUNK)
    n_ch = tk // ch

    def key_start(t, c):
        return t * tk + c * ch if isinstance(t, int) else pl.multiple_of(t * tk + c * ch, ch)

    def step(sc=None, sm=None, va=None):
        if sm is not None:
            m = m_ref[sm:sm + 1, :]
            m_new = jnp.maximum(m, smax_ref[sm:sm + 1, :])
            m_ref[sm:sm + 1, :] = m_new
            al_ref[sm:sm + 1, :] = jnp.exp2(m - m_new)
        smax = pv = None
        for c in range(n_ch):
            rows = slice(c * ch, (c + 1) * ch)
            if sc is not None:
                t, h = sc
                s = jnp.dot(k_ref[pl.ds(key_start(t, c), ch), :], qs[h],
                            preferred_element_type=F32)
                s_bufs[h % 2][rows, :] = s
                cmax = jnp.max(s, axis=0, keepdims=True)
                smax = cmax if smax is None else jnp.maximum(smax, cmax)
            if sm is not None:
                p_bufs[sm % 2][rows, :] = jnp.exp2(s_bufs[sm % 2][rows, :] - m_new).astype(BF16)
            if va is not None:
                t, h = va
                g = h // GROUP
                vt = vt_ref[g * VT_ROWS:(g + 1) * VT_ROWS, pl.ds(key_start(t, c), ch)]
                d = jnp.dot(vt, p_bufs[h % 2][rows, :], preferred_element_type=F32)
                pv = d if pv is None else pv + d
        if sc is not None:
            smax_ref[sc[1]:sc[1] + 1, :] = smax
        if va is not None:
            h = va[1]
            acc_ref[arows(h), :] = al_ref[h:h + 1, :] * acc_ref[arows(h), :] + pv

    def tile_steps(t, last):
        for h in range(Q_HEADS):
            if h + 2 < Q_HEADS:
                sc = (t, h + 2)
            else:
                sc = None if last else (t + 1, h + 2 - Q_HEADS)
            if h + 1 < Q_HEADS:
                sm = h + 1
            else:
                sm = None if last else 0
            step(sc, sm, (t, h))

    step(sc=(0, 0))
    step(sc=(0, 1), sm=0)
    if n_kt > 1:
        def body(t, carry):
            tile_steps(t, False)
            return carry
        lax.fori_loop(0, n_kt - 1, body, 0)
    tile_steps(n_kt - 1, True)


def _global_attn_kernel(qt_ref, k_ref, vt_ref, o_ref, acc_ref, m_ref, al_ref, smax_ref,
                        s0_ref, s1_ref, p0_ref, p1_ref, *, tk):
    i = pl.program_id(1)
    attend = functools.partial(_attend, qt_ref, k_ref, vt_ref, acc_ref, m_ref, al_ref, smax_ref,
                               (s0_ref, s1_ref), (p0_ref, p1_ref))

    @pl.when(i == 0)
    def _():
        attend(n_keys=TM, tk=TM)

    @pl.when(i > 0)
    def _():
        attend(n_keys=k_ref.shape[0], tk=tk)

    outs = []
    for h in range(Q_HEADS):
        a = acc_ref[h * VT_ROWS:(h + 1) * VT_ROWS, :]
        outs.append(a[0:HEAD_DIM, :] / a[HEAD_DIM:HEAD_DIM + 1, :])
    o_ref[...] = jnp.concatenate(outs, axis=0).T


def _key_tile(l):
    return next(t for t in (1280, 1024, 768, 512, 256) if l % t == 0)


def _global_attn(qt, k, vt):
    b, _, l = qt.shape
    tk = _key_tile(l)
    stat = pltpu.VMEM((Q_HEADS, TM), F32)
    return pl.pallas_call(
        functools.partial(_global_attn_kernel, tk=tk),
        grid=(b, l // TM),
        in_specs=[pl.BlockSpec((None, Q_COLS, TM), lambda bi, i: (bi, 0, i)),
                  pl.BlockSpec((None, l, KV_COLS), lambda bi, i: (bi, 0, 0)),
                  pl.BlockSpec((None, KV_HEADS * VT_ROWS, l), lambda bi, i: (bi, 0, 0))],
        out_specs=pl.BlockSpec((None, TM, Q_COLS), lambda bi, i: (bi, i, 0)),
        out_shape=jax.ShapeDtypeStruct((b, l, Q_COLS), F32),
        scratch_shapes=[pltpu.VMEM((Q_HEADS * VT_ROWS, TM), F32), stat, stat, stat,
                        pltpu.VMEM((tk, TM), F32), pltpu.VMEM((tk, TM), F32),
                        pltpu.VMEM((tk, TM), BF16), pltpu.VMEM((tk, TM), BF16)],
        compiler_params=_params(2),
        name="global_attn",
    )(qt, k, vt)


def _window_attn_kernel(qt_ref, k_ref, vt_ref, sink_ref, o_ref, acc_ref):
    i = pl.program_id(1)
    tq = qt_ref.shape[-1]
    l_all = k_ref.shape[0]
    span = tq + 2 * WINDOW
    start = pl.multiple_of(jnp.clip(i * tq - WINDOW, 0, l_all - span), LANES)
    kpos = start - TM + lax.broadcasted_iota(jnp.int32, (span, 1), 0)
    qbase = jnp.where(i >= 1, i * tq - TM, -(1 << 20))
    qpos = qbase + lax.broadcasted_iota(jnp.int32, (1, tq), 1)
    ok = jnp.logical_and(kpos >= 0, jnp.abs(qpos - kpos) <= WINDOW)
    bias = jnp.where(ok, 0.0, NEG_INF).astype(F32)
    k_ctx = k_ref[0:TM, :]
    k_win = k_ref[pl.ds(start, span), :]
    for h in range(Q_HEADS):
        g = h // GROUP
        q = _pad_q(qt_ref[h * HEAD_DIM:(h + 1) * HEAD_DIM, :], g)
        s1 = jnp.dot(k_ctx, q, preferred_element_type=F32)
        s2 = jnp.dot(k_win, q, preferred_element_type=F32) + bias
        sink = sink_ref[h:h + 1, :]
        m = jnp.maximum(jnp.maximum(jnp.max(s1, axis=0, keepdims=True),
                                    jnp.max(s2, axis=0, keepdims=True)), sink)
        p1 = jnp.exp2(s1 - m)
        p2 = jnp.exp2(s2 - m)
        l = (jnp.sum(p1, axis=0, keepdims=True) + jnp.sum(p2, axis=0, keepdims=True)
             + jnp.exp2(sink - m))
        rows = slice(g * HEAD_DIM, (g + 1) * HEAD_DIM)
        acc = (jnp.dot(vt_ref[rows, 0:TM], p1.astype(BF16), preferred_element_type=F32)
               + jnp.dot(vt_ref[rows, pl.ds(start, span)], p2.astype(BF16),
                         preferred_element_type=F32))
        acc_ref[h * HEAD_DIM:(h + 1) * HEAD_DIM, :] = acc / l
    o_ref[...] = acc_ref[...].T


def _window_attn(qt, k, vt, sink):
    b, _, l = qt.shape
    sink_b = jnp.broadcast_to((sink.astype(F32) * LOG2E)[:, None], (Q_HEADS, TM))
    return pl.pallas_call(
        _window_attn_kernel,
        grid=(b, l // TM),
        in_specs=[pl.BlockSpec((None, Q_COLS, TM), lambda bi, i: (bi, 0, i)),
                  pl.BlockSpec((None, l, KV_COLS), lambda bi, i: (bi, 0, 0)),
                  pl.BlockSpec((None, KV_COLS, l), lambda bi, i: (bi, 0, 0)),
                  pl.BlockSpec((Q_HEADS, TM), lambda bi, i: (0, 0))],
        out_specs=pl.BlockSpec((None, TM, Q_COLS), lambda bi, i: (bi, i, 0)),
        out_shape=jax.ShapeDtypeStruct((b, l, Q_COLS), F32),
        scratch_shapes=[pltpu.VMEM((Q_COLS, TM), F32)],
        compiler_params=_params(2),
        name="window_attn",
    )(qt, k, vt, sink_b)


def _merge_kernel(x_ref, mod_ref, g_ref, oa_ref, oc_ref, b_ref, cu_ref, cup_ref, cun_ref, cw_ref,
                  wgate_ref, bgate_ref, wpa_ref, wpb_ref, wpc_ref, wo_ref, o_ref):
    i = pl.program_id(1)
    n_tiles = pl.num_programs(1)
    x = x_ref[...]
    tm, d = x.shape
    h = _rms_mod(x, g_ref[...], mod_ref[3:4, :], mod_ref[4:5, :]).astype(BF16)
    gates = _sigmoid(jnp.dot(h, wgate_ref[...], preferred_element_type=F32) + bgate_ref[...])

    cu = cu_ref[...]
    has_prev = (i >= 2).astype(F32)
    has_next = jnp.logical_and(i >= 1, i < n_tiles - 1).astype(F32)
    prev_row = cup_ref[7:8, :] * has_prev
    next_row = cun_ref[0:1, :] * has_next
    row = lax.broadcasted_iota(jnp.int32, (tm, 1), 0)
    cu_dn = jnp.where(row == 0, prev_row, pltpu.roll(cu, 1, 0))
    cu_up = jnp.where(row == tm - 1, next_row, pltpu.roll(cu, tm - 1, 0))
    y = cw_ref[0:1, :] * cu_dn + cw_ref[1:2, :] * cu + cw_ref[2:3, :] * cu_up
    o_b = (b_ref[...] * y).astype(BF16)

    pa = jnp.dot(oa_ref[...].astype(BF16), wpa_ref[...], preferred_element_type=F32)
    pb = jnp.dot(o_b, wpb_ref[...], preferred_element_type=F32)
    pc = jnp.dot(oc_ref[...].astype(BF16), wpc_ref[...], preferred_element_type=F32)
    mix = gates[:, 0:d] * pa + gates[:, d:2 * d] * pb + gates[:, 2 * d:3 * d] * pc
    out = jnp.dot(mix.astype(BF16), wo_ref[...], preferred_element_type=F32)
    o_ref[...] = x + mod_ref[5:6, :] * out


def _merge(xs, mod, norm_g, o_a, o_c, bb, cu, conv_w, w_gate, b_gate, w_pa, w_pb, w_pc, w_o,
           *, layer):
    b, l, d = xs.shape
    cw = cu.shape[-1]
    sub = 8
    tok = lambda w: pl.BlockSpec((None, TM, w), lambda bi, i: (bi, i, 0))
    lay = lambda *s: _resident((None,) + s, lambda bi, i: (layer,) + (0,) * len(s))
    return pl.pallas_call(
        _merge_kernel,
        grid=(b, l // TM),
        in_specs=[tok(d),
                  pl.BlockSpec((None, None, N_MOD, d), lambda bi, i: (bi, jnp.minimum(i, 1), 0, 0)),
                  _resident((None, None, 1, d), lambda bi, i: (layer, 1, 0, 0)),
                  tok(Q_COLS), tok(Q_COLS), tok(cw), tok(cw),
                  pl.BlockSpec((None, sub, cw),
                               lambda bi, i: (bi, jnp.maximum(i * (TM // sub) - 1, 0), 0)),
                  pl.BlockSpec((None, sub, cw),
                               lambda bi, i: (bi, jnp.minimum((i + 1) * (TM // sub), l // sub - 1), 0)),
                  lay(3, cw), lay(d, 3 * d), lay(1, 3 * d), lay(Q_COLS, d), lay(cw, d),
                  lay(Q_COLS, d), lay(d, d)],
        out_specs=tok(d),
        out_shape=jax.ShapeDtypeStruct(xs.shape, F32),
        input_output_aliases={0: 0},
        compiler_params=_params(2),
        name="merge",
    )(xs, mod, norm_g, o_a, o_c, bb, cu, cu, cu, conv_w, w_gate, b_gate, w_pa, w_pb, w_pc, w_o)


def _rope_tables(s, n_ctx):
    rows = s // GRID_W
    row = jnp.repeat(jnp.arange(rows), GRID_W).astype(F32)
    col = jnp.tile(jnp.arange(GRID_W), rows).astype(F32)
    half = HEAD_DIM // 2
    inv = ROPE_THETA ** (-jnp.arange(0, half, 2, dtype=F32) / half)
    ang_r = row[:, None] * inv
    ang_c = col[:, None] * inv
    cos = jnp.concatenate([jnp.cos(ang_r)] * 2 + [jnp.cos(ang_c)] * 2, axis=-1)
    sin = jnp.concatenate([jnp.sin(ang_r)] * 2 + [jnp.sin(ang_c)] * 2, axis=-1)
    q = HEAD_DIM // 4
    first = (jnp.arange(HEAD_DIM) // q) % 2 == 0
    sa = jnp.where(first, -sin, 0.0)
    sb = jnp.where(first, 0.0, sin)
    tabs = jnp.stack([cos, sa, sb])
    ident = jnp.stack([jnp.ones((n_ctx, HEAD_DIM), F32), jnp.zeros((n_ctx, HEAD_DIM), F32),
                       jnp.zeros((n_ctx, HEAD_DIM), F32)])
    tabs = jnp.concatenate([ident, tabs], axis=1)
    return jnp.tile(tabs, (1, 1, LANES // HEAD_DIM))


def kernel(x, c, ctx, c_ctx, w_ada, b_ada, norm_g, ffn_w_gate, ffn_w_up, ffn_w_down, w_in, qk_g,
           sink_a, conv_w, w_pa, w_pb, w_pc, w_gate, b_gate, w_o):
    bsz, s, d = x.shape
    n_ctx = ctx.shape[1]
    depth = w_ada.shape[0]
    assert n_ctx == TM and s % TM == 0 and s % GRID_W == 0 and bsz < 8

    cvec = jnp.zeros((8, d), F32).at[:bsz].set(c).at[bsz].set(c_ctx)
    mods = _ada(cvec, w_ada, b_ada).reshape(depth, 8, N_MOD, d)
    lat = mods[:, :bsz]
    con = jnp.broadcast_to(mods[:, bsz:bsz + 1], lat.shape)
    mod_all = jnp.stack([con, lat], axis=2)

    rope = _rope_tables(s, n_ctx)
    qkg = jnp.tile(qk_g.astype(F32), (1, 1, LANES // HEAD_DIM))
    ng = norm_g.astype(F32)[:, :, None, :]
    wg, wu, wd = (w.astype(BF16) for w in (ffn_w_gate, ffn_w_up, ffn_w_down))
    w_in_b, w_gate_b, w_pa_b, w_pb_b, w_pc_b, w_o_b = (
        w.astype(BF16) for w in (w_in, w_gate, w_pa, w_pb, w_pc, w_o))
    b_gate3 = b_gate.astype(F32)[:, None, :]

    xs = jnp.concatenate([ctx, x], axis=1)
    for i in range(depth):
        mod = mod_all[i]
        xs = _ffn(xs, mod, ng, wg, wu, wd, layer=i, which=0, j0=0)
        qta, qtc, ka, kc, vta, vtc, bb, cu = _inproj(xs, mod, ng, w_in_b, qkg, rope, layer=i)
        o_a = _window_attn(qta, ka, vta, sink_a[i])
        o_c = _global_attn(qtc, kc, vtc)
        xs = _merge(xs, mod, ng, o_a, o_c, bb, cu, conv_w.astype(F32), w_gate_b, b_gate3,
                    w_pa_b, w_pb_b, w_pc_b, w_o_b, layer=i)
        xs = _ffn(xs, mod, ng, wg, wu, wd, layer=i, which=1, j0=6)
    return xs[:, n_ctx:, :]
```

```python
import functools

import jax
import jax.numpy as jnp
from jax import lax
from jax.experimental import pallas as pl
from jax.experimental.pallas import tpu as pltpu

GRID_W = 64
HEAD_DIM = 64
Q_HEADS = 8
KV_HEADS = 2
GROUP = Q_HEADS // KV_HEADS
WINDOW = 128
N_MOD = 9
ROPE_THETA = 10000.0
EPS = 1e-6
NEG_INF = -1e30
SCALE = HEAD_DIM ** -0.5
LOG2E = 1.4426950408889634
Q_SCALE = SCALE * LOG2E
SCORE_BOUND = 48.0
Q_COLS = Q_HEADS * HEAD_DIM
KV_COLS = KV_HEADS * HEAD_DIM
VT_ROWS = HEAD_DIM + 16

LANES = 128
TM = 256
CHUNK = 256
VMEM_LIMIT = 56 * 1024 * 1024

F32 = jnp.float32
BF16 = jnp.bfloat16


def _params(n_axes, flags=None):
    return pltpu.CompilerParams(dimension_semantics=("arbitrary",) * n_axes,
                                vmem_limit_bytes=VMEM_LIMIT, flags=flags)


def _resident(shape, index_map):
    return pl.BlockSpec(shape, index_map, pipeline_mode=pl.Buffered(1))


def _sigmoid(v):
    return 1.0 / (1.0 + jnp.exp(-v))


def _rms_mod(x, g, shift, scale):
    y = x * lax.rsqrt(jnp.mean(x * x, axis=-1, keepdims=True) + EPS)
    return (y * g) * (1.0 + scale) + shift


def _ada_kernel(c_ref, w_ref, b_ref, o_ref):
    c = c_ref[...]
    s = (c * _sigmoid(c)).astype(BF16)
    o_ref[...] = jnp.dot(s, w_ref[...].astype(BF16), preferred_element_type=F32) + b_ref[...]


def _ada(cvec, w_ada, b_ada):
    depth, d, n = w_ada.shape
    tn = 1024
    return pl.pallas_call(
        _ada_kernel,
        grid=(depth, n // tn),
        in_specs=[pl.BlockSpec((8, d), lambda l, j: (0, 0)),
                  pl.BlockSpec((None, d, tn), lambda l, j: (l, 0, j)),
                  pl.BlockSpec((None, 1, tn), lambda l, j: (l, 0, j))],
        out_specs=pl.BlockSpec((None, 8, tn), lambda l, j: (l, 0, j)),
        out_shape=jax.ShapeDtypeStruct((depth, 8, n), F32),
        compiler_params=_params(2),
        name="ada",
    )(cvec, w_ada, b_ada.reshape(depth, 1, n))


def _ffn_kernel(x_ref, mod_ref, g_ref, wg_ref, wu_ref, wd_ref, o_ref, *, j0):
    x = x_ref[...]
    h = _rms_mod(x, g_ref[...], mod_ref[j0:j0 + 1, :], mod_ref[j0 + 1:j0 + 2, :]).astype(BF16)
    a = jnp.dot(h, wg_ref[...], preferred_element_type=F32)
    u = jnp.dot(h, wu_ref[...], preferred_element_type=F32)
    act = (a * _sigmoid(a) * u).astype(BF16)
    y = jnp.dot(act, wd_ref[...], preferred_element_type=F32)
    o_ref[...] = x + (0.5 * mod_ref[j0 + 2:j0 + 3, :]) * y


def _ffn(xs, mod, norm_g, wg, wu, wd, *, layer, which, j0):
    b, l, d = xs.shape
    dff = wg.shape[-1]
    x_spec = pl.BlockSpec((None, TM, d), lambda bi, i: (bi, i, 0))
    return pl.pallas_call(
        functools.partial(_ffn_kernel, j0=j0),
        grid=(b, l // TM),
        in_specs=[x_spec,
                  pl.BlockSpec((None, None, N_MOD, d), lambda bi, i: (bi, jnp.minimum(i, 1), 0, 0)),
                  _resident((None, None, 1, d), lambda bi, i: (layer, 2 * which, 0, 0)),
                  _resident((None, None, d, dff), lambda bi, i: (layer, which, 0, 0)),
                  _resident((None, None, d, dff), lambda bi, i: (layer, which, 0, 0)),
                  _resident((None, None, dff, d), lambda bi, i: (layer, which, 0, 0))],
        out_specs=x_spec,
        out_shape=jax.ShapeDtypeStruct(xs.shape, F32),
        input_output_aliases={0: 0},
        compiler_params=_params(2),
        name="ffn",
    )(xs, mod, norm_g, wg, wu, wd)


def _norm_rope_t(tt, gain, rope):
    ms = jnp.sum(tt * tt, axis=0, keepdims=True) * (1.0 / HEAD_DIM)
    y = tt * lax.rsqrt(ms + EPS) * gain
    q = HEAD_DIM // 4
    cr, sr, cc, sc = (rope[j * q:(j + 1) * q, :] for j in range(4))
    y1, y2, y3, y4 = (y[j * q:(j + 1) * q, :] for j in range(4))
    return jnp.concatenate([y1 * cr - y2 * sr, y2 * cr + y1 * sr,
                            y3 * cc - y4 * sc, y4 * cc + y3 * sc], axis=0)


def _inproj_kernel(x_ref, mod_ref, g_ref, w_ref, qkg_ref, rope_ref,
                   qta_ref, qtc_ref, ka_ref, kc_ref, vta_ref, vtc_ref, b_ref, cu_ref):
    x = x_ref[...]
    h = _rms_mod(x, g_ref[...], mod_ref[3:4, :], mod_ref[4:5, :]).astype(BF16)
    z = jnp.dot(h, w_ref[...], preferred_element_type=F32)
    rope = rope_ref[...]

    def nr(lo, j):
        tt = z[:, lo:lo + LANES].T
        gain = qkg_ref[j * HEAD_DIM:(j + 1) * HEAD_DIM, :]
        return jnp.concatenate([_norm_rope_t(tt[0:HEAD_DIM, :], gain, rope),
                                _norm_rope_t(tt[HEAD_DIM:, :], gain, rope)], axis=0)

    ka_ref[...] = nr(0, 1).T.astype(BF16)
    kc_ref[...] = nr(256, 3).T.astype(BF16)
    ones = jnp.ones((VT_ROWS - HEAD_DIM, x.shape[0]), BF16)
    for vt_ref, lo in ((vta_ref, 128), (vtc_ref, 384)):
        vt = z[:, lo:lo + KV_COLS].T.astype(BF16)
        for g in range(KV_HEADS):
            vt_ref[g * VT_ROWS:g * VT_ROWS + HEAD_DIM, :] = vt[g * HEAD_DIM:(g + 1) * HEAD_DIM, :]
            vt_ref[g * VT_ROWS + HEAD_DIM:(g + 1) * VT_ROWS, :] = ones
    q0 = 4 * KV_COLS
    for c in range(Q_COLS // LANES):
        qta_ref[c * LANES:(c + 1) * LANES, :] = nr(q0 + c * LANES, 0).astype(BF16)
        qtc_ref[c * LANES:(c + 1) * LANES, :] = nr(q0 + Q_COLS + c * LANES, 2).astype(BF16)
    c0 = q0 + 2 * Q_COLS
    cw = b_ref.shape[-1]
    b_ref[...] = z[:, c0:c0 + cw]
    cu_ref[...] = z[:, c0 + cw:c0 + 2 * cw] * z[:, c0 + 2 * cw:c0 + 3 * cw]


def _inproj(xs, mod, norm_g, w_in, qkg, rope, *, layer):
    b, l, d = xs.shape
    ncol = w_in.shape[-1]
    cw = (ncol - 4 * KV_COLS - 2 * Q_COLS) // 3
    tok = lambda w: pl.BlockSpec((None, TM, w), lambda bi, i: (bi, i, 0))
    tr = lambda w: pl.BlockSpec((None, w, TM), lambda bi, i: (bi, 0, i))
    return pl.pallas_call(
        _inproj_kernel,
        grid=(b, l // TM),
        in_specs=[tok(d),
                  pl.BlockSpec((None, None, N_MOD, d), lambda bi, i: (bi, jnp.minimum(i, 1), 0, 0)),
                  _resident((None, None, 1, d), lambda bi, i: (layer, 1, 0, 0)),
                  _resident((None, d, ncol), lambda bi, i: (layer, 0, 0)),
                  _resident((None, 4 * HEAD_DIM, TM), lambda bi, i: (layer, 0, 0)),
                  pl.BlockSpec((HEAD_DIM, TM), lambda bi, i: (0, i))],
        out_specs=[tr(Q_COLS), tr(Q_COLS), tok(KV_COLS), tok(KV_COLS), tr(KV_HEADS * VT_ROWS),
                   tr(KV_HEADS * VT_ROWS), tok(cw), tok(cw)],
        out_shape=[jax.ShapeDtypeStruct((b, Q_COLS, l), BF16),
                   jax.ShapeDtypeStruct((b, Q_COLS, l), BF16),
                   jax.ShapeDtypeStruct((b, l, KV_COLS), BF16),
                   jax.ShapeDtypeStruct((b, l, KV_COLS), BF16),
                   jax.ShapeDtypeStruct((b, KV_HEADS * VT_ROWS, l), BF16),
                   jax.ShapeDtypeStruct((b, KV_HEADS * VT_ROWS, l), BF16),
                   jax.ShapeDtypeStruct((b, l, cw), F32),
                   jax.ShapeDtypeStruct((b, l, cw), F32)],
        compiler_params=_params(2),
        name="inproj",
    )(xs, mod, norm_g, w_in, qkg, rope)


def _pad_q(q, g):
    z = jnp.zeros_like(q)
    return jnp.concatenate([q, z] if g == 0 else [z, q], axis=0)


def _attend(qt_ref, k_ref, vt_ref, acc_ref, m_ref, al_ref, smax_ref, s_bufs, p_bufs, *, n_keys, tk):
    n_kt = n_keys // tk
    qrows = lambda h: slice(h * HEAD_DIM, (h + 1) * HEAD_DIM)
    arows = lambda h: slice(h * VT_ROWS, (h + 1) * VT_ROWS)
    qs = [_pad_q(qt_ref[qrows(h), :], h // GROUP) for h in range(Q_HEADS)]
    m_ref[...] = jnp.full(m_ref.shape, NEG_INF, F32)
    acc_ref[...] = jnp.zeros(acc_ref.shape, F32)

    ch = min(tk, CHUNK)
    n_ch = tk // ch

    def key_start(t, c):
        return t * tk + c * ch if isinstance(t, int) else pl.multiple_of(t * tk + c * ch, ch)

    def step(sc=None, sm=None, va=None):
        if sm is not None:
            m = m_ref[sm:sm + 1, :]
            m_new = jnp.maximum(m, smax_ref[sm:sm + 1, :])
            m_ref[sm:sm + 1, :] = m_new
            al_ref[sm:sm + 1, :] = jnp.exp2(m - m_new)
        smax = pv = None
        for c in range(n_ch):
            rows = slice(c * ch, (c + 1) * ch)
            if sc is not None:
                t, h = sc
                s = jnp.dot(k_ref[pl.ds(key_start(t, c), ch), :], qs[h],
                            preferred_element_type=F32)
                s_bufs[h % 2][rows, :] = s
                cmax = jnp.max(s, axis=0, keepdims=True)
                smax = cmax if smax is None else jnp.maximum(smax, cmax)
            if sm is not None:
                p_bufs[sm % 2][rows, :] = jnp.exp2(s_bufs[sm % 2][rows, :] - m_new).astype(BF16)
            if va is not None:
                t, h = va
                g = h // GROUP
                vt = vt_ref[g * VT_ROWS:(g + 1) * VT_ROWS, pl.ds(key_start(t, c), ch)]
                d = jnp.dot(vt, p_bufs[h % 2][rows, :], preferred_element_type=F32)
                pv = d if pv is None else pv + d
        if sc is not None:
            smax_ref[sc[1]:sc[1] + 1, :] = smax
        if va is not None:
            h = va[1]
            acc_ref[arows(h), :] = al_ref[h:h + 1, :] * acc_ref[arows(h), :] + pv

    def tile_steps(t, last):
        for h in range(Q_HEADS):
            if h + 2 < Q_HEADS:
                sc = (t, h + 2)
            else:
                sc = None if last else (t + 1, h + 2 - Q_HEADS)
            if h + 1 < Q_HEADS:
                sm = h + 1
            else:
                sm = None if last else 0
            step(sc, sm, (t, h))

    step(sc=(0, 0))
    step(sc=(0, 1), sm=0)
    if n_kt > 1:
        def body(t, carry):
            tile_steps(t, False)
            return carry
        lax.fori_loop(0, n_kt - 1, body, 0)
    tile_steps(n_kt - 1, True)


def _attend_bounded(qt_ref, k_ref, vt_ref, acc_ref, p_bufs, *, n_keys, tk):
    n_kt = n_keys // tk
    qrows = lambda h: slice(h * HEAD_DIM, (h + 1) * HEAD_DIM)
    arows = lambda h: slice(h * VT_ROWS, (h + 1) * VT_ROWS)
    qs = [_pad_q(qt_ref[qrows(h), :], h // GROUP) for h in range(Q_HEADS)]
    acc_ref[...] = jnp.zeros(acc_ref.shape, F32)
    ch = min(tk, CHUNK)
    n_ch = tk // ch

    def key_start(t, c):
        return t * tk + c * ch if isinstance(t, int) else pl.multiple_of(t * tk + c * ch, ch)

    def step(sc=None, va=None):
        pv = None
        for c in range(n_ch):
            rows = slice(c * ch, (c + 1) * ch)
            if sc is not None:
                t, h = sc
                s = jnp.dot(k_ref[pl.ds(key_start(t, c), ch), :], qs[h],
                            preferred_element_type=F32)
                p_bufs[h % 2][rows, :] = jnp.exp2(s).astype(BF16)
            if va is not None:
                t, h = va
                g = h // GROUP
                vt = vt_ref[g * VT_ROWS:(g + 1) * VT_ROWS, pl.ds(key_start(t, c), ch)]
                d = jnp.dot(vt, p_bufs[h % 2][rows, :], preferred_element_type=F32)
                pv = d if pv is None else pv + d
        if va is not None:
            h = va[1]
            acc_ref[arows(h), :] = acc_ref[arows(h), :] + pv

    def tile_steps(t, last):
        for h in range(Q_HEADS):
            if h + 1 < Q_HEADS:
                sc = (t, h + 1)
            else:
                sc = None if last else (t + 1, 0)
            step(sc, (t, h))

    step(sc=(0, 0))
    if n_kt > 1:
        def body(t, carry):
            tile_steps(t, False)
            return carry
        lax.fori_loop(0, n_kt - 1, body, 0)
    tile_steps(n_kt - 1, True)


def _global_attn_kernel(qt_ref, k_ref, vt_ref, o_ref, acc_ref, *scratch, tk, bounded):
    i = pl.program_id(1)
    if bounded:
        attend = functools.partial(_attend_bounded, qt_ref, k_ref, vt_ref, acc_ref, scratch)
    else:
        m_ref, al_ref, smax_ref, s0_ref, s1_ref, p0_ref, p1_ref = scratch
        attend = functools.partial(_attend, qt_ref, k_ref, vt_ref, acc_ref, m_ref, al_ref,
                                   smax_ref, (s0_ref, s1_ref), (p0_ref, p1_ref))

    @pl.when(i == 0)
    def _():
        attend(n_keys=TM, tk=TM)

    @pl.when(i > 0)
    def _():
        attend(n_keys=k_ref.shape[0], tk=tk)

    outs = []
    for h in range(Q_HEADS):
        a = acc_ref[h * VT_ROWS:(h + 1) * VT_ROWS, :]
        outs.append(a[0:HEAD_DIM, :] / a[HEAD_DIM:HEAD_DIM + 1, :])
    o_ref[...] = jnp.concatenate(outs, axis=0).T


def _key_tile(l):
    return next(t for t in (1280, 1024, 768, 512, 256) if l % t == 0)


def _global_attn(qt, k, vt, score_bound):
    return lax.cond(score_bound <= SCORE_BOUND,
                    functools.partial(_global_attn_call, bounded=True),
                    functools.partial(_global_attn_call, bounded=False), qt, k, vt)


def _global_attn_call(qt, k, vt, *, bounded):
    b, _, l = qt.shape
    tk = _key_tile(l)
    stat = pltpu.VMEM((Q_HEADS, TM), F32)
    p_buf = pltpu.VMEM((tk, TM), BF16)
    s_buf = pltpu.VMEM((tk, TM), F32)
    scratch = [p_buf, p_buf] if bounded else [stat, stat, stat, s_buf, s_buf, p_buf, p_buf]
    return pl.pallas_call(
        functools.partial(_global_attn_kernel, tk=tk, bounded=bounded),
        grid=(b, l // TM),
        in_specs=[pl.BlockSpec((None, Q_COLS, TM), lambda bi, i: (bi, 0, i)),
                  pl.BlockSpec((None, l, KV_COLS), lambda bi, i: (bi, 0, 0)),
                  pl.BlockSpec((None, KV_HEADS * VT_ROWS, l), lambda bi, i: (bi, 0, 0))],
        out_specs=pl.BlockSpec((None, TM, Q_COLS), lambda bi, i: (bi, i, 0)),
        out_shape=jax.ShapeDtypeStruct((b, l, Q_COLS), F32),
        scratch_shapes=[pltpu.VMEM((Q_HEADS * VT_ROWS, TM), F32)] + scratch,
        compiler_params=_params(2),
        name="global_attn_bounded" if bounded else "global_attn",
    )(qt, k, vt)


def _window_geometry(i, tq, l_all):
    span = tq + 2 * WINDOW
    start = pl.multiple_of(jnp.clip(i * tq - WINDOW, 0, l_all - span), LANES)
    kpos = start - TM + lax.broadcasted_iota(jnp.int32, (span, 1), 0)
    qbase = jnp.where(i >= 1, i * tq - TM, -(1 << 20))
    qpos = qbase + lax.broadcasted_iota(jnp.int32, (1, tq), 1)
    ok = jnp.logical_and(kpos >= 0, jnp.abs(qpos - kpos) <= WINDOW)
    return start, jnp.where(ok, 0.0, NEG_INF).astype(F32)


def _window_attn_kernel(qt_ref, k_ref, vt_ref, sink_ref, o_ref, acc_ref):
    tq = qt_ref.shape[-1]
    span = tq + 2 * WINDOW
    start, bias = _window_geometry(pl.program_id(1), tq, k_ref.shape[0])
    k_ctx = k_ref[0:TM, :]
    k_win = k_ref[pl.ds(start, span), :]
    for h in range(Q_HEADS):
        g = h // GROUP
        q = _pad_q(qt_ref[h * HEAD_DIM:(h + 1) * HEAD_DIM, :], g)
        s1 = jnp.dot(k_ctx, q, preferred_element_type=F32)
        s2 = jnp.dot(k_win, q, preferred_element_type=F32) + bias
        sink = sink_ref[h:h + 1, :]
        m = jnp.maximum(jnp.maximum(jnp.max(s1, axis=0, keepdims=True),
                                    jnp.max(s2, axis=0, keepdims=True)), sink)
        p1 = jnp.exp2(s1 - m)
        p2 = jnp.exp2(s2 - m)
        l = (jnp.sum(p1, axis=0, keepdims=True) + jnp.sum(p2, axis=0, keepdims=True)
             + jnp.exp2(sink - m))
        rows = slice(g * VT_ROWS, g * VT_ROWS + HEAD_DIM)
        acc = (jnp.dot(vt_ref[rows, 0:TM], p1.astype(BF16), preferred_element_type=F32)
               + jnp.dot(vt_ref[rows, pl.ds(start, span)], p2.astype(BF16),
                         preferred_element_type=F32))
        acc_ref[h * HEAD_DIM:(h + 1) * HEAD_DIM, :] = acc / l
    o_ref[...] = acc_ref[...].T


def _window_attn_bounded_kernel(qt_ref, k_ref, vt_ref, sink_ref, o_ref, p0_ref, p1_ref):
    tq = qt_ref.shape[-1]
    start, bias = _window_geometry(pl.program_id(1), tq, k_ref.shape[0])
    p_bufs = (p0_ref, p1_ref)
    n_win = bias.shape[0] // CHUNK
    chunks = [(0, None)] + [(pl.multiple_of(start + c * CHUNK, LANES),
                             bias[c * CHUNK:(c + 1) * CHUNK, :]) for c in range(n_win)]
    qs = [_pad_q(qt_ref[h * HEAD_DIM:(h + 1) * HEAD_DIM, :], h // GROUP) for h in range(Q_HEADS)]
    accs = [None] * Q_HEADS

    def step(sc=None, va=None):
        pv = None
        for ci, (k0, mask) in enumerate(chunks):
            rows = slice(ci * CHUNK, (ci + 1) * CHUNK)
            if sc is not None:
                s = jnp.dot(k_ref[pl.ds(k0, CHUNK), :], qs[sc], preferred_element_type=F32)
                if mask is not None:
                    s = s + mask
                p_bufs[sc % 2][rows, :] = jnp.exp2(s).astype(BF16)
            if va is not None:
                g = va // GROUP
                vt = vt_ref[g * VT_ROWS:(g + 1) * VT_ROWS, pl.ds(k0, CHUNK)]
                d = jnp.dot(vt, p_bufs[va % 2][rows, :], preferred_element_type=F32)
                pv = d if pv is None else pv + d
        if va is not None:
            accs[va] = pv

    step(sc=0)
    for h in range(Q_HEADS):
        step(sc=h + 1 if h + 1 < Q_HEADS else None, va=h)
    outs = []
    for h in range(Q_HEADS):
        l = accs[h][HEAD_DIM:HEAD_DIM + 1, :] + jnp.exp2(sink_ref[h:h + 1, :])
        outs.append(accs[h][0:HEAD_DIM, :] / l)
    o_ref[...] = jnp.concatenate(outs, axis=0).T


def _window_attn(qt, k, vt, sink, score_bound):
    sink2 = sink.astype(F32) * LOG2E
    ok = jnp.logical_and(score_bound <= SCORE_BOUND, jnp.max(jnp.abs(sink2)) <= SCORE_BOUND)
    sink_b = jnp.broadcast_to(sink2[:, None], (Q_HEADS, TM))
    return lax.cond(ok, functools.partial(_window_attn_call, bounded=True),
                    functools.partial(_window_attn_call, bounded=False), qt, k, vt, sink_b)


def _window_attn_call(qt, k, vt, sink_b, *, bounded):
    b, _, l = qt.shape
    p_buf = pltpu.VMEM((2 * TM + 2 * WINDOW, TM), BF16)
    return pl.pallas_call(
        _window_attn_bounded_kernel if bounded else _window_attn_kernel,
        grid=(b, l // TM),
        in_specs=[pl.BlockSpec((None, Q_COLS, TM), lambda bi, i: (bi, 0, i)),
                  pl.BlockSpec((None, l, KV_COLS), lambda bi, i: (bi, 0, 0)),
                  pl.BlockSpec((None, KV_HEADS * VT_ROWS, l), lambda bi, i: (bi, 0, 0)),
                  pl.BlockSpec((Q_HEADS, TM), lambda bi, i: (0, 0))],
        out_specs=pl.BlockSpec((None, TM, Q_COLS), lambda bi, i: (bi, i, 0)),
        out_shape=jax.ShapeDtypeStruct((b, l, Q_COLS), F32),
        scratch_shapes=[p_buf, p_buf] if bounded else [pltpu.VMEM((Q_COLS, TM), F32)],
        compiler_params=_params(2),
        name="window_attn_bounded" if bounded else "window_attn",
    )(qt, k, vt, sink_b)


def _merge_kernel(x_ref, mod_ref, g_ref, oa_ref, oc_ref, b_ref, cu_ref, cup_ref, cun_ref, cw_ref,
                  wgate_ref, bgate_ref, wpa_ref, wpb_ref, wpc_ref, wo_ref, o_ref):
    i = pl.program_id(1)
    n_tiles = pl.num_programs(1)
    x = x_ref[...]
    tm, d = x.shape
    h = _rms_mod(x, g_ref[...], mod_ref[3:4, :], mod_ref[4:5, :]).astype(BF16)
    gates = _sigmoid(jnp.dot(h, wgate_ref[...], preferred_element_type=F32) + bgate_ref[...])

    cu = cu_ref[...]
    has_prev = (i >= 2).astype(F32)
    has_next = jnp.logical_and(i >= 1, i < n_tiles - 1).astype(F32)
    prev_row = cup_ref[7:8, :] * has_prev
    next_row = cun_ref[0:1, :] * has_next
    row = lax.broadcasted_iota(jnp.int32, (tm, 1), 0)
    cu_dn = jnp.where(row == 0, prev_row, pltpu.roll(cu, 1, 0))
    cu_up = jnp.where(row == tm - 1, next_row, pltpu.roll(cu, tm - 1, 0))
    y = cw_ref[0:1, :] * cu_dn + cw_ref[1:2, :] * cu + cw_ref[2:3, :] * cu_up
    o_b = (b_ref[...] * y).astype(BF16)

    pa = jnp.dot(oa_ref[...].astype(BF16), wpa_ref[...], preferred_element_type=F32)
    pb = jnp.dot(o_b, wpb_ref[...], preferred_element_type=F32)
    pc = jnp.dot(oc_ref[...].astype(BF16), wpc_ref[...], preferred_element_type=F32)
    mix = gates[:, 0:d] * pa + gates[:, d:2 * d] * pb + gates[:, 2 * d:3 * d] * pc
    out = jnp.dot(mix.astype(BF16), wo_ref[...], preferred_element_type=F32)
    o_ref[...] = x + mod_ref[5:6, :] * out


def _merge(xs, mod, norm_g, o_a, o_c, bb, cu, conv_w, w_gate, b_gate, w_pa, w_pb, w_pc, w_o,
           *, layer):
    b, l, d = xs.shape
    cw = cu.shape[-1]
    sub = 8
    tok = lambda w: pl.BlockSpec((None, TM, w), lambda bi, i: (bi, i, 0))
    lay = lambda *s: _resident((None,) + s, lambda bi, i: (layer,) + (0,) * len(s))
    return pl.pallas_call(
        _merge_kernel,
        grid=(b, l // TM),
        in_specs=[tok(d),
                  pl.BlockSpec((None, None, N_MOD, d), lambda bi, i: (bi, jnp.minimum(i, 1), 0, 0)),
                  _resident((None, None, 1, d), lambda bi, i: (layer, 1, 0, 0)),
                  tok(Q_COLS), tok(Q_COLS), tok(cw), tok(cw),
                  pl.BlockSpec((None, sub, cw),
                               lambda bi, i: (bi, jnp.maximum(i * (TM // sub) - 1, 0), 0)),
                  pl.BlockSpec((None, sub, cw),
                               lambda bi, i: (bi, jnp.minimum((i + 1) * (TM // sub), l // sub - 1), 0)),
                  lay(3, cw), lay(d, 3 * d), lay(1, 3 * d), lay(Q_COLS, d), lay(cw, d),
                  lay(Q_COLS, d), lay(d, d)],
        out_specs=tok(d),
        out_shape=jax.ShapeDtypeStruct(xs.shape, F32),
        input_output_aliases={0: 0},
        compiler_params=_params(2),
        name="merge",
    )(xs, mod, norm_g, o_a, o_c, bb, cu, cu, cu, conv_w, w_gate, b_gate, w_pa, w_pb, w_pc, w_o)


def _rope_tables(s, n_ctx):
    rows = s // GRID_W
    row = jnp.repeat(jnp.arange(rows), GRID_W).astype(F32)
    col = jnp.tile(jnp.arange(GRID_W), rows).astype(F32)
    half = HEAD_DIM // 2
    inv = ROPE_THETA ** (-jnp.arange(0, half, 2, dtype=F32) / half)
    ang_r = row[:, None] * inv
    ang_c = col[:, None] * inv
    tabs = jnp.concatenate([jnp.cos(ang_r), jnp.sin(ang_r), jnp.cos(ang_c), jnp.sin(ang_c)],
                           axis=-1)
    one, zero = jnp.ones((n_ctx, HEAD_DIM // 4), F32), jnp.zeros((n_ctx, HEAD_DIM // 4), F32)
    ident = jnp.concatenate([one, zero, one, zero], axis=-1)
    return jnp.concatenate([ident, tabs], axis=0).T


def kernel(x, c, ctx, c_ctx, w_ada, b_ada, norm_g, ffn_w_gate, ffn_w_up, ffn_w_down, w_in, qk_g,
           sink_a, conv_w, w_pa, w_pb, w_pc, w_gate, b_gate, w_o):
    bsz, s, d = x.shape
    n_ctx = ctx.shape[1]
    depth = w_ada.shape[0]
    assert n_ctx == TM and s % TM == 0 and s % GRID_W == 0 and bsz < 8

    cvec = jnp.zeros((8, d), F32).at[:bsz].set(c).at[bsz].set(c_ctx)
    mods = _ada(cvec, w_ada, b_ada).reshape(depth, 8, N_MOD, d)
    lat = mods[:, :bsz]
    con = jnp.broadcast_to(mods[:, bsz:bsz + 1], lat.shape)
    mod_all = jnp.stack([con, lat], axis=2)

    rope = _rope_tables(s, n_ctx)
    fold = jnp.array([Q_SCALE, 1.0, Q_SCALE, 1.0], F32)[None, :, None]
    qkg = jnp.broadcast_to((qk_g.astype(F32) * fold).reshape(depth, 4 * HEAD_DIM, 1),
                           (depth, 4 * HEAD_DIM, TM))
    ng = norm_g.astype(F32)[:, :, None, :]
    wg, wu, wd = (w.astype(BF16) for w in (ffn_w_gate, ffn_w_up, ffn_w_down))
    w_in_b, w_gate_b, w_pa_b, w_pb_b, w_pc_b, w_o_b = (
        w.astype(BF16) for w in (w_in, w_gate, w_pa, w_pb, w_pc, w_o))
    b_gate3 = b_gate.astype(F32)[:, None, :]

    xs = jnp.concatenate([ctx, x], axis=1)
    for i in range(depth):
        mod = mod_all[i]
        xs = _ffn(xs, mod, ng, wg, wu, wd, layer=i, which=0, j0=0)
        qta, qtc, ka, kc, vta, vtc, bb, cu = _inproj(xs, mod, ng, w_in_b, qkg, rope, layer=i)
        gmax = jnp.max(jnp.abs(qk_g[i].astype(F32)), axis=-1)
        bound = 1.02 * HEAD_DIM * Q_SCALE
        o_a = _window_attn(qta, ka, vta, sink_a[i], bound * gmax[0] * gmax[1])
        o_c = _global_attn(qtc, kc, vtc, bound * gmax[2] * gmax[3])
        xs = _merge(xs, mod, ng, o_a, o_c, bb, cu, conv_w.astype(F32), w_gate_b, b_gate3,
                    w_pa_b, w_pb_b, w_pc_b, w_o_b, layer=i)
        xs = _ffn(xs, mod, ng, wg, wu, wd, layer=i, which=1, j0=6)
    return xs[:, n_ctx:, :]
```

```python
import functools

import jax
import jax.numpy as jnp
from jax import lax
from jax.experimental import pallas as pl
from jax.experimental.pallas import tpu as pltpu

GRID_W = 64
HEAD_DIM = 64
Q_HEADS = 8
KV_HEADS = 2
GROUP = Q_HEADS // KV_HEADS
WINDOW = 128
N_MOD = 9
ROPE_THETA = 10000.0
EPS = 1e-6
NEG_INF = -1e30
SCALE = HEAD_DIM ** -0.5
LOG2E = 1.4426950408889634
Q_SCALE = SCALE * LOG2E
SCORE_BOUND = 48.0
Q_COLS = Q_HEADS * HEAD_DIM
KV_COLS = KV_HEADS * HEAD_DIM
VT_ROWS = HEAD_DIM + 16

LANES = 128
TM = 256
CHUNK = 256
VMEM_LIMIT = 56 * 1024 * 1024

F32 = jnp.float32
BF16 = jnp.bfloat16
F8 = jnp.float8_e4m3fn
F8_MAX = 448.0
F8_ROWS = 4 * HEAD_DIM


def _params(n_axes, flags=None):
    return pltpu.CompilerParams(dimension_semantics=("arbitrary",) * n_axes,
                                vmem_limit_bytes=VMEM_LIMIT, flags=flags)


def _resident(shape, index_map):
    return pl.BlockSpec(shape, index_map, pipeline_mode=pl.Buffered(1))


def _sigmoid(v):
    return 1.0 / (1.0 + jnp.exp(-v))


def _rms_mod(x, g, shift, scale):
    y = x * lax.rsqrt(jnp.mean(x * x, axis=-1, keepdims=True) + EPS)
    return (y * g) * (1.0 + scale) + shift


def _ada_kernel(c_ref, w_ref, b_ref, o_ref):
    c = c_ref[...]
    s = (c * _sigmoid(c)).astype(BF16)
    o_ref[...] = jnp.dot(s, w_ref[...].astype(BF16), preferred_element_type=F32) + b_ref[...]


def _ada(cvec, w_ada, b_ada):
    depth, d, n = w_ada.shape
    tn = 1024
    return pl.pallas_call(
        _ada_kernel,
        grid=(depth, n // tn),
        in_specs=[pl.BlockSpec((8, d), lambda l, j: (0, 0)),
                  pl.BlockSpec((None, d, tn), lambda l, j: (l, 0, j)),
                  pl.BlockSpec((None, 1, tn), lambda l, j: (l, 0, j))],
        out_specs=pl.BlockSpec((None, 8, tn), lambda l, j: (l, 0, j)),
        out_shape=jax.ShapeDtypeStruct((depth, 8, n), F32),
        compiler_params=_params(2),
        name="ada",
    )(cvec, w_ada, b_ada.reshape(depth, 1, n))


def _ffn_kernel(x_ref, mod_ref, g_ref, wg_ref, wu_ref, wd_ref, o_ref, *, j0):
    x = x_ref[...]
    h = _rms_mod(x, g_ref[...], mod_ref[j0:j0 + 1, :], mod_ref[j0 + 1:j0 + 2, :]).astype(BF16)
    a = jnp.dot(h, wg_ref[...], preferred_element_type=F32)
    u = jnp.dot(h, wu_ref[...], preferred_element_type=F32)
    act = (a * _sigmoid(a) * u).astype(BF16)
    y = jnp.dot(act, wd_ref[...], preferred_element_type=F32)
    o_ref[...] = x + (0.5 * mod_ref[j0 + 2:j0 + 3, :]) * y


def _ffn(xs, mod, norm_g, wg, wu, wd, *, layer, which, j0):
    b, l, d = xs.shape
    dff = wg.shape[-1]
    x_spec = pl.BlockSpec((None, TM, d), lambda bi, i: (bi, i, 0))
    return pl.pallas_call(
        functools.partial(_ffn_kernel, j0=j0),
        grid=(b, l // TM),
        in_specs=[x_spec,
                  pl.BlockSpec((None, None, N_MOD, d), lambda bi, i: (bi, jnp.minimum(i, 1), 0, 0)),
                  _resident((None, None, 1, d), lambda bi, i: (layer, 2 * which, 0, 0)),
                  _resident((None, None, d, dff), lambda bi, i: (layer, which, 0, 0)),
                  _resident((None, None, d, dff), lambda bi, i: (layer, which, 0, 0)),
                  _resident((None, None, dff, d), lambda bi, i: (layer, which, 0, 0))],
        out_specs=x_spec,
        out_shape=jax.ShapeDtypeStruct(xs.shape, F32),
        input_output_aliases={0: 0},
        compiler_params=_params(2),
        name="ffn",
    )(xs, mod, norm_g, wg, wu, wd)


def _norm_rope_t(tt, gain, rope):
    ms = jnp.sum(tt * tt, axis=0, keepdims=True) * (1.0 / HEAD_DIM)
    y = tt * lax.rsqrt(ms + EPS) * gain
    q = HEAD_DIM // 4
    cr, sr, cc, sc = (rope[j * q:(j + 1) * q, :] for j in range(4))
    y1, y2, y3, y4 = (y[j * q:(j + 1) * q, :] for j in range(4))
    return jnp.concatenate([y1 * cr - y2 * sr, y2 * cr + y1 * sr,
                            y3 * cc - y4 * sc, y4 * cc + y3 * sc], axis=0)


def _split8(y):
    hi = y.astype(F8).astype(F32)
    return hi, y - hi


def _inproj_kernel(x_ref, mod_ref, g_ref, w_ref, qkg_ref, rope_ref,
                   qta_ref, qc8_ref, ka_ref, kc8_ref, vta_ref, vtc_ref, b_ref, cu_ref):
    x = x_ref[...]
    h = _rms_mod(x, g_ref[...], mod_ref[3:4, :], mod_ref[4:5, :]).astype(BF16)
    z = jnp.dot(h, w_ref[...], preferred_element_type=F32)
    rope = rope_ref[...]

    def heads(lo, j):
        tt = z[:, lo:lo + LANES].T
        gain = qkg_ref[j * HEAD_DIM:(j + 1) * HEAD_DIM, :]
        return [_norm_rope_t(tt[0:HEAD_DIM, :], gain, rope),
                _norm_rope_t(tt[HEAD_DIM:, :], gain, rope)]

    nr = lambda lo, j: jnp.concatenate(heads(lo, j), axis=0)
    zeros = jnp.zeros((HEAD_DIM, x.shape[0]), F32)

    ka_ref[...] = nr(0, 1).T.astype(BF16)
    for g, y in enumerate(heads(256, 3)):
        hi, lo = _split8(y)
        kc8_ref[g] = jnp.concatenate([hi, hi, lo, zeros], axis=0).T.astype(F8)
    ones = jnp.ones((VT_ROWS - HEAD_DIM, x.shape[0]), BF16)
    for vt_ref, lo in ((vta_ref, 128), (vtc_ref, 384)):
        vt = z[:, lo:lo + KV_COLS].T.astype(BF16)
        for g in range(KV_HEADS):
            vt_ref[g * VT_ROWS:g * VT_ROWS + HEAD_DIM, :] = vt[g * HEAD_DIM:(g + 1) * HEAD_DIM, :]
            vt_ref[g * VT_ROWS + HEAD_DIM:(g + 1) * VT_ROWS, :] = ones
    q0 = 4 * KV_COLS
    for c in range(Q_COLS // LANES):
        qta_ref[c * LANES:(c + 1) * LANES, :] = nr(q0 + c * LANES, 0).astype(BF16)
        for j, y in enumerate(heads(q0 + Q_COLS + c * LANES, 2)):
            hi, lo = _split8(y)
            r0 = (2 * c + j) * F8_ROWS
            qc8_ref[r0:r0 + F8_ROWS, :] = jnp.concatenate([hi, lo, hi, zeros], axis=0).astype(F8)
    c0 = q0 + 2 * Q_COLS
    cw = b_ref.shape[-1]
    b_ref[...] = z[:, c0:c0 + cw]
    cu_ref[...] = z[:, c0 + cw:c0 + 2 * cw] * z[:, c0 + 2 * cw:c0 + 3 * cw]


def _inproj(xs, mod, norm_g, w_in, qkg, rope, *, layer):
    b, l, d = xs.shape
    ncol = w_in.shape[-1]
    cw = (ncol - 4 * KV_COLS - 2 * Q_COLS) // 3
    tok = lambda w: pl.BlockSpec((None, TM, w), lambda bi, i: (bi, i, 0))
    tr = lambda w: pl.BlockSpec((None, w, TM), lambda bi, i: (bi, 0, i))
    return pl.pallas_call(
        _inproj_kernel,
        grid=(b, l // TM),
        in_specs=[tok(d),
                  pl.BlockSpec((None, None, N_MOD, d), lambda bi, i: (bi, jnp.minimum(i, 1), 0, 0)),
                  _resident((None, None, 1, d), lambda bi, i: (layer, 1, 0, 0)),
                  _resident((None, d, ncol), lambda bi, i: (layer, 0, 0)),
                  _resident((None, 4 * HEAD_DIM, TM), lambda bi, i: (layer, 0, 0)),
                  pl.BlockSpec((HEAD_DIM, TM), lambda bi, i: (0, i))],
        out_specs=[tr(Q_COLS), tr(Q_HEADS * F8_ROWS), tok(KV_COLS),
                   pl.BlockSpec((None, KV_HEADS, TM, F8_ROWS), lambda bi, i: (bi, 0, i, 0)),
                   tr(KV_HEADS * VT_ROWS), tr(KV_HEADS * VT_ROWS), tok(cw), tok(cw)],
        out_shape=[jax.ShapeDtypeStruct((b, Q_COLS, l), BF16),
                   jax.ShapeDtypeStruct((b, Q_HEADS * F8_ROWS, l), F8),
                   jax.ShapeDtypeStruct((b, l, KV_COLS), BF16),
                   jax.ShapeDtypeStruct((b, KV_HEADS, l, F8_ROWS), F8),
                   jax.ShapeDtypeStruct((b, KV_HEADS * VT_ROWS, l), BF16),
                   jax.ShapeDtypeStruct((b, KV_HEADS * VT_ROWS, l), BF16),
                   jax.ShapeDtypeStruct((b, l, cw), F32),
                   jax.ShapeDtypeStruct((b, l, cw), F32)],
        compiler_params=_params(2),
        name="inproj",
    )(xs, mod, norm_g, w_in, qkg, rope)


def _pad_q(q, g):
    z = jnp.zeros_like(q)
    return jnp.concatenate([q, z] if g == 0 else [z, q], axis=0)


def _score_chunk(q8_ref, k8_ref, un_ref, start, ch, h):
    s = jnp.dot(k8_ref[h // GROUP, pl.ds(start, ch), :], q8_ref[h * F8_ROWS:(h + 1) * F8_ROWS, :],
                preferred_element_type=F32)
    return s * un_ref[...]


def _attend(q8_ref, k8_ref, un_ref, vt_ref, acc_ref, m_ref, al_ref, smax_ref, s_bufs, p_bufs, *,
            n_keys, tk):
    n_kt = n_keys // tk
    arows = lambda h: slice(h * VT_ROWS, (h + 1) * VT_ROWS)
    m_ref[...] = jnp.full(m_ref.shape, NEG_INF, F32)
    acc_ref[...] = jnp.zeros(acc_ref.shape, F32)

    ch = min(tk, CHUNK)
    n_ch = tk // ch

    def key_start(t, c):
        return t * tk + c * ch if isinstance(t, int) else pl.multiple_of(t * tk + c * ch, ch)

    def step(sc=None, sm=None, va=None):
        if sm is not None:
            m = m_ref[sm:sm + 1, :]
            m_new = jnp.maximum(m, smax_ref[sm:sm + 1, :])
            m_ref[sm:sm + 1, :] = m_new
            al_ref[sm:sm + 1, :] = jnp.exp2(m - m_new)
        smax = pv = None
        for c in range(n_ch):
            rows = slice(c * ch, (c + 1) * ch)
            if sc is not None:
                t, h = sc
                s = _score_chunk(q8_ref, k8_ref, un_ref, key_start(t, c), ch, h)
                s_bufs[h % 2][rows, :] = s
                cmax = jnp.max(s, axis=0, keepdims=True)
                smax = cmax if smax is None else jnp.maximum(smax, cmax)
            if sm is not None:
                p_bufs[sm % 2][rows, :] = jnp.exp2(s_bufs[sm % 2][rows, :] - m_new).astype(BF16)
            if va is not None:
                t, h = va
                g = h // GROUP
                vt = vt_ref[g * VT_ROWS:(g + 1) * VT_ROWS, pl.ds(key_start(t, c), ch)]
                d = jnp.dot(vt, p_bufs[h % 2][rows, :], preferred_element_type=F32)
                pv = d if pv is None else pv + d
        if sc is not None:
            smax_ref[sc[1]:sc[1] + 1, :] = smax
        if va is not None:
            h = va[1]
            acc_ref[arows(h), :] = al_ref[h:h + 1, :] * acc_ref[arows(h), :] + pv

    def tile_steps(t, last):
        for h in range(Q_HEADS):
            if h + 2 < Q_HEADS:
                sc = (t, h + 2)
            else:
                sc = None if last else (t + 1, h + 2 - Q_HEADS)
            if h + 1 < Q_HEADS:
                sm = h + 1
            else:
                sm = None if last else 0
            step(sc, sm, (t, h))

    step(sc=(0, 0))
    step(sc=(0, 1), sm=0)
    if n_kt > 1:
        def body(t, carry):
            tile_steps(t, False)
            return carry
        lax.fori_loop(0, n_kt - 1, body, 0)
    tile_steps(n_kt - 1, True)


def _attend_bounded(q8_ref, k8_ref, un_ref, vt_ref, acc_ref, p_bufs, *, n_keys, tk):
    n_kt = n_keys // tk
    arows = lambda h: slice(h * VT_ROWS, (h + 1) * VT_ROWS)
    acc_ref[...] = jnp.zeros(acc_ref.shape, F32)
    ch = min(tk, CHUNK)
    n_ch = tk // ch

    def key_start(t, c):
        return t * tk + c * ch if isinstance(t, int) else pl.multiple_of(t * tk + c * ch, ch)

    def step(sc=None, va=None):
        if va is not None:
            t, h = va
            g = h // GROUP
            vt = vt_ref[g * VT_ROWS:(g + 1) * VT_ROWS, pl.ds(key_start(t, 0), tk)]
            acc_ref[arows(h), :] = acc_ref[arows(h), :] + jnp.dot(
                vt, p_bufs[h % 2][0:tk, :], preferred_element_type=F32)
        if sc is not None:
            t, h = sc
            for c in range(n_ch):
                s = _score_chunk(q8_ref, k8_ref, un_ref, key_start(t, c), ch, h)
                p_bufs[h % 2][c * ch:(c + 1) * ch, :] = jnp.exp2(s).astype(BF16)

    def tile_steps(t, last):
        for h in range(Q_HEADS):
            if h + 1 < Q_HEADS:
                sc = (t, h + 1)
            else:
                sc = None if last else (t + 1, 0)
            step(sc, (t, h))

    step(sc=(0, 0))
    if n_kt > 1:
        def body(t, carry):
            tile_steps(t, False)
            return carry
        lax.fori_loop(0, n_kt - 1, body, 0)
    tile_steps(n_kt - 1, True)


def _global_attn_kernel(q8_ref, k8_ref, un_ref, vt_ref, o_ref, acc_ref, *scratch, tk, bounded):
    i = pl.program_id(1)
    if bounded:
        attend = functools.partial(_attend_bounded, q8_ref, k8_ref, un_ref, vt_ref, acc_ref, scratch)
    else:
        m_ref, al_ref, smax_ref, s0_ref, s1_ref, p0_ref, p1_ref = scratch
        attend = functools.partial(_attend, q8_ref, k8_ref, un_ref, vt_ref, acc_ref, m_ref, al_ref,
                                   smax_ref, (s0_ref, s1_ref), (p0_ref, p1_ref))

    @pl.when(i == 0)
    def _():
        attend(n_keys=TM, tk=TM)

    @pl.when(i > 0)
    def _():
        attend(n_keys=k8_ref.shape[1], tk=tk)

    outs = []
    for h in range(Q_HEADS):
        a = acc_ref[h * VT_ROWS:(h + 1) * VT_ROWS, :]
        outs.append(a[0:HEAD_DIM, :] / a[HEAD_DIM:HEAD_DIM + 1, :])
    o_ref[...] = jnp.concatenate(outs, axis=0).T


def _key_tile(l):
    return next(t for t in (1280, 1024, 768, 512, 256) if l % t == 0)


def _global_attn(q8, k8, unscale, vt, score_bound):
    un = jnp.broadcast_to(unscale.astype(F32), (1, TM))
    return lax.cond(score_bound <= SCORE_BOUND,
                    functools.partial(_global_attn_call, bounded=True),
                    functools.partial(_global_attn_call, bounded=False), q8, k8, un, vt)


def _global_attn_call(q8, k8, un, vt, *, bounded):
    b, _, l = q8.shape
    tk = _key_tile(l)
    stat = pltpu.VMEM((Q_HEADS, TM), F32)
    p_buf = pltpu.VMEM((tk, TM), BF16)
    s_buf = pltpu.VMEM((tk, TM), F32)
    scratch = [p_buf, p_buf] if bounded else [stat, stat, stat, s_buf, s_buf, p_buf, p_buf]
    return pl.pallas_call(
        functools.partial(_global_attn_kernel, tk=tk, bounded=bounded),
        grid=(b, l // TM),
        in_specs=[pl.BlockSpec((None, Q_HEADS * F8_ROWS, TM), lambda bi, i: (bi, 0, i)),
                  pl.BlockSpec((None, KV_HEADS, l, F8_ROWS), lambda bi, i: (bi, 0, 0, 0)),
                  pl.BlockSpec((1, TM), lambda bi, i: (0, 0)),
                  pl.BlockSpec((None, KV_HEADS * VT_ROWS, l), lambda bi, i: (bi, 0, 0))],
        out_specs=pl.BlockSpec((None, TM, Q_COLS), lambda bi, i: (bi, i, 0)),
        out_shape=jax.ShapeDtypeStruct((b, l, Q_COLS), F32),
        scratch_shapes=[pltpu.VMEM((Q_HEADS * VT_ROWS, TM), F32)] + scratch,
        compiler_params=_params(2),
        name="global_attn_bounded" if bounded else "global_attn",
    )(q8, k8, un, vt)


def _window_geometry(i, tq, l_all):
    span = tq + 2 * WINDOW
    start = pl.multiple_of(jnp.clip(i * tq - WINDOW, 0, l_all - span), LANES)
    kpos = start - TM + lax.broadcasted_iota(jnp.int32, (span, 1), 0)
    qbase = jnp.where(i >= 1, i * tq - TM, -(1 << 20))
    qpos = qbase + lax.broadcasted_iota(jnp.int32, (1, tq), 1)
    ok = jnp.logical_and(kpos >= 0, jnp.abs(qpos - kpos) <= WINDOW)
    return start, jnp.where(ok, 0.0, NEG_INF).astype(F32)


def _window_attn_kernel(qt_ref, k_ref, vt_ref, sink_ref, o_ref, acc_ref):
    tq = qt_ref.shape[-1]
    span = tq + 2 * WINDOW
    start, bias = _window_geometry(pl.program_id(1), tq, k_ref.shape[0])
    k_ctx = k_ref[0:TM, :]
    k_win = k_ref[pl.ds(start, span), :]
    for h in range(Q_HEADS):
        g = h // GROUP
        q = _pad_q(qt_ref[h * HEAD_DIM:(h + 1) * HEAD_DIM, :], g)
        s1 = jnp.dot(k_ctx, q, preferred_element_type=F32)
        s2 = jnp.dot(k_win, q, preferred_element_type=F32) + bias
        sink = sink_ref[h:h + 1, :]
        m = jnp.maximum(jnp.maximum(jnp.max(s1, axis=0, keepdims=True),
                                    jnp.max(s2, axis=0, keepdims=True)), sink)
        p1 = jnp.exp2(s1 - m)
        p2 = jnp.exp2(s2 - m)
        l = (jnp.sum(p1, axis=0, keepdims=True) + jnp.sum(p2, axis=0, keepdims=True)
             + jnp.exp2(sink - m))
        rows = slice(g * VT_ROWS, g * VT_ROWS + HEAD_DIM)
        acc = (jnp.dot(vt_ref[rows, 0:TM], p1.astype(BF16), preferred_element_type=F32)
               + jnp.dot(vt_ref[rows, pl.ds(start, span)], p2.astype(BF16),
                         preferred_element_type=F32))
        acc_ref[h * HEAD_DIM:(h + 1) * HEAD_DIM, :] = acc / l
    o_ref[...] = acc_ref[...].T


def _window_attn_bounded_kernel(qt_ref, k_ref, vt_ref, sink_ref, o_ref, p0_ref, p1_ref):
    tq = qt_ref.shape[-1]
    start, bias = _window_geometry(pl.program_id(1), tq, k_ref.shape[0])
    p_bufs = (p0_ref, p1_ref)
    n_win = bias.shape[0] // CHUNK
    chunks = [(0, None)] + [(pl.multiple_of(start + c * CHUNK, LANES),
                             bias[c * CHUNK:(c + 1) * CHUNK, :]) for c in range(n_win)]
    qs = [_pad_q(qt_ref[h * HEAD_DIM:(h + 1) * HEAD_DIM, :], h // GROUP) for h in range(Q_HEADS)]
    accs = [None] * Q_HEADS

    def step(sc=None, va=None):
        pv = None
        for ci, (k0, mask) in enumerate(chunks):
            rows = slice(ci * CHUNK, (ci + 1) * CHUNK)
            if sc is not None:
                s = jnp.dot(k_ref[pl.ds(k0, CHUNK), :], qs[sc], preferred_element_type=F32)
                if mask is not None:
                    s = s + mask
                p_bufs[sc % 2][rows, :] = jnp.exp2(s).astype(BF16)
            if va is not None:
                g = va // GROUP
                vt = vt_ref[g * VT_ROWS:(g + 1) * VT_ROWS, pl.ds(k0, CHUNK)]
                d = jnp.dot(vt, p_bufs[va % 2][rows, :], preferred_element_type=F32)
                pv = d if pv is None else pv + d
        if va is not None:
            accs[va] = pv

    step(sc=0)
    for h in range(Q_HEADS):
        step(sc=h + 1 if h + 1 < Q_HEADS else None, va=h)
    outs = []
    for h in range(Q_HEADS):
        l = accs[h][HEAD_DIM:HEAD_DIM + 1, :] + jnp.exp2(sink_ref[h:h + 1, :])
        outs.append(accs[h][0:HEAD_DIM, :] / l)
    o_ref[...] = jnp.concatenate(outs, axis=0).T


def _window_attn(qt, k, vt, sink, score_bound):
    sink2 = sink.astype(F32) * LOG2E
    ok = jnp.logical_and(score_bound <= SCORE_BOUND, jnp.max(jnp.abs(sink2)) <= SCORE_BOUND)
    sink_b = jnp.broadcast_to(sink2[:, None], (Q_HEADS, TM))
    return lax.cond(ok, functools.partial(_window_attn_call, bounded=True),
                    functools.partial(_window_attn_call, bounded=False), qt, k, vt, sink_b)


def _window_attn_call(qt, k, vt, sink_b, *, bounded):
    b, _, l = qt.shape
    p_buf = pltpu.VMEM((2 * TM + 2 * WINDOW, TM), BF16)
    return pl.pallas_call(
        _window_attn_bounded_kernel if bounded else _window_attn_kernel,
        grid=(b, l // TM),
        in_specs=[pl.BlockSpec((None, Q_COLS, TM), lambda bi, i: (bi, 0, i)),
                  pl.BlockSpec((None, l, KV_COLS), lambda bi, i: (bi, 0, 0)),
                  pl.BlockSpec((None, KV_HEADS * VT_ROWS, l), lambda bi, i: (bi, 0, 0)),
                  pl.BlockSpec((Q_HEADS, TM), lambda bi, i: (0, 0))],
        out_specs=pl.BlockSpec((None, TM, Q_COLS), lambda bi, i: (bi, i, 0)),
        out_shape=jax.ShapeDtypeStruct((b, l, Q_COLS), F32),
        scratch_shapes=[p_buf, p_buf] if bounded else [pltpu.VMEM((Q_COLS, TM), F32)],
        compiler_params=_params(2),
        name="window_attn_bounded" if bounded else "window_attn",
    )(qt, k, vt, sink_b)


def _merge_kernel(x_ref, mod_ref, g_ref, oa_ref, oc_ref, b_ref, cu_ref, cup_ref, cun_ref, cw_ref,
                  wgate_ref, bgate_ref, wpa_ref, wpb_ref, wpc_ref, wo_ref, o_ref):
    i = pl.program_id(1)
    n_tiles = pl.num_programs(1)
    x = x_ref[...]
    tm, d = x.shape
    h = _rms_mod(x, g_ref[...], mod_ref[3:4, :], mod_ref[4:5, :]).astype(BF16)
    gates = _sigmoid(jnp.dot(h, wgate_ref[...], preferred_element_type=F32) + bgate_ref[...])

    cu = cu_ref[...]
    has_prev = (i >= 2).astype(F32)
    has_next = jnp.logical_and(i >= 1, i < n_tiles - 1).astype(F32)
    prev_row = cup_ref[7:8, :] * has_prev
    next_row = cun_ref[0:1, :] * has_next
    row = lax.broadcasted_iota(jnp.int32, (tm, 1), 0)
    cu_dn = jnp.where(row == 0, prev_row, pltpu.roll(cu, 1, 0))
    cu_up = jnp.where(row == tm - 1, next_row, pltpu.roll(cu, tm - 1, 0))
    y = cw_ref[0:1, :] * cu_dn + cw_ref[1:2, :] * cu + cw_ref[2:3, :] * cu_up
    o_b = (b_ref[...] * y).astype(BF16)

    pa = jnp.dot(oa_ref[...].astype(BF16), wpa_ref[...], preferred_element_type=F32)
    pb = jnp.dot(o_b, wpb_ref[...], preferred_element_type=F32)
    pc = jnp.dot(oc_ref[...].astype(BF16), wpc_ref[...], preferred_element_type=F32)
    mix = gates[:, 0:d] * pa + gates[:, d:2 * d] * pb + gates[:, 2 * d:3 * d] * pc
    out = jnp.dot(mix.astype(BF16), wo_ref[...], preferred_element_type=F32)
    o_ref[...] = x + mod_ref[5:6, :] * out


def _merge(xs, mod, norm_g, o_a, o_c, bb, cu, conv_w, w_gate, b_gate, w_pa, w_pb, w_pc, w_o,
           *, layer):
    b, l, d = xs.shape
    cw = cu.shape[-1]
    sub = 8
    tok = lambda w: pl.BlockSpec((None, TM, w), lambda bi, i: (bi, i, 0))
    lay = lambda *s: _resident((None,) + s, lambda bi, i: (layer,) + (0,) * len(s))
    return pl.pallas_call(
        _merge_kernel,
        grid=(b, l // TM),
        in_specs=[tok(d),
                  pl.BlockSpec((None, None, N_MOD, d), lambda bi, i: (bi, jnp.minimum(i, 1), 0, 0)),
                  _resident((None, None, 1, d), lambda bi, i: (layer, 1, 0, 0)),
                  tok(Q_COLS), tok(Q_COLS), tok(cw), tok(cw),
                  pl.BlockSpec((None, sub, cw),
                               lambda bi, i: (bi, jnp.maximum(i * (TM // sub) - 1, 0), 0)),
                  pl.BlockSpec((None, sub, cw),
                               lambda bi, i: (bi, jnp.minimum((i + 1) * (TM // sub), l // sub - 1), 0)),
                  lay(3, cw), lay(d, 3 * d), lay(1, 3 * d), lay(Q_COLS, d), lay(cw, d),
                  lay(Q_COLS, d), lay(d, d)],
        out_specs=tok(d),
        out_shape=jax.ShapeDtypeStruct(xs.shape, F32),
        input_output_aliases={0: 0},
        compiler_params=_params(2),
        name="merge",
    )(xs, mod, norm_g, o_a, o_c, bb, cu, cu, cu, conv_w, w_gate, b_gate, w_pa, w_pb, w_pc, w_o)


def _rope_tables(s, n_ctx):
    rows = s // GRID_W
    row = jnp.repeat(jnp.arange(rows), GRID_W).astype(F32)
    col = jnp.tile(jnp.arange(GRID_W), rows).astype(F32)
    half = HEAD_DIM // 2
    inv = ROPE_THETA ** (-jnp.arange(0, half, 2, dtype=F32) / half)
    ang_r = row[:, None] * inv
    ang_c = col[:, None] * inv
    tabs = jnp.concatenate([jnp.cos(ang_r), jnp.sin(ang_r), jnp.cos(ang_c), jnp.sin(ang_c)],
                           axis=-1)
    one, zero = jnp.ones((n_ctx, HEAD_DIM // 4), F32), jnp.zeros((n_ctx, HEAD_DIM // 4), F32)
    ident = jnp.concatenate([one, zero, one, zero], axis=-1)
    return jnp.concatenate([ident, tabs], axis=0).T


def kernel(x, c, ctx, c_ctx, w_ada, b_ada, norm_g, ffn_w_gate, ffn_w_up, ffn_w_down, w_in, qk_g,
           sink_a, conv_w, w_pa, w_pb, w_pc, w_gate, b_gate, w_o):
    bsz, s, d = x.shape
    n_ctx = ctx.shape[1]
    depth = w_ada.shape[0]
    assert n_ctx == TM and s % TM == 0 and s % GRID_W == 0 and bsz < 8

    cvec = jnp.zeros((8, d), F32).at[:bsz].set(c).at[bsz].set(c_ctx)
    mods = _ada(cvec, w_ada, b_ada).reshape(depth, 8, N_MOD, d)
    lat = mods[:, :bsz]
    con = jnp.broadcast_to(mods[:, bsz:bsz + 1], lat.shape)
    mod_all = jnp.stack([con, lat], axis=2)

    rope = _rope_tables(s, n_ctx)
    gmax = jnp.max(jnp.abs(qk_g.astype(F32)), axis=-1)
    vmax = gmax * (HEAD_DIM ** 0.5) * jnp.array([Q_SCALE, 1.0, Q_SCALE, 1.0], F32)
    exps = jnp.clip(jnp.floor(jnp.log2(0.98 * F8_MAX / vmax[:, 2:])), -20.0, 20.0)
    pow2 = jnp.exp2(exps)
    unscale = 1.0 / (pow2[:, 0] * pow2[:, 1])
    fold = jnp.concatenate([jnp.full((depth, 1), Q_SCALE, F32), jnp.ones((depth, 1), F32),
                            Q_SCALE * pow2[:, 0:1], pow2[:, 1:2]], axis=1)[:, :, None]
    qkg = jnp.broadcast_to((qk_g.astype(F32) * fold).reshape(depth, 4 * HEAD_DIM, 1),
                           (depth, 4 * HEAD_DIM, TM))
    ng = norm_g.astype(F32)[:, :, None, :]
    wg, wu, wd = (w.astype(BF16) for w in (ffn_w_gate, ffn_w_up, ffn_w_down))
    w_in_b, w_gate_b, w_pa_b, w_pb_b, w_pc_b, w_o_b = (
        w.astype(BF16) for w in (w_in, w_gate, w_pa, w_pb, w_pc, w_o))
    b_gate3 = b_gate.astype(F32)[:, None, :]

    xs = jnp.concatenate([ctx, x], axis=1)
    for i in range(depth):
        mod = mod_all[i]
        xs = _ffn(xs, mod, ng, wg, wu, wd, layer=i, which=0, j0=0)
        qta, qc8, ka, kc8, vta, vtc, bb, cu = _inproj(xs, mod, ng, w_in_b, qkg, rope, layer=i)
        o_a = _window_attn(qta, ka, vta, sink_a[i], 1.02 * vmax[i, 0] * vmax[i, 1])
        o_c = _global_attn(qc8, kc8, unscale[i], vtc, 1.02 * vmax[i, 2] * vmax[i, 3])
        xs = _merge(xs, mod, ng, o_a, o_c, bb, cu, conv_w.astype(F32), w_gate_b, b_gate3,
                    w_pa_b, w_pb_b, w_pc_b, w_o_b, layer=i)
        xs = _ffn(xs, mod, ng, wg, wu, wd, layer=i, which=1, j0=6)
    return xs[:, n_ctx:, :]
```

```python
import functools

import jax
import jax.numpy as jnp
from jax import lax
from jax.experimental import pallas as pl
from jax.experimental.pallas import tpu as pltpu

GRID_W = 64
HEAD_DIM = 64
Q_HEADS = 8
KV_HEADS = 2
GROUP = Q_HEADS // KV_HEADS
WINDOW = 128
N_MOD = 9
ROPE_THETA = 10000.0
EPS = 1e-6
NEG_INF = -1e30
SCALE = HEAD_DIM ** -0.5
LOG2E = 1.4426950408889634
Q_SCALE = SCALE * LOG2E
SCORE_BOUND = 48.0
Q_COLS = Q_HEADS * HEAD_DIM
KV_COLS = KV_HEADS * HEAD_DIM
VT_ROWS = HEAD_DIM + 16

LANES = 128
TM = 256
FFN_TM = 512
CHUNK = 256
PAIR = 1
LAG = 1
P_BUFS = 2
VMEM_LIMIT = 56 * 1024 * 1024

F32 = jnp.float32
BF16 = jnp.bfloat16
F8 = jnp.float8_e4m3fn
F8_MAX = 448.0
F8_ROWS = 4 * HEAD_DIM


def _params(n_axes, flags=None):
    return pltpu.CompilerParams(dimension_semantics=("arbitrary",) * n_axes,
                                vmem_limit_bytes=VMEM_LIMIT, flags=flags)


def _resident(shape, index_map):
    return pl.BlockSpec(shape, index_map, pipeline_mode=pl.Buffered(1))


def _sigmoid(v):
    return 1.0 / (1.0 + jnp.exp(-v))


def _rms_mod(x, g, shift, scale):
    y = x * lax.rsqrt(jnp.mean(x * x, axis=-1, keepdims=True) + EPS)
    return (y * g) * (1.0 + scale) + shift


def _ada_kernel(c_ref, w_ref, b_ref, o_ref):
    c = c_ref[...]
    s = (c * _sigmoid(c)).astype(BF16)
    o_ref[...] = jnp.dot(s, w_ref[...].astype(BF16), preferred_element_type=F32) + b_ref[...]


def _ada(cvec, w_ada, b_ada):
    depth, d, n = w_ada.shape
    tn = 1024
    return pl.pallas_call(
        _ada_kernel,
        grid=(depth, n // tn),
        in_specs=[pl.BlockSpec((8, d), lambda l, j: (0, 0)),
                  pl.BlockSpec((None, d, tn), lambda l, j: (l, 0, j)),
                  pl.BlockSpec((None, 1, tn), lambda l, j: (l, 0, j))],
        out_specs=pl.BlockSpec((None, 8, tn), lambda l, j: (l, 0, j)),
        out_shape=jax.ShapeDtypeStruct((depth, 8, n), F32),
        compiler_params=_params(2),
        name="ada",
    )(cvec, w_ada, b_ada.reshape(depth, 1, n))


def _ffn_kernel(x_ref, mod_ref, g_ref, wg_ref, wu_ref, wd_ref, o_ref, *, j0):
    xb = x_ref[...].reshape(x_ref.shape[-2:])
    outs = []
    for r in range(0, xb.shape[0], TM):
        x = xb[r:r + TM, :]
        h = _rms_mod(x, g_ref[...], mod_ref[j0:j0 + 1, :], mod_ref[j0 + 1:j0 + 2, :]).astype(BF16)
        a = jnp.dot(h, wg_ref[...], preferred_element_type=F32)
        u = jnp.dot(h, wu_ref[...], preferred_element_type=F32)
        act = (a * _sigmoid(a) * u).astype(BF16)
        y = jnp.dot(act, wd_ref[...], preferred_element_type=F32)
        outs.append(x + (0.5 * mod_ref[j0 + 2:j0 + 3, :]) * y)
    o_ref[...] = jnp.concatenate(outs, axis=0).reshape(o_ref.shape)


def _ffn(xs, mod, norm_g, wg, wu, wd, *, layer, which, j0, latents_out=False):
    b, l, d = xs.shape
    dff = wg.shape[-1]

    def call(xs, x_spec, n_steps, stream, out_spec=None, out_rows=None):
        return pl.pallas_call(
            functools.partial(_ffn_kernel, j0=j0),
            grid=(b, n_steps),
            in_specs=[x_spec,
                      pl.BlockSpec((None, None, N_MOD, d), lambda bi, i: (bi, stream, 0, 0)),
                      _resident((None, None, 1, d), lambda bi, i: (layer, 2 * which, 0, 0)),
                      _resident((None, None, d, dff), lambda bi, i: (layer, which, 0, 0)),
                      _resident((None, None, d, dff), lambda bi, i: (layer, which, 0, 0)),
                      _resident((None, None, dff, d), lambda bi, i: (layer, which, 0, 0))],
            out_specs=x_spec if out_spec is None else out_spec,
            out_shape=jax.ShapeDtypeStruct((b, l if out_spec is None else out_rows, d), F32),
            input_output_aliases={0: 0} if out_spec is None else {},
            compiler_params=_params(2),
            name="ffn_ctx" if stream == 0 else "ffn",
        )(xs, mod, norm_g, wg, wu, wd)

    tl = FFN_TM if (l - TM) % FFN_TM == 0 else TM
    lat_spec = pl.BlockSpec((pl.Element(1), pl.Element(tl), pl.Element(d)),
                            lambda bi, i: (bi, pl.multiple_of(TM + i * tl, TM), 0))
    if latents_out:
        return call(xs, lat_spec, (l - TM) // tl, 1,
                    pl.BlockSpec((None, tl, d), lambda bi, i: (bi, i, 0)), l - TM)
    xs = call(xs, pl.BlockSpec((None, TM, d), lambda bi, i: (bi, 0, 0)), 1, 0)
    return call(xs, lat_spec, (l - TM) // tl, 1)


def _norm_rope_t(tt, gain, rope):
    ms = jnp.sum(tt * tt, axis=0, keepdims=True) * (1.0 / HEAD_DIM)
    y = tt * lax.rsqrt(ms + EPS) * gain
    q = HEAD_DIM // 4
    cr, sr, cc, sc = (rope[j * q:(j + 1) * q, :] for j in range(4))
    y1, y2, y3, y4 = (y[j * q:(j + 1) * q, :] for j in range(4))
    return jnp.concatenate([y1 * cr - y2 * sr, y2 * cr + y1 * sr,
                            y3 * cc - y4 * sc, y4 * cc + y3 * sc], axis=0)


def _split8(y):
    hi = y.astype(F8).astype(F32)
    return hi, y - hi


def _inproj_kernel(x_ref, mod_ref, g_ref, w_ref, qkg_ref, rope_ref,
                   qta_ref, qc8_ref, ka_ref, kc8_ref, vta_ref, vtc_ref, b_ref, cu_ref):
    x = x_ref[...]
    h = _rms_mod(x, g_ref[...], mod_ref[3:4, :], mod_ref[4:5, :]).astype(BF16)
    z = jnp.dot(h, w_ref[...], preferred_element_type=F32)
    rope = rope_ref[...]

    def heads(lo, j):
        tt = z[:, lo:lo + LANES].T
        gain = qkg_ref[j * HEAD_DIM:(j + 1) * HEAD_DIM, :]
        return [_norm_rope_t(tt[0:HEAD_DIM, :], gain, rope),
                _norm_rope_t(tt[HEAD_DIM:, :], gain, rope)]

    nr = lambda lo, j: jnp.concatenate(heads(lo, j), axis=0)
    zeros = jnp.zeros((HEAD_DIM, x.shape[0]), F32)

    ka_ref[...] = nr(0, 1).T.astype(BF16)
    for g, y in enumerate(heads(256, 3)):
        hi, lo = _split8(y)
        kc8_ref[g] = jnp.concatenate([hi, hi, lo, zeros], axis=0).T.astype(F8)
    ones = jnp.ones((VT_ROWS - HEAD_DIM, x.shape[0]), BF16)
    for vt_ref, lo in ((vta_ref, 128), (vtc_ref, 384)):
        vt = z[:, lo:lo + KV_COLS].T.astype(BF16)
        for g in range(KV_HEADS):
            vt_ref[g * VT_ROWS:g * VT_ROWS + HEAD_DIM, :] = vt[g * HEAD_DIM:(g + 1) * HEAD_DIM, :]
            vt_ref[g * VT_ROWS + HEAD_DIM:(g + 1) * VT_ROWS, :] = ones
    q0 = 4 * KV_COLS
    for c in range(Q_COLS // LANES):
        qta_ref[c * LANES:(c + 1) * LANES, :] = nr(q0 + c * LANES, 0).astype(BF16)
        for j, y in enumerate(heads(q0 + Q_COLS + c * LANES, 2)):
            hi, lo = _split8(y)
            r0 = (2 * c + j) * F8_ROWS
            qc8_ref[r0:r0 + F8_ROWS, :] = jnp.concatenate([hi, lo, hi, zeros], axis=0).astype(F8)
    c0 = q0 + 2 * Q_COLS
    cw = b_ref.shape[-1]
    b_ref[...] = z[:, c0:c0 + cw]
    cu_ref[...] = z[:, c0 + cw:c0 + 2 * cw] * z[:, c0 + 2 * cw:c0 + 3 * cw]


def _inproj(xs, mod, norm_g, w_in, qkg, rope, *, layer):
    b, l, d = xs.shape
    ncol = w_in.shape[-1]
    cw = (ncol - 4 * KV_COLS - 2 * Q_COLS) // 3
    tok = lambda w: pl.BlockSpec((None, TM, w), lambda bi, i: (bi, i, 0))
    tr = lambda w: pl.BlockSpec((None, w, TM), lambda bi, i: (bi, 0, i))
    return pl.pallas_call(
        _inproj_kernel,
        grid=(b, l // TM),
        in_specs=[tok(d),
                  pl.BlockSpec((None, None, N_MOD, d), lambda bi, i: (bi, jnp.minimum(i, 1), 0, 0)),
                  _resident((None, None, 1, d), lambda bi, i: (layer, 1, 0, 0)),
                  _resident((None, d, ncol), lambda bi, i: (layer, 0, 0)),
                  _resident((None, 4 * HEAD_DIM, TM), lambda bi, i: (layer, 0, 0)),
                  pl.BlockSpec((HEAD_DIM, TM), lambda bi, i: (0, i))],
        out_specs=[tr(Q_COLS), tr(Q_HEADS * F8_ROWS), tok(KV_COLS),
                   pl.BlockSpec((None, KV_HEADS, TM, F8_ROWS), lambda bi, i: (bi, 0, i, 0)),
                   tr(KV_HEADS * VT_ROWS), tr(KV_HEADS * VT_ROWS), tok(cw), tok(cw)],
        out_shape=[jax.ShapeDtypeStruct((b, Q_COLS, l), BF16),
                   jax.ShapeDtypeStruct((b, Q_HEADS * F8_ROWS, l), F8),
                   jax.ShapeDtypeStruct((b, l, KV_COLS), BF16),
                   jax.ShapeDtypeStruct((b, KV_HEADS, l, F8_ROWS), F8),
                   jax.ShapeDtypeStruct((b, KV_HEADS * VT_ROWS, l), BF16),
                   jax.ShapeDtypeStruct((b, KV_HEADS * VT_ROWS, l), BF16),
                   jax.ShapeDtypeStruct((b, l, cw), F32),
                   jax.ShapeDtypeStruct((b, l, cw), F32)],
        compiler_params=_params(2),
        name="inproj",
    )(xs, mod, norm_g, w_in, qkg, rope)


def _pad_q(q, g):
    z = jnp.zeros_like(q)
    return jnp.concatenate([q, z] if g == 0 else [z, q], axis=0)


def _score_chunk(q8_ref, k8_ref, un_ref, start, ch, h):
    s = jnp.dot(k8_ref[h // GROUP, pl.ds(start, ch), :], q8_ref[h * F8_ROWS:(h + 1) * F8_ROWS, :],
                preferred_element_type=F32)
    return s * un_ref[...]


def _attend(q8_ref, k8_ref, un_ref, vt_ref, acc_ref, m_ref, al_ref, smax_ref, s_bufs, p_bufs, *,
            n_keys, tk):
    n_kt = n_keys // tk
    arows = lambda h: slice(h * VT_ROWS, (h + 1) * VT_ROWS)
    m_ref[...] = jnp.full(m_ref.shape, NEG_INF, F32)
    acc_ref[...] = jnp.zeros(acc_ref.shape, F32)

    ch = min(tk, CHUNK)
    n_ch = tk // ch

    def key_start(t, c):
        return t * tk + c * ch if isinstance(t, int) else pl.multiple_of(t * tk + c * ch, ch)

    def step(sc=None, sm=None, va=None):
        if sm is not None:
            m = m_ref[sm:sm + 1, :]
            m_new = jnp.maximum(m, smax_ref[sm:sm + 1, :])
            m_ref[sm:sm + 1, :] = m_new
            al_ref[sm:sm + 1, :] = jnp.exp2(m - m_new)
        smax = pv = None
        for c in range(n_ch):
            rows = slice(c * ch, (c + 1) * ch)
            if sc is not None:
                t, h = sc
                s = _score_chunk(q8_ref, k8_ref, un_ref, key_start(t, c), ch, h)
                s_bufs[h % 2][rows, :] = s
                cmax = jnp.max(s, axis=0, keepdims=True)
                smax = cmax if smax is None else jnp.maximum(smax, cmax)
            if sm is not None:
                p_bufs[sm % 2][rows, :] = jnp.exp2(s_bufs[sm % 2][rows, :] - m_new).astype(BF16)
            if va is not None:
                t, h = va
                g = h // GROUP
                vt = vt_ref[g * VT_ROWS:(g + 1) * VT_ROWS, pl.ds(key_start(t, c), ch)]
                d = jnp.dot(vt, p_bufs[h % 2][rows, :], preferred_element_type=F32)
                pv = d if pv is None else pv + d
        if sc is not None:
            smax_ref[sc[1]:sc[1] + 1, :] = smax
        if va is not None:
            h = va[1]
            acc_ref[arows(h), :] = al_ref[h:h + 1, :] * acc_ref[arows(h), :] + pv

    def tile_steps(t, last):
        for h in range(Q_HEADS):
            if h + 2 < Q_HEADS:
                sc = (t, h + 2)
            else:
                sc = None if last else (t + 1, h + 2 - Q_HEADS)
            if h + 1 < Q_HEADS:
                sm = h + 1
            else:
                sm = None if last else 0
            step(sc, sm, (t, h))

    step(sc=(0, 0))
    step(sc=(0, 1), sm=0)
    if n_kt > 1:
        def body(t, carry):
            tile_steps(t, False)
            return carry
        lax.fori_loop(0, n_kt - 1, body, 0)
    tile_steps(n_kt - 1, True)


def _attend_bounded(q8_ref, k8_ref, un_ref, vt_ref, acc_ref, p_bufs, *, n_keys, tk):
    n_kt = n_keys // tk
    n_buf = len(p_bufs)
    n_items = Q_HEADS // PAIR
    hrows = lambda h: slice(h * F8_ROWS, (h + 1) * F8_ROWS)
    arows = lambda j: slice(j * VT_ROWS, (j + 1) * VT_ROWS)
    qp = [jnp.concatenate([q8_ref[hrows(PAIR * j + e), :] for e in range(PAIR)], axis=1)
          for j in range(n_items)]
    un = jnp.concatenate([un_ref[...]] * PAIR, axis=1)
    acc_ref[...] = jnp.zeros(acc_ref.shape, F32)
    ch = min(tk, CHUNK // PAIR)
    n_ch = tk // ch

    def key_start(t, c):
        return t * tk + c * ch if isinstance(t, int) else pl.multiple_of(t * tk + c * ch, ch)

    def step(sc=None, va=None):
        if va is not None:
            t, j = va
            g = PAIR * j // GROUP
            vt = vt_ref[g * VT_ROWS:(g + 1) * VT_ROWS, pl.ds(key_start(t, 0), tk)]
            acc_ref[arows(j), :] = acc_ref[arows(j), :] + jnp.dot(
                vt, p_bufs[j % n_buf][0:tk, :], preferred_element_type=F32)
        if sc is not None:
            t, j = sc
            g = PAIR * j // GROUP
            for c in range(n_ch):
                s = jnp.dot(k8_ref[g, pl.ds(key_start(t, c), ch), :], qp[j],
                            preferred_element_type=F32) * un
                p_bufs[j % n_buf][c * ch:(c + 1) * ch, :] = jnp.exp2(s).astype(BF16)

    def tile_steps(t, last):
        for j in range(n_items):
            if j + LAG < n_items:
                sc = (t, j + LAG)
            else:
                sc = None if last else (t + 1, j + LAG - n_items)
            step(sc, (t, j))

    for j in range(LAG):
        step(sc=(0, j))
    if n_kt > 1:
        def body(t, carry):
            tile_steps(t, False)
            return carry
        lax.fori_loop(0, n_kt - 1, body, 0)
    tile_steps(n_kt - 1, True)


def _global_attn_kernel(q8_ref, k8_ref, un_ref, vt_ref, o_ref, acc_ref, *scratch, tk, bounded):
    i = pl.program_id(1)
    if bounded:
        attend = functools.partial(_attend_bounded, q8_ref, k8_ref, un_ref, vt_ref, acc_ref, scratch)
    else:
        m_ref, al_ref, smax_ref, s0_ref, s1_ref, p0_ref, p1_ref = scratch
        attend = functools.partial(_attend, q8_ref, k8_ref, un_ref, vt_ref, acc_ref, m_ref, al_ref,
                                   smax_ref, (s0_ref, s1_ref), (p0_ref, p1_ref))

    @pl.when(i == 0)
    def _():
        attend(n_keys=TM, tk=TM)

    @pl.when(i > 0)
    def _():
        attend(n_keys=k8_ref.shape[1], tk=tk)

    tq = o_ref.shape[0]
    per_block = PAIR if bounded else 1
    outs = []
    for h in range(Q_HEADS):
        j, e = divmod(h, per_block)
        a = acc_ref[j * VT_ROWS:(j + 1) * VT_ROWS, e * tq:(e + 1) * tq]
        outs.append(a[0:HEAD_DIM, :] / a[HEAD_DIM:HEAD_DIM + 1, :])
    o_ref[...] = jnp.concatenate(outs, axis=0).T


def _key_tile(l):
    return next(t for t in (1280, 1024, 768, 512, 256) if l % t == 0)


def _global_attn(q8, k8, unscale, vt, score_bound):
    un = jnp.broadcast_to(unscale.astype(F32), (1, TM))
    return lax.cond(score_bound <= SCORE_BOUND,
                    functools.partial(_global_attn_call, bounded=True),
                    functools.partial(_global_attn_call, bounded=False), q8, k8, un, vt)


def _global_attn_call(q8, k8, un, vt, *, bounded):
    b, _, l = q8.shape
    tk = _key_tile(l)
    stat = pltpu.VMEM((Q_HEADS, TM), F32)
    p_buf = pltpu.VMEM((tk, TM), BF16)
    s_buf = pltpu.VMEM((tk, TM), F32)
    if bounded:
        scratch = [pltpu.VMEM((Q_HEADS // PAIR * VT_ROWS, PAIR * TM), F32)]
        scratch += [pltpu.VMEM((tk, PAIR * TM), BF16)] * P_BUFS
    else:
        scratch = [pltpu.VMEM((Q_HEADS * VT_ROWS, TM), F32),
                   stat, stat, stat, s_buf, s_buf, p_buf, p_buf]
    return pl.pallas_call(
        functools.partial(_global_attn_kernel, tk=tk, bounded=bounded),
        grid=(b, l // TM),
        in_specs=[pl.BlockSpec((None, Q_HEADS * F8_ROWS, TM), lambda bi, i: (bi, 0, i)),
                  pl.BlockSpec((None, KV_HEADS, l, F8_ROWS), lambda bi, i: (bi, 0, 0, 0)),
                  pl.BlockSpec((1, TM), lambda bi, i: (0, 0)),
                  pl.BlockSpec((None, KV_HEADS * VT_ROWS, l), lambda bi, i: (bi, 0, 0))],
        out_specs=pl.BlockSpec((None, TM, Q_COLS), lambda bi, i: (bi, i, 0)),
        out_shape=jax.ShapeDtypeStruct((b, l, Q_COLS), F32),
        scratch_shapes=scratch,
        compiler_params=_params(2),
        name="global_attn_bounded" if bounded else "global_attn",
    )(q8, k8, un, vt)


def _window_geometry(i, tq, l_all):
    span = tq + 2 * WINDOW
    start = pl.multiple_of(jnp.clip(i * tq - WINDOW, 0, l_all - span), LANES)
    kpos = start - TM + lax.broadcasted_iota(jnp.int32, (span, 1), 0)
    qbase = jnp.where(i >= 1, i * tq - TM, -(1 << 20))
    qpos = qbase + lax.broadcasted_iota(jnp.int32, (1, tq), 1)
    ok = jnp.logical_and(kpos >= 0, jnp.abs(qpos - kpos) <= WINDOW)
    return start, jnp.where(ok, 0.0, NEG_INF).astype(F32)


def _window_attn_kernel(qt_ref, k_ref, vt_ref, sink_ref, o_ref, acc_ref):
    tq = qt_ref.shape[-1]
    span = tq + 2 * WINDOW
    start, bias = _window_geometry(pl.program_id(1), tq, k_ref.shape[0])
    k_ctx = k_ref[0:TM, :]
    k_win = k_ref[pl.ds(start, span), :]
    for h in range(Q_HEADS):
        g = h // GROUP
        q = _pad_q(qt_ref[h * HEAD_DIM:(h + 1) * HEAD_DIM, :], g)
        s1 = jnp.dot(k_ctx, q, preferred_element_type=F32)
        s2 = jnp.dot(k_win, q, preferred_element_type=F32) + bias
        sink = sink_ref[h:h + 1, :]
        m = jnp.maximum(jnp.maximum(jnp.max(s1, axis=0, keepdims=True),
                                    jnp.max(s2, axis=0, keepdims=True)), sink)
        p1 = jnp.exp2(s1 - m)
        p2 = jnp.exp2(s2 - m)
        l = (jnp.sum(p1, axis=0, keepdims=True) + jnp.sum(p2, axis=0, keepdims=True)
             + jnp.exp2(sink - m))
        rows = slice(g * VT_ROWS, g * VT_ROWS + HEAD_DIM)
        acc = (jnp.dot(vt_ref[rows, 0:TM], p1.astype(BF16), preferred_element_type=F32)
               + jnp.dot(vt_ref[rows, pl.ds(start, span)], p2.astype(BF16),
                         preferred_element_type=F32))
        acc_ref[h * HEAD_DIM:(h + 1) * HEAD_DIM, :] = acc / l
    o_ref[...] = acc_ref[...].T


def _window_attn_bounded_kernel(qt_ref, k_ref, vt_ref, sink_ref, o_ref, p0_ref, p1_ref):
    tq = qt_ref.shape[-1]
    start, bias = _window_geometry(pl.program_id(1), tq, k_ref.shape[0])
    p_bufs = (p0_ref, p1_ref)
    n_win = bias.shape[0] // CHUNK
    chunks = [(0, None)] + [(pl.multiple_of(start + c * CHUNK, LANES),
                             bias[c * CHUNK:(c + 1) * CHUNK, :]) for c in range(n_win)]
    qs = [_pad_q(qt_ref[h * HEAD_DIM:(h + 1) * HEAD_DIM, :], h // GROUP) for h in range(Q_HEADS)]
    accs = [None] * Q_HEADS

    def step(sc=None, va=None):
        pv = None
        for ci, (k0, mask) in enumerate(chunks):
            rows = slice(ci * CHUNK, (ci + 1) * CHUNK)
            if sc is not None:
                s = jnp.dot(k_ref[pl.ds(k0, CHUNK), :], qs[sc], preferred_element_type=F32)
                if mask is not None:
                    s = s + mask
                p_bufs[sc % 2][rows, :] = jnp.exp2(s).astype(BF16)
            if va is not None:
                g = va // GROUP
                vt = vt_ref[g * VT_ROWS:(g + 1) * VT_ROWS, pl.ds(k0, CHUNK)]
                d = jnp.dot(vt, p_bufs[va % 2][rows, :], preferred_element_type=F32)
                pv = d if pv is None else pv + d
        if va is not None:
            accs[va] = pv

    step(sc=0)
    for h in range(Q_HEADS):
        step(sc=h + 1 if h + 1 < Q_HEADS else None, va=h)
    outs = []
    for h in range(Q_HEADS):
        l = accs[h][HEAD_DIM:HEAD_DIM + 1, :] + jnp.exp2(sink_ref[h:h + 1, :])
        outs.append(accs[h][0:HEAD_DIM, :] / l)
    o_ref[...] = jnp.concatenate(outs, axis=0).T


def _window_attn(qt, k, vt, sink, score_bound):
    sink2 = sink.astype(F32) * LOG2E
    ok = jnp.logical_and(score_bound <= SCORE_BOUND, jnp.max(jnp.abs(sink2)) <= SCORE_BOUND)
    sink_b = jnp.broadcast_to(sink2[:, None], (Q_HEADS, TM))
    return lax.cond(ok, functools.partial(_window_attn_call, bounded=True),
                    functools.partial(_window_attn_call, bounded=False), qt, k, vt, sink_b)


def _window_attn_call(qt, k, vt, sink_b, *, bounded):
    b, _, l = qt.shape
    p_buf = pltpu.VMEM((2 * TM + 2 * WINDOW, TM), BF16)
    return pl.pallas_call(
        _window_attn_bounded_kernel if bounded else _window_attn_kernel,
        grid=(b, l // TM),
        in_specs=[pl.BlockSpec((None, Q_COLS, TM), lambda bi, i: (bi, 0, i)),
                  pl.BlockSpec((None, l, KV_COLS), lambda bi, i: (bi, 0, 0)),
                  pl.BlockSpec((None, KV_HEADS * VT_ROWS, l), lambda bi, i: (bi, 0, 0)),
                  pl.BlockSpec((Q_HEADS, TM), lambda bi, i: (0, 0))],
        out_specs=pl.BlockSpec((None, TM, Q_COLS), lambda bi, i: (bi, i, 0)),
        out_shape=jax.ShapeDtypeStruct((b, l, Q_COLS), F32),
        scratch_shapes=[p_buf, p_buf] if bounded else [pltpu.VMEM((Q_COLS, TM), F32)],
        compiler_params=_params(2),
        name="window_attn_bounded" if bounded else "window_attn",
    )(qt, k, vt, sink_b)


def _merge_kernel(x_ref, mod_ref, g_ref, oa_ref, oc_ref, b_ref, cu_ref, cup_ref, cun_ref, cw_ref,
                  wgate_ref, bgate_ref, wpa_ref, wpb_ref, wpc_ref, wo_ref, o_ref):
    i = pl.program_id(1)
    n_tiles = pl.num_programs(1)
    x = x_ref[...]
    tm, d = x.shape
    h = _rms_mod(x, g_ref[...], mod_ref[3:4, :], mod_ref[4:5, :]).astype(BF16)
    gates = _sigmoid(jnp.dot(h, wgate_ref[...], preferred_element_type=F32) + bgate_ref[...])

    cu = cu_ref[...]
    has_prev = (i >= 2).astype(F32)
    has_next = jnp.logical_and(i >= 1, i < n_tiles - 1).astype(F32)
    prev_row = cup_ref[7:8, :] * has_prev
    next_row = cun_ref[0:1, :] * has_next
    row = lax.broadcasted_iota(jnp.int32, (tm, 1), 0)
    cu_dn = jnp.where(row == 0, prev_row, pltpu.roll(cu, 1, 0))
    cu_up = jnp.where(row == tm - 1, next_row, pltpu.roll(cu, tm - 1, 0))
    y = cw_ref[0:1, :] * cu_dn + cw_ref[1:2, :] * cu + cw_ref[2:3, :] * cu_up
    o_b = (b_ref[...] * y).astype(BF16)

    pa = jnp.dot(oa_ref[...].astype(BF16), wpa_ref[...], preferred_element_type=F32)
    pb = jnp.dot(o_b, wpb_ref[...], preferred_element_type=F32)
    pc = jnp.dot(oc_ref[...].astype(BF16), wpc_ref[...], preferred_element_type=F32)
    mix = gates[:, 0:d] * pa + gates[:, d:2 * d] * pb + gates[:, 2 * d:3 * d] * pc
    out = jnp.dot(mix.astype(BF16), wo_ref[...], preferred_element_type=F32)
    o_ref[...] = x + mod_ref[5:6, :] * out


def _merge(xs, mod, norm_g, o_a, o_c, bb, cu, conv_w, w_gate, b_gate, w_pa, w_pb, w_pc, w_o,
           *, layer):
    b, l, d = xs.shape
    cw = cu.shape[-1]
    sub = 8
    tok = lambda w: pl.BlockSpec((None, TM, w), lambda bi, i: (bi, i, 0))
    lay = lambda *s: _resident((None,) + s, lambda bi, i: (layer,) + (0,) * len(s))
    return pl.pallas_call(
        _merge_kernel,
        grid=(b, l // TM),
        in_specs=[tok(d),
                  pl.BlockSpec((None, None, N_MOD, d), lambda bi, i: (bi, jnp.minimum(i, 1), 0, 0)),
                  _resident((None, None, 1, d), lambda bi, i: (layer, 1, 0, 0)),
                  tok(Q_COLS), tok(Q_COLS), tok(cw), tok(cw),
                  pl.BlockSpec((None, sub, cw),
                               lambda bi, i: (bi, jnp.maximum(i * (TM // sub) - 1, 0), 0)),
                  pl.BlockSpec((None, sub, cw),
                               lambda bi, i: (bi, jnp.minimum((i + 1) * (TM // sub), l // sub - 1), 0)),
                  lay(3, cw), lay(d, 3 * d), lay(1, 3 * d), lay(Q_COLS, d), lay(cw, d),
                  lay(Q_COLS, d), lay(d, d)],
        out_specs=tok(d),
        out_shape=jax.ShapeDtypeStruct(xs.shape, F32),
        input_output_aliases={0: 0},
        compiler_params=_params(2),
        name="merge",
    )(xs, mod, norm_g, o_a, o_c, bb, cu, cu, cu, conv_w, w_gate, b_gate, w_pa, w_pb, w_pc, w_o)


def _rope_tables(s, n_ctx):
    rows = s // GRID_W
    row = jnp.repeat(jnp.arange(rows), GRID_W).astype(F32)
    col = jnp.tile(jnp.arange(GRID_W), rows).astype(F32)
    half = HEAD_DIM // 2
    inv = ROPE_THETA ** (-jnp.arange(0, half, 2, dtype=F32) / half)
    ang_r = row[:, None] * inv
    ang_c = col[:, None] * inv
    tabs = jnp.concatenate([jnp.cos(ang_r), jnp.sin(ang_r), jnp.cos(ang_c), jnp.sin(ang_c)],
                           axis=-1)
    one, zero = jnp.ones((n_ctx, HEAD_DIM // 4), F32), jnp.zeros((n_ctx, HEAD_DIM // 4), F32)
    ident = jnp.concatenate([one, zero, one, zero], axis=-1)
    return jnp.concatenate([ident, tabs], axis=0).T


def kernel(x, c, ctx, c_ctx, w_ada, b_ada, norm_g, ffn_w_gate, ffn_w_up, ffn_w_down, w_in, qk_g,
           sink_a, conv_w, w_pa, w_pb, w_pc, w_gate, b_gate, w_o):
    bsz, s, d = x.shape
    n_ctx = ctx.shape[1]
    depth = w_ada.shape[0]
    assert n_ctx == TM and s % TM == 0 and s % GRID_W == 0 and bsz < 8

    cvec = jnp.zeros((8, d), F32).at[:bsz].set(c).at[bsz].set(c_ctx)
    mods = _ada(cvec, w_ada, b_ada).reshape(depth, 8, N_MOD, d)
    lat = mods[:, :bsz]
    con = jnp.broadcast_to(mods[:, bsz:bsz + 1], lat.shape)
    mod_all = jnp.stack([con, lat], axis=2)

    rope = _rope_tables(s, n_ctx)
    gmax = jnp.max(jnp.abs(qk_g.astype(F32)), axis=-1)
    vmax = gmax * (HEAD_DIM ** 0.5) * jnp.array([Q_SCALE, 1.0, Q_SCALE, 1.0], F32)
    exps = jnp.clip(jnp.floor(jnp.log2(0.98 * F8_MAX / vmax[:, 2:])), -20.0, 20.0)
    pow2 = jnp.exp2(exps)
    unscale = 1.0 / (pow2[:, 0] * pow2[:, 1])
    fold = jnp.concatenate([jnp.full((depth, 1), Q_SCALE, F32), jnp.ones((depth, 1), F32),
                            Q_SCALE * pow2[:, 0:1], pow2[:, 1:2]], axis=1)[:, :, None]
    qkg = jnp.broadcast_to((qk_g.astype(F32) * fold).reshape(depth, 4 * HEAD_DIM, 1),
                           (depth, 4 * HEAD_DIM, TM))
    ng = norm_g.astype(F32)[:, :, None, :]
    wg, wu, wd = (w.astype(BF16) for w in (ffn_w_gate, ffn_w_up, ffn_w_down))
    w_in_b, w_gate_b, w_pa_b, w_pb_b, w_pc_b, w_o_b = (
        w.astype(BF16) for w in (w_in, w_gate, w_pa, w_pb, w_pc, w_o))
    b_gate3 = b_gate.astype(F32)[:, None, :]

    xs = jnp.concatenate([ctx, x], axis=1)
    for i in range(depth):
        mod = mod_all[i]
        xs = _ffn(xs, mod, ng, wg, wu, wd, layer=i, which=0, j0=0)
        qta, qc8, ka, kc8, vta, vtc, bb, cu = _inproj(xs, mod, ng, w_in_b, qkg, rope, layer=i)
        o_a = _window_attn(qta, ka, vta, sink_a[i], 1.02 * vmax[i, 0] * vmax[i, 1])
        o_c = _global_attn(qc8, kc8, unscale[i], vtc, 1.02 * vmax[i, 2] * vmax[i, 3])
        xs = _merge(xs, mod, ng, o_a, o_c, bb, cu, conv_w.astype(F32), w_gate_b, b_gate3,
                    w_pa_b, w_pb_b, w_pc_b, w_o_b, layer=i)
        xs = _ffn(xs, mod, ng, wg, wu, wd, layer=i, which=1, j0=6, latents_out=i + 1 == depth)
    return xs
```

```python
import functools

import jax
import jax.numpy as jnp
from jax import lax
from jax.experimental import pallas as pl
from jax.experimental.pallas import tpu as pltpu

GRID_W = 64
HEAD_DIM = 64
Q_HEADS = 8
KV_HEADS = 2
GROUP = Q_HEADS // KV_HEADS
WINDOW = 128
N_MOD = 9
ROPE_THETA = 10000.0
EPS = 1e-6
NEG_INF = -1e30
SCALE = HEAD_DIM ** -0.5
LOG2E = 1.4426950408889634
Q_SCALE = SCALE * LOG2E
SCORE_BOUND = 48.0
Q_COLS = Q_HEADS * HEAD_DIM
KV_COLS = KV_HEADS * HEAD_DIM
VT_ROWS = HEAD_DIM + 16

LANES = 128
TM = 256
FFN_TM = 512
CHUNK = 256
P_BUFS = 2
WIN_LAG = 2
VMEM_LIMIT = 56 * 1024 * 1024

F32 = jnp.float32
BF16 = jnp.bfloat16
F8 = jnp.float8_e4m3fn
F8_MAX = 448.0
F8_ROWS = 4 * HEAD_DIM


def _params(n_axes):
    return pltpu.CompilerParams(dimension_semantics=("arbitrary",) * n_axes,
                                vmem_limit_bytes=VMEM_LIMIT)


def _resident(shape, index_map):
    return pl.BlockSpec(shape, index_map, pipeline_mode=pl.Buffered(1))


def _sigmoid(v):
    return 1.0 / (1.0 + jnp.exp(-v))


def _rms_mod(x, g, shift, scale):
    y = x * lax.rsqrt(jnp.mean(x * x, axis=-1, keepdims=True) + EPS)
    return (y * g) * (1.0 + scale) + shift


def _ada_kernel(c_ref, w_ref, b_ref, o_ref):
    c = c_ref[...]
    s = (c * _sigmoid(c)).astype(BF16)
    o_ref[...] = jnp.dot(s, w_ref[...].astype(BF16), preferred_element_type=F32) + b_ref[...]


def _ada(cvec, w_ada, b_ada):
    depth, d, n = w_ada.shape
    tn = 1024
    return pl.pallas_call(
        _ada_kernel,
        grid=(depth, n // tn),
        in_specs=[pl.BlockSpec((8, d), lambda l, j: (0, 0)),
                  pl.BlockSpec((None, d, tn), lambda l, j: (l, 0, j)),
                  pl.BlockSpec((None, 1, tn), lambda l, j: (l, 0, j))],
        out_specs=pl.BlockSpec((None, 8, tn), lambda l, j: (l, 0, j)),
        out_shape=jax.ShapeDtypeStruct((depth, 8, n), F32),
        compiler_params=_params(2),
        name="ada",
    )(cvec, w_ada, b_ada.reshape(depth, 1, n))


def _ffn_kernel(x_ref, mod_ref, g_ref, wg_ref, wu_ref, wd_ref, o_ref, *, j0):
    xb = x_ref[...].reshape(x_ref.shape[-2:])
    outs = []
    for r in range(0, xb.shape[0], TM):
        x = xb[r:r + TM, :]
        h = _rms_mod(x, g_ref[...], mod_ref[j0:j0 + 1, :], mod_ref[j0 + 1:j0 + 2, :]).astype(BF16)
        a = jnp.dot(h, wg_ref[...], preferred_element_type=F32)
        u = jnp.dot(h, wu_ref[...], preferred_element_type=F32)
        act = (a * _sigmoid(a) * u).astype(BF16)
        y = jnp.dot(act, wd_ref[...], preferred_element_type=F32)
        outs.append(x + (0.5 * mod_ref[j0 + 2:j0 + 3, :]) * y)
    o_ref[...] = jnp.concatenate(outs, axis=0).reshape(o_ref.shape)


def _ffn(xs, mod, norm_g, wg, wu, wd, *, layer, which, j0, latents_out=False):
    b, l, d = xs.shape
    dff = wg.shape[-1]

    def call(xs, x_spec, n_steps, stream, out_spec=None, out_rows=None):
        return pl.pallas_call(
            functools.partial(_ffn_kernel, j0=j0),
            grid=(b, n_steps),
            in_specs=[x_spec,
                      pl.BlockSpec((None, None, N_MOD, d), lambda bi, i: (bi, stream, 0, 0)),
                      _resident((None, None, 1, d), lambda bi, i: (layer, 2 * which, 0, 0)),
                      _resident((None, None, d, dff), lambda bi, i: (layer, which, 0, 0)),
                      _resident((None, None, d, dff), lambda bi, i: (layer, which, 0, 0)),
                      _resident((None, None, dff, d), lambda bi, i: (layer, which, 0, 0))],
            out_specs=x_spec if out_spec is None else out_spec,
            out_shape=jax.ShapeDtypeStruct((b, l if out_spec is None else out_rows, d), F32),
            input_output_aliases={0: 0} if out_spec is None else {},
            compiler_params=_params(2),
            name="ffn_ctx" if stream == 0 else "ffn",
        )(xs, mod, norm_g, wg, wu, wd)

    tl = FFN_TM if (l - TM) % FFN_TM == 0 else TM
    lat_spec = pl.BlockSpec((pl.Element(1), pl.Element(tl), pl.Element(d)),
                            lambda bi, i: (bi, pl.multiple_of(TM + i * tl, TM), 0))
    if latents_out:
        return call(xs, lat_spec, (l - TM) // tl, 1,
                    pl.BlockSpec((None, tl, d), lambda bi, i: (bi, i, 0)), l - TM)
    xs = call(xs, pl.BlockSpec((None, TM, d), lambda bi, i: (bi, 0, 0)), 1, 0)
    return call(xs, lat_spec, (l - TM) // tl, 1)


def _norm_rope_t(tt, gain, rope):
    ms = jnp.sum(tt * tt, axis=0, keepdims=True) * (1.0 / HEAD_DIM)
    y = tt * lax.rsqrt(ms + EPS) * gain
    q = HEAD_DIM // 4
    cr, sr, cc, sc = (rope[j * q:(j + 1) * q, :] for j in range(4))
    y1, y2, y3, y4 = (y[j * q:(j + 1) * q, :] for j in range(4))
    return jnp.concatenate([y1 * cr - y2 * sr, y2 * cr + y1 * sr,
                            y3 * cc - y4 * sc, y4 * cc + y3 * sc], axis=0)


def _split8(y):
    hi = y.astype(F8).astype(F32)
    return hi, y - hi


def _inproj_kernel(x_ref, mod_ref, g_ref, w_ref, qkg_ref, rope_ref,
                   qa8_ref, qc8_ref, ka8_ref, kc8_ref, vta_ref, vtc_ref, b_ref, cu_ref):
    x = x_ref[...]
    h = _rms_mod(x, g_ref[...], mod_ref[3:4, :], mod_ref[4:5, :]).astype(BF16)
    z = jnp.dot(h, w_ref[...], preferred_element_type=F32)
    rope = rope_ref[...]

    def heads(lo, j):
        tt = z[:, lo:lo + LANES].T
        gain = qkg_ref[j * HEAD_DIM:(j + 1) * HEAD_DIM, :]
        return [_norm_rope_t(tt[0:HEAD_DIM, :], gain, rope),
                _norm_rope_t(tt[HEAD_DIM:, :], gain, rope)]

    zeros = jnp.zeros((HEAD_DIM, x.shape[0]), F32)

    for k8_ref, col, j in ((ka8_ref, 0, 1), (kc8_ref, 2 * KV_COLS, 3)):
        for g, y in enumerate(heads(col, j)):
            hi, lo = _split8(y)
            k8_ref[g] = jnp.concatenate([hi, hi, lo, zeros], axis=0).T.astype(F8)
    ones = jnp.ones((VT_ROWS - HEAD_DIM, x.shape[0]), BF16)
    for vt_ref, lo in ((vta_ref, KV_COLS), (vtc_ref, 3 * KV_COLS)):
        vt = z[:, lo:lo + KV_COLS].T.astype(BF16)
        for g in range(KV_HEADS):
            vt_ref[g * VT_ROWS:g * VT_ROWS + HEAD_DIM, :] = vt[g * HEAD_DIM:(g + 1) * HEAD_DIM, :]
            vt_ref[g * VT_ROWS + HEAD_DIM:(g + 1) * VT_ROWS, :] = ones
    q0 = 4 * KV_COLS
    for c in range(Q_COLS // LANES):
        for q8_ref, col, j in ((qa8_ref, q0, 0), (qc8_ref, q0 + Q_COLS, 2)):
            for e, y in enumerate(heads(col + c * LANES, j)):
                hi, lo = _split8(y)
                r0 = (2 * c + e) * F8_ROWS
                q8_ref[r0:r0 + F8_ROWS, :] = jnp.concatenate([hi, lo, hi, zeros],
                                                             axis=0).astype(F8)
    c0 = q0 + 2 * Q_COLS
    cw = b_ref.shape[-1]
    b_ref[...] = z[:, c0:c0 + cw]
    cu_ref[...] = z[:, c0 + cw:c0 + 2 * cw] * z[:, c0 + 2 * cw:c0 + 3 * cw]


def _inproj(xs, mod, norm_g, w_in, qkg, rope, *, layer):
    b, l, d = xs.shape
    ncol = w_in.shape[-1]
    cw = (ncol - 4 * KV_COLS - 2 * Q_COLS) // 3
    tok = lambda w: pl.BlockSpec((None, TM, w), lambda bi, i: (bi, i, 0))
    tr = lambda w: pl.BlockSpec((None, w, TM), lambda bi, i: (bi, 0, i))
    k8_spec = pl.BlockSpec((None, KV_HEADS, TM, F8_ROWS), lambda bi, i: (bi, 0, i, 0))
    return pl.pallas_call(
        _inproj_kernel,
        grid=(b, l // TM),
        in_specs=[tok(d),
                  pl.BlockSpec((None, None, N_MOD, d), lambda bi, i: (bi, jnp.minimum(i, 1), 0, 0)),
                  _resident((None, None, 1, d), lambda bi, i: (layer, 1, 0, 0)),
                  _resident((None, d, ncol), lambda bi, i: (layer, 0, 0)),
                  _resident((None, 4 * HEAD_DIM, TM), lambda bi, i: (layer, 0, 0)),
                  pl.BlockSpec((HEAD_DIM, TM), lambda bi, i: (0, i))],
        out_specs=[tr(Q_HEADS * F8_ROWS), tr(Q_HEADS * F8_ROWS), k8_spec, k8_spec,
                   tr(KV_HEADS * VT_ROWS), tr(KV_HEADS * VT_ROWS), tok(cw), tok(cw)],
        out_shape=[jax.ShapeDtypeStruct((b, Q_HEADS * F8_ROWS, l), F8),
                   jax.ShapeDtypeStruct((b, Q_HEADS * F8_ROWS, l), F8),
                   jax.ShapeDtypeStruct((b, KV_HEADS, l, F8_ROWS), F8),
                   jax.ShapeDtypeStruct((b, KV_HEADS, l, F8_ROWS), F8),
                   jax.ShapeDtypeStruct((b, KV_HEADS * VT_ROWS, l), BF16),
                   jax.ShapeDtypeStruct((b, KV_HEADS * VT_ROWS, l), BF16),
                   jax.ShapeDtypeStruct((b, l, cw), F32),
                   jax.ShapeDtypeStruct((b, l, cw), F32)],
        compiler_params=_params(2),
        name="inproj",
    )(xs, mod, norm_g, w_in, qkg, rope)


def _score_chunk(q8_ref, k8_ref, un_ref, start, ch, h):
    s = jnp.dot(k8_ref[h // GROUP, pl.ds(start, ch), :], q8_ref[h * F8_ROWS:(h + 1) * F8_ROWS, :],
                preferred_element_type=F32)
    return s * un_ref[...]


def _attend(q8_ref, k8_ref, un_ref, vt_ref, acc_ref, m_ref, al_ref, smax_ref, s_bufs, p_bufs, *,
            n_keys, tk):
    n_kt = n_keys // tk
    arows = lambda h: slice(h * VT_ROWS, (h + 1) * VT_ROWS)
    m_ref[...] = jnp.full(m_ref.shape, NEG_INF, F32)
    acc_ref[...] = jnp.zeros(acc_ref.shape, F32)

    ch = min(tk, CHUNK)
    n_ch = tk // ch

    def key_start(t, c):
        return t * tk + c * ch if isinstance(t, int) else pl.multiple_of(t * tk + c * ch, ch)

    def step(sc=None, sm=None, va=None):
        if sm is not None:
            m = m_ref[sm:sm + 1, :]
            m_new = jnp.maximum(m, smax_ref[sm:sm + 1, :])
            m_ref[sm:sm + 1, :] = m_new
            al_ref[sm:sm + 1, :] = jnp.exp2(m - m_new)
        smax = pv = None
        for c in range(n_ch):
            rows = slice(c * ch, (c + 1) * ch)
            if sc is not None:
                t, h = sc
                s = _score_chunk(q8_ref, k8_ref, un_ref, key_start(t, c), ch, h)
                s_bufs[h % 2][rows, :] = s
                cmax = jnp.max(s, axis=0, keepdims=True)
                smax = cmax if smax is None else jnp.maximum(smax, cmax)
            if sm is not None:
                p_bufs[sm % 2][rows, :] = jnp.exp2(s_bufs[sm % 2][rows, :] - m_new).astype(BF16)
            if va is not None:
                t, h = va
                g = h // GROUP
                vt = vt_ref[g * VT_ROWS:(g + 1) * VT_ROWS, pl.ds(key_start(t, c), ch)]
                d = jnp.dot(vt, p_bufs[h % 2][rows, :], preferred_element_type=F32)
                pv = d if pv is None else pv + d
        if sc is not None:
            smax_ref[sc[1]:sc[1] + 1, :] = smax
        if va is not None:
            h = va[1]
            acc_ref[arows(h), :] = al_ref[h:h + 1, :] * acc_ref[arows(h), :] + pv

    def tile_steps(t, last):
        for h in range(Q_HEADS):
            if h + 2 < Q_HEADS:
                sc = (t, h + 2)
            else:
                sc = None if last else (t + 1, h + 2 - Q_HEADS)
            if h + 1 < Q_HEADS:
                sm = h + 1
            else:
                sm = None if last else 0
            step(sc, sm, (t, h))

    step(sc=(0, 0))
    step(sc=(0, 1), sm=0)
    if n_kt > 1:
        def body(t, carry):
            tile_steps(t, False)
            return carry
        lax.fori_loop(0, n_kt - 1, body, 0)
    tile_steps(n_kt - 1, True)


def _attend_bounded(q8_ref, k8_ref, un_ref, vt_ref, acc_ref, p_bufs, *, n_keys, tk):
    n_kt = n_keys // tk
    n_buf = len(p_bufs)
    arows = lambda h: slice(h * VT_ROWS, (h + 1) * VT_ROWS)
    acc_ref[...] = jnp.zeros(acc_ref.shape, F32)
    ch = min(tk, CHUNK)
    n_ch = tk // ch

    def key_start(t, c):
        return t * tk + c * ch if isinstance(t, int) else pl.multiple_of(t * tk + c * ch, ch)

    def step(sc=None, va=None):
        if va is not None:
            t, h = va
            g = h // GROUP
            vt = vt_ref[g * VT_ROWS:(g + 1) * VT_ROWS, pl.ds(key_start(t, 0), tk)]
            acc_ref[arows(h), :] = acc_ref[arows(h), :] + jnp.dot(
                vt, p_bufs[h % n_buf][0:tk, :], preferred_element_type=F32)
        if sc is not None:
            t, h = sc
            for c in range(n_ch):
                s = _score_chunk(q8_ref, k8_ref, un_ref, key_start(t, c), ch, h)
                p_bufs[h % n_buf][c * ch:(c + 1) * ch, :] = jnp.exp2(s).astype(BF16)

    def tile_steps(t, last):
        for h in range(Q_HEADS):
            if h + 1 < Q_HEADS:
                sc = (t, h + 1)
            else:
                sc = None if last else (t + 1, 0)
            step(sc, (t, h))

    step(sc=(0, 0))
    if n_kt > 1:
        def body(t, carry):
            tile_steps(t, False)
            return carry
        lax.fori_loop(0, n_kt - 1, body, 0)
    tile_steps(n_kt - 1, True)


def _global_attn_kernel(q8_ref, k8_ref, un_ref, vt_ref, o_ref, acc_ref, *scratch, tk, bounded):
    i = pl.program_id(1)
    if bounded:
        attend = functools.partial(_attend_bounded, q8_ref, k8_ref, un_ref, vt_ref, acc_ref, scratch)
    else:
        m_ref, al_ref, smax_ref, s0_ref, s1_ref, p0_ref, p1_ref = scratch
        attend = functools.partial(_attend, q8_ref, k8_ref, un_ref, vt_ref, acc_ref, m_ref, al_ref,
                                   smax_ref, (s0_ref, s1_ref), (p0_ref, p1_ref))

    @pl.when(i == 0)
    def _():
        attend(n_keys=TM, tk=TM)

    @pl.when(i > 0)
    def _():
        attend(n_keys=k8_ref.shape[1], tk=tk)

    outs = []
    for h in range(Q_HEADS):
        a = acc_ref[h * VT_ROWS:(h + 1) * VT_ROWS, :]
        outs.append(a[0:HEAD_DIM, :] / a[HEAD_DIM:HEAD_DIM + 1, :])
    o_ref[...] = jnp.concatenate(outs, axis=0).T


def _key_tile(l):
    return next(t for t in (1280, 1024, 768, 512, 256) if l % t == 0)


def _global_attn(q8, k8, unscale, vt, score_bound):
    un = jnp.broadcast_to(unscale.astype(F32), (1, TM))
    return lax.cond(score_bound <= SCORE_BOUND,
                    functools.partial(_global_attn_call, bounded=True),
                    functools.partial(_global_attn_call, bounded=False), q8, k8, un, vt)


def _global_attn_call(q8, k8, un, vt, *, bounded):
    b, _, l = q8.shape
    tk = _key_tile(l)
    stat = pltpu.VMEM((Q_HEADS, TM), F32)
    p_buf = pltpu.VMEM((tk, TM), BF16)
    s_buf = pltpu.VMEM((tk, TM), F32)
    scratch = [p_buf] * P_BUFS if bounded else [stat, stat, stat, s_buf, s_buf, p_buf, p_buf]
    return pl.pallas_call(
        functools.partial(_global_attn_kernel, tk=tk, bounded=bounded),
        grid=(b, l // TM),
        in_specs=[pl.BlockSpec((None, Q_HEADS * F8_ROWS, TM), lambda bi, i: (bi, 0, i)),
                  pl.BlockSpec((None, KV_HEADS, l, F8_ROWS), lambda bi, i: (bi, 0, 0, 0)),
                  pl.BlockSpec((1, TM), lambda bi, i: (0, 0)),
                  pl.BlockSpec((None, KV_HEADS * VT_ROWS, l), lambda bi, i: (bi, 0, 0))],
        out_specs=pl.BlockSpec((None, TM, Q_COLS), lambda bi, i: (bi, i, 0)),
        out_shape=jax.ShapeDtypeStruct((b, l, Q_COLS), F32),
        scratch_shapes=[pltpu.VMEM((Q_HEADS * VT_ROWS, TM), F32)] + scratch,
        compiler_params=_params(2),
        name="global_attn_bounded" if bounded else "global_attn",
    )(q8, k8, un, vt)


def _window_geometry(i, tq, l_all):
    span = tq + 2 * WINDOW
    start = pl.multiple_of(jnp.clip(i * tq - WINDOW, 0, l_all - span), LANES)
    kpos = start - TM + lax.broadcasted_iota(jnp.int32, (span, 1), 0)
    qbase = jnp.where(i >= 1, i * tq - TM, -(1 << 20))
    qpos = qbase + lax.broadcasted_iota(jnp.int32, (1, tq), 1)
    ok = jnp.logical_and(kpos >= 0, jnp.abs(qpos - kpos) <= WINDOW)
    return start, jnp.where(ok, 0.0, NEG_INF).astype(F32)


def _window_attn_kernel(q8_ref, k8_ref, un_ref, vt_ref, sink_ref, o_ref, acc_ref):
    tq = q8_ref.shape[-1]
    span = tq + 2 * WINDOW
    start, bias = _window_geometry(pl.program_id(1), tq, k8_ref.shape[1])
    for h in range(Q_HEADS):
        g = h // GROUP
        s1 = _score_chunk(q8_ref, k8_ref, un_ref, 0, TM, h)
        s2 = _score_chunk(q8_ref, k8_ref, un_ref, start, span, h) + bias
        sink = sink_ref[h:h + 1, :]
        m = jnp.maximum(jnp.maximum(jnp.max(s1, axis=0, keepdims=True),
                                    jnp.max(s2, axis=0, keepdims=True)), sink)
        p1 = jnp.exp2(s1 - m)
        p2 = jnp.exp2(s2 - m)
        l = (jnp.sum(p1, axis=0, keepdims=True) + jnp.sum(p2, axis=0, keepdims=True)
             + jnp.exp2(sink - m))
        rows = slice(g * VT_ROWS, g * VT_ROWS + HEAD_DIM)
        acc = (jnp.dot(vt_ref[rows, 0:TM], p1.astype(BF16), preferred_element_type=F32)
               + jnp.dot(vt_ref[rows, pl.ds(start, span)], p2.astype(BF16),
                         preferred_element_type=F32))
        acc_ref[h * HEAD_DIM:(h + 1) * HEAD_DIM, :] = acc / l
    o_ref[...] = acc_ref[...].T


def _window_attn_bounded_kernel(q8_ref, k8_ref, un_ref, vt_ref, sink_ref, o_ref, bias_ref, *p_bufs):
    tq = q8_ref.shape[-1]
    start, bias = _window_geometry(pl.program_id(1), tq, k8_ref.shape[1])
    bias_ref[...] = bias
    n_buf = len(p_bufs)
    n_win = bias.shape[0] // CHUNK
    chunks = [(0, None)] + [(pl.multiple_of(start + c * CHUNK, LANES), c) for c in range(n_win)]
    accs = [None] * Q_HEADS

    def step(sc=None, va=None):
        if va is not None:
            g = va // GROUP
            pv = None
            for ci, (k0, _) in enumerate(chunks):
                vt = vt_ref[g * VT_ROWS:(g + 1) * VT_ROWS, pl.ds(k0, CHUNK)]
                d = jnp.dot(vt, p_bufs[va % n_buf][ci * CHUNK:(ci + 1) * CHUNK, :],
                            preferred_element_type=F32)
                pv = d if pv is None else pv + d
            accs[va] = pv
        if sc is not None:
            for ci, (k0, c) in enumerate(chunks):
                s = _score_chunk(q8_ref, k8_ref, un_ref, k0, CHUNK, sc)
                if c is not None:
                    s = s + bias_ref[c * CHUNK:(c + 1) * CHUNK, :]
                p_bufs[sc % n_buf][ci * CHUNK:(ci + 1) * CHUNK, :] = jnp.exp2(s).astype(BF16)

    for h in range(WIN_LAG):
        step(sc=h)
    for h in range(Q_HEADS):
        step(sc=h + WIN_LAG if h + WIN_LAG < Q_HEADS else None, va=h)
    outs = []
    for h in range(Q_HEADS):
        l = accs[h][HEAD_DIM:HEAD_DIM + 1, :] + jnp.exp2(sink_ref[h:h + 1, :])
        outs.append(accs[h][0:HEAD_DIM, :] / l)
    o_ref[...] = jnp.concatenate(outs, axis=0).T


def _window_attn(q8, k8, unscale, vt, sink, score_bound):
    sink2 = sink.astype(F32) * LOG2E
    ok = jnp.logical_and(score_bound <= SCORE_BOUND, jnp.max(jnp.abs(sink2)) <= SCORE_BOUND)
    sink_b = jnp.broadcast_to(sink2[:, None], (Q_HEADS, TM))
    un = jnp.broadcast_to(unscale.astype(F32), (1, TM))
    return lax.cond(ok, functools.partial(_window_attn_call, bounded=True),
                    functools.partial(_window_attn_call, bounded=False), q8, k8, un, vt, sink_b)


def _window_attn_call(q8, k8, un, vt, sink_b, *, bounded):
    b, _, l = q8.shape
    p_buf = pltpu.VMEM((2 * TM + 2 * WINDOW, TM), BF16)
    bounded_scratch = [pltpu.VMEM((TM + 2 * WINDOW, TM), F32)] + [p_buf] * (2 * WIN_LAG)
    return pl.pallas_call(
        _window_attn_bounded_kernel if bounded else _window_attn_kernel,
        grid=(b, l // TM),
        in_specs=[pl.BlockSpec((None, Q_HEADS * F8_ROWS, TM), lambda bi, i: (bi, 0, i)),
                  pl.BlockSpec((None, KV_HEADS, l, F8_ROWS), lambda bi, i: (bi, 0, 0, 0)),
                  pl.BlockSpec((1, TM), lambda bi, i: (0, 0)),
                  pl.BlockSpec((None, KV_HEADS * VT_ROWS, l), lambda bi, i: (bi, 0, 0)),
                  pl.BlockSpec((Q_HEADS, TM), lambda bi, i: (0, 0))],
        out_specs=pl.BlockSpec((None, TM, Q_COLS), lambda bi, i: (bi, i, 0)),
        out_shape=jax.ShapeDtypeStruct((b, l, Q_COLS), F32),
        scratch_shapes=bounded_scratch if bounded else [pltpu.VMEM((Q_COLS, TM), F32)],
        compiler_params=_params(2),
        name="window_attn_bounded" if bounded else "window_attn",
    )(q8, k8, un, vt, sink_b)


def _merge_kernel(x_ref, mod_ref, g_ref, oa_ref, oc_ref, b_ref, cu_ref, cup_ref, cun_ref, cw_ref,
                  wgate_ref, bgate_ref, wpa_ref, wpb_ref, wpc_ref, wo_ref, o_ref):
    i = pl.program_id(1)
    n_tiles = pl.num_programs(1)
    x = x_ref[...]
    tm, d = x.shape
    h = _rms_mod(x, g_ref[...], mod_ref[3:4, :], mod_ref[4:5, :]).astype(BF16)
    gates = _sigmoid(jnp.dot(h, wgate_ref[...], preferred_element_type=F32) + bgate_ref[...])

    cu = cu_ref[...]
    has_prev = (i >= 2).astype(F32)
    has_next = jnp.logical_and(i >= 1, i < n_tiles - 1).astype(F32)
    prev_row = cup_ref[7:8, :] * has_prev
    next_row = cun_ref[0:1, :] * has_next
    row = lax.broadcasted_iota(jnp.int32, (tm, 1), 0)
    cu_dn = jnp.where(row == 0, prev_row, pltpu.roll(cu, 1, 0))
    cu_up = jnp.where(row == tm - 1, next_row, pltpu.roll(cu, tm - 1, 0))
    y = cw_ref[0:1, :] * cu_dn + cw_ref[1:2, :] * cu + cw_ref[2:3, :] * cu_up
    o_b = (b_ref[...] * y).astype(BF16)

    pa = jnp.dot(oa_ref[...].astype(BF16), wpa_ref[...], preferred_element_type=F32)
    pb = jnp.dot(o_b, wpb_ref[...], preferred_element_type=F32)
    pc = jnp.dot(oc_ref[...].astype(BF16), wpc_ref[...], preferred_element_type=F32)
    mix = gates[:, 0:d] * pa + gates[:, d:2 * d] * pb + gates[:, 2 * d:3 * d] * pc
    out = jnp.dot(mix.astype(BF16), wo_ref[...], preferred_element_type=F32)
    o_ref[...] = x + mod_ref[5:6, :] * out


def _merge(xs, mod, norm_g, o_a, o_c, bb, cu, conv_w, w_gate, b_gate, w_pa, w_pb, w_pc, w_o,
           *, layer):
    b, l, d = xs.shape
    cw = cu.shape[-1]
    sub = 8
    tok = lambda w: pl.BlockSpec((None, TM, w), lambda bi, i: (bi, i, 0))
    lay = lambda *s: _resident((None,) + s, lambda bi, i: (layer,) + (0,) * len(s))
    return pl.pallas_call(
        _merge_kernel,
        grid=(b, l // TM),
        in_specs=[tok(d),
                  pl.BlockSpec((None, None, N_MOD, d), lambda bi, i: (bi, jnp.minimum(i, 1), 0, 0)),
                  _resident((None, None, 1, d), lambda bi, i: (layer, 1, 0, 0)),
                  tok(Q_COLS), tok(Q_COLS), tok(cw), tok(cw),
                  pl.BlockSpec((None, sub, cw),
                               lambda bi, i: (bi, jnp.maximum(i * (TM // sub) - 1, 0), 0)),
                  pl.BlockSpec((None, sub, cw),
                               lambda bi, i: (bi, jnp.minimum((i + 1) * (TM // sub), l // sub - 1), 0)),
                  lay(3, cw), lay(d, 3 * d), lay(1, 3 * d), lay(Q_COLS, d), lay(cw, d),
                  lay(Q_COLS, d), lay(d, d)],
        out_specs=tok(d),
        out_shape=jax.ShapeDtypeStruct(xs.shape, F32),
        input_output_aliases={0: 0},
        compiler_params=_params(2),
        name="merge",
    )(xs, mod, norm_g, o_a, o_c, bb, cu, cu, cu, conv_w, w_gate, b_gate, w_pa, w_pb, w_pc, w_o)


def _rope_tables(s, n_ctx):
    rows = s // GRID_W
    row = jnp.repeat(jnp.arange(rows), GRID_W).astype(F32)
    col = jnp.tile(jnp.arange(GRID_W), rows).astype(F32)
    half = HEAD_DIM // 2
    inv = ROPE_THETA ** (-jnp.arange(0, half, 2, dtype=F32) / half)
    ang_r = row[:, None] * inv
    ang_c = col[:, None] * inv
    tabs = jnp.concatenate([jnp.cos(ang_r), jnp.sin(ang_r), jnp.cos(ang_c), jnp.sin(ang_c)],
                           axis=-1)
    one, zero = jnp.ones((n_ctx, HEAD_DIM // 4), F32), jnp.zeros((n_ctx, HEAD_DIM // 4), F32)
    ident = jnp.concatenate([one, zero, one, zero], axis=-1)
    return jnp.concatenate([ident, tabs], axis=0).T


def kernel(x, c, ctx, c_ctx, w_ada, b_ada, norm_g, ffn_w_gate, ffn_w_up, ffn_w_down, w_in, qk_g,
           sink_a, conv_w, w_pa, w_pb, w_pc, w_gate, b_gate, w_o):
    bsz, s, d = x.shape
    n_ctx = ctx.shape[1]
    depth = w_ada.shape[0]
    assert n_ctx == TM and s % TM == 0 and s % GRID_W == 0 and bsz < 8

    cvec = jnp.zeros((8, d), F32).at[:bsz].set(c).at[bsz].set(c_ctx)
    mods = _ada(cvec, w_ada, b_ada).reshape(depth, 8, N_MOD, d)
    lat = mods[:, :bsz]
    con = jnp.broadcast_to(mods[:, bsz:bsz + 1], lat.shape)
    mod_all = jnp.stack([con, lat], axis=2)

    rope = _rope_tables(s, n_ctx)
    gmax = jnp.max(jnp.abs(qk_g.astype(F32)), axis=-1)
    vmax = gmax * (HEAD_DIM ** 0.5) * jnp.array([Q_SCALE, 1.0, Q_SCALE, 1.0], F32)
    pow2 = jnp.exp2(jnp.clip(jnp.floor(jnp.log2(0.98 * F8_MAX / vmax)), -20.0, 20.0))
    unscale = 1.0 / (pow2[:, 0::2] * pow2[:, 1::2])
    fold = (pow2 * jnp.array([Q_SCALE, 1.0, Q_SCALE, 1.0], F32))[:, :, None]
    qkg = jnp.broadcast_to((qk_g.astype(F32) * fold).reshape(depth, 4 * HEAD_DIM, 1),
                           (depth, 4 * HEAD_DIM, TM))
    ng = norm_g.astype(F32)[:, :, None, :]
    wg, wu, wd = (w.astype(BF16) for w in (ffn_w_gate, ffn_w_up, ffn_w_down))
    w_in_b, w_gate_b, w_pa_b, w_pb_b, w_pc_b, w_o_b = (
        w.astype(BF16) for w in (w_in, w_gate, w_pa, w_pb, w_pc, w_o))
    b_gate3 = b_gate.astype(F32)[:, None, :]

    xs = jnp.concatenate([ctx, x], axis=1)
    for i in range(depth):
        mod = mod_all[i]
        xs = _ffn(xs, mod, ng, wg, wu, wd, layer=i, which=0, j0=0)
        qa8, qc8, ka8, kc8, vta, vtc, bb, cu = _inproj(xs, mod, ng, w_in_b, qkg, rope, layer=i)
        o_a = _window_attn(qa8, ka8, unscale[i, 0], vta, sink_a[i], 1.02 * vmax[i, 0] * vmax[i, 1])
        o_c = _global_attn(qc8, kc8, unscale[i, 1], vtc, 1.02 * vmax[i, 2] * vmax[i, 3])
        xs = _merge(xs, mod, ng, o_a, o_c, bb, cu, conv_w.astype(F32), w_gate_b, b_gate3,
                    w_pa_b, w_pb_b, w_pc_b, w_o_b, layer=i)
        xs = _ffn(xs, mod, ng, wg, wu, wd, layer=i, which=1, j0=6, latents_out=i + 1 == depth)
    return xs
```

```python
import functools

import jax
import jax.numpy as jnp
from jax import lax
from jax.experimental import pallas as pl
from jax.experimental.pallas import tpu as pltpu

GRID_W = 64
HEAD_DIM = 64
Q_HEADS = 8
KV_HEADS = 2
GROUP = Q_HEADS // KV_HEADS
WINDOW = 128
N_MOD = 9
ROPE_THETA = 10000.0
EPS = 1e-6
NEG_INF = -1e30
SCALE = HEAD_DIM ** -0.5
LOG2E = 1.4426950408889634
Q_SCALE = SCALE * LOG2E
SCORE_BOUND = 48.0
Q_COLS = Q_HEADS * HEAD_DIM
KV_COLS = KV_HEADS * HEAD_DIM
VT_ROWS = HEAD_DIM + 16

LANES = 128
TM = 256
FFN_TM = 512
CHUNK = 256
P_BUFS = 2
UNROLL = 4
WIN_LAG = 2
VMEM_LIMIT = 56 * 1024 * 1024

F32 = jnp.float32
BF16 = jnp.bfloat16
F8 = jnp.float8_e4m3fn
F8_MAX = 448.0
F8_ROWS = 4 * HEAD_DIM


def _params(n_axes):
    return pltpu.CompilerParams(dimension_semantics=("arbitrary",) * n_axes,
                                vmem_limit_bytes=VMEM_LIMIT)


def _resident(shape, index_map):
    return pl.BlockSpec(shape, index_map, pipeline_mode=pl.Buffered(1))


def _sigmoid(v):
    return 1.0 / (1.0 + jnp.exp(-v))


def _rms_mod(x, g, shift, scale):
    y = x * lax.rsqrt(jnp.mean(x * x, axis=-1, keepdims=True) + EPS)
    return (y * g) * (1.0 + scale) + shift


def _ada_kernel(c_ref, w_ref, b_ref, o_ref):
    c = c_ref[...]
    s = (c * _sigmoid(c)).astype(BF16)
    o_ref[...] = jnp.dot(s, w_ref[...].astype(BF16), preferred_element_type=F32) + b_ref[...]


def _ada(cvec, w_ada, b_ada):
    depth, d, n = w_ada.shape
    tn = 1024
    return pl.pallas_call(
        _ada_kernel,
        grid=(depth, n // tn),
        in_specs=[pl.BlockSpec((8, d), lambda l, j: (0, 0)),
                  pl.BlockSpec((None, d, tn), lambda l, j: (l, 0, j)),
                  pl.BlockSpec((None, 1, tn), lambda l, j: (l, 0, j))],
        out_specs=pl.BlockSpec((None, 8, tn), lambda l, j: (l, 0, j)),
        out_shape=jax.ShapeDtypeStruct((depth, 8, n), F32),
        compiler_params=_params(2),
        name="ada",
    )(cvec, w_ada, b_ada.reshape(depth, 1, n))


def _ffn_kernel(x_ref, mod_ref, g_ref, wg_ref, wu_ref, wd_ref, o_ref, *, j0):
    xb = x_ref[...].reshape(x_ref.shape[-2:])
    outs = []
    for r in range(0, xb.shape[0], TM):
        x = xb[r:r + TM, :]
        h = _rms_mod(x, g_ref[...], mod_ref[j0:j0 + 1, :], mod_ref[j0 + 1:j0 + 2, :]).astype(BF16)
        a = jnp.dot(h, wg_ref[...], preferred_element_type=F32)
        u = jnp.dot(h, wu_ref[...], preferred_element_type=F32)
        act = (a * _sigmoid(a) * u).astype(BF16)
        y = jnp.dot(act, wd_ref[...], preferred_element_type=F32)
        outs.append(x + (0.5 * mod_ref[j0 + 2:j0 + 3, :]) * y)
    o_ref[...] = jnp.concatenate(outs, axis=0).reshape(o_ref.shape)


def _ffn(xs, mod, norm_g, wg, wu, wd, *, layer, which, j0, latents_out=False):
    b, l, d = xs.shape
    dff = wg.shape[-1]

    def call(xs, x_spec, n_steps, stream, out_spec=None, out_rows=None):
        return pl.pallas_call(
            functools.partial(_ffn_kernel, j0=j0),
            grid=(b, n_steps),
            in_specs=[x_spec,
                      pl.BlockSpec((None, None, N_MOD, d), lambda bi, i: (bi, stream, 0, 0)),
                      _resident((None, None, 1, d), lambda bi, i: (layer, 2 * which, 0, 0)),
                      _resident((None, None, d, dff), lambda bi, i: (layer, which, 0, 0)),
                      _resident((None, None, d, dff), lambda bi, i: (layer, which, 0, 0)),
                      _resident((None, None, dff, d), lambda bi, i: (layer, which, 0, 0))],
            out_specs=x_spec if out_spec is None else out_spec,
            out_shape=jax.ShapeDtypeStruct((b, l if out_spec is None else out_rows, d), F32),
            input_output_aliases={0: 0} if out_spec is None else {},
            compiler_params=_params(2),
            name="ffn_ctx" if stream == 0 else "ffn",
        )(xs, mod, norm_g, wg, wu, wd)

    tl = FFN_TM if (l - TM) % FFN_TM == 0 else TM
    lat_spec = pl.BlockSpec((pl.Element(1), pl.Element(tl), pl.Element(d)),
                            lambda bi, i: (bi, pl.multiple_of(TM + i * tl, TM), 0))
    if latents_out:
        return call(xs, lat_spec, (l - TM) // tl, 1,
                    pl.BlockSpec((None, tl, d), lambda bi, i: (bi, i, 0)), l - TM)
    xs = call(xs, pl.BlockSpec((None, TM, d), lambda bi, i: (bi, 0, 0)), 1, 0)
    return call(xs, lat_spec, (l - TM) // tl, 1)


def _norm_rope_t(tt, gain, rope):
    ms = jnp.sum(tt * tt, axis=0, keepdims=True) * (1.0 / HEAD_DIM)
    y = tt * lax.rsqrt(ms + EPS) * gain
    q = HEAD_DIM // 4
    cr, sr, cc, sc = (rope[j * q:(j + 1) * q, :] for j in range(4))
    y1, y2, y3, y4 = (y[j * q:(j + 1) * q, :] for j in range(4))
    return jnp.concatenate([y1 * cr - y2 * sr, y2 * cr + y1 * sr,
                            y3 * cc - y4 * sc, y4 * cc + y3 * sc], axis=0)


def _split8(y):
    hi = y.astype(F8).astype(F32)
    return hi, y - hi


def _inproj_kernel(x_ref, mod_ref, g_ref, w_ref, qkg_ref, rope_ref,
                   qa8_ref, qc8_ref, ka8_ref, kc8_ref, vta_ref, vtc_ref, b_ref, cu_ref):
    x = x_ref[...]
    h = _rms_mod(x, g_ref[...], mod_ref[3:4, :], mod_ref[4:5, :]).astype(BF16)
    z = jnp.dot(h, w_ref[...], preferred_element_type=F32)
    rope = rope_ref[...]

    def heads(lo, j):
        tt = z[:, lo:lo + LANES].T
        gain = qkg_ref[j * HEAD_DIM:(j + 1) * HEAD_DIM, :]
        return [_norm_rope_t(tt[0:HEAD_DIM, :], gain, rope),
                _norm_rope_t(tt[HEAD_DIM:, :], gain, rope)]

    zeros = jnp.zeros((HEAD_DIM, x.shape[0]), F32)

    for k8_ref, col, j in ((ka8_ref, 0, 1), (kc8_ref, 2 * KV_COLS, 3)):
        for g, y in enumerate(heads(col, j)):
            hi, lo = _split8(y)
            k8_ref[g] = jnp.concatenate([hi, hi, lo, zeros], axis=0).T.astype(F8)
    ones = jnp.ones((VT_ROWS - HEAD_DIM, x.shape[0]), BF16)
    for vt_ref, lo in ((vta_ref, KV_COLS), (vtc_ref, 3 * KV_COLS)):
        vt = z[:, lo:lo + KV_COLS].T.astype(BF16)
        for g in range(KV_HEADS):
            vt_ref[g * VT_ROWS:g * VT_ROWS + HEAD_DIM, :] = vt[g * HEAD_DIM:(g + 1) * HEAD_DIM, :]
            vt_ref[g * VT_ROWS + HEAD_DIM:(g + 1) * VT_ROWS, :] = ones
    q0 = 4 * KV_COLS
    for c in range(Q_COLS // LANES):
        for q8_ref, col, j in ((qa8_ref, q0, 0), (qc8_ref, q0 + Q_COLS, 2)):
            for e, y in enumerate(heads(col + c * LANES, j)):
                hi, lo = _split8(y)
                r0 = (2 * c + e) * F8_ROWS
                q8_ref[r0:r0 + F8_ROWS, :] = jnp.concatenate([hi, lo, hi, zeros],
                                                             axis=0).astype(F8)
    c0 = q0 + 2 * Q_COLS
    cw = b_ref.shape[-1]
    b_ref[...] = z[:, c0:c0 + cw]
    cu_ref[...] = z[:, c0 + cw:c0 + 2 * cw] * z[:, c0 + 2 * cw:c0 + 3 * cw]


def _inproj(xs, mod, norm_g, w_in, qkg, rope, *, layer):
    b, l, d = xs.shape
    ncol = w_in.shape[-1]
    cw = (ncol - 4 * KV_COLS - 2 * Q_COLS) // 3
    tok = lambda w: pl.BlockSpec((None, TM, w), lambda bi, i: (bi, i, 0))
    tr = lambda w: pl.BlockSpec((None, w, TM), lambda bi, i: (bi, 0, i))
    k8_spec = pl.BlockSpec((None, KV_HEADS, TM, F8_ROWS), lambda bi, i: (bi, 0, i, 0))
    return pl.pallas_call(
        _inproj_kernel,
        grid=(b, l // TM),
        in_specs=[tok(d),
                  pl.BlockSpec((None, None, N_MOD, d), lambda bi, i: (bi, jnp.minimum(i, 1), 0, 0)),
                  _resident((None, None, 1, d), lambda bi, i: (layer, 1, 0, 0)),
                  _resident((None, d, ncol), lambda bi, i: (layer, 0, 0)),
                  _resident((None, 4 * HEAD_DIM, TM), lambda bi, i: (layer, 0, 0)),
                  pl.BlockSpec((HEAD_DIM, TM), lambda bi, i: (0, i))],
        out_specs=[tr(Q_HEADS * F8_ROWS), tr(Q_HEADS * F8_ROWS), k8_spec, k8_spec,
                   tr(KV_HEADS * VT_ROWS), tr(KV_HEADS * VT_ROWS), tok(cw), tok(cw)],
        out_shape=[jax.ShapeDtypeStruct((b, Q_HEADS * F8_ROWS, l), F8),
                   jax.ShapeDtypeStruct((b, Q_HEADS * F8_ROWS, l), F8),
                   jax.ShapeDtypeStruct((b, KV_HEADS, l, F8_ROWS), F8),
                   jax.ShapeDtypeStruct((b, KV_HEADS, l, F8_ROWS), F8),
                   jax.ShapeDtypeStruct((b, KV_HEADS * VT_ROWS, l), BF16),
                   jax.ShapeDtypeStruct((b, KV_HEADS * VT_ROWS, l), BF16),
                   jax.ShapeDtypeStruct((b, l, cw), F32),
                   jax.ShapeDtypeStruct((b, l, cw), F32)],
        compiler_params=_params(2),
        name="inproj",
    )(xs, mod, norm_g, w_in, qkg, rope)


def _score_chunk(q8_ref, k8_ref, un_ref, start, ch, h):
    s = jnp.dot(k8_ref[h // GROUP, pl.ds(start, ch), :], q8_ref[h * F8_ROWS:(h + 1) * F8_ROWS, :],
                preferred_element_type=F32)
    return s * un_ref[...]


def _attend(q8_ref, k8_ref, un_ref, vt_ref, acc_ref, m_ref, al_ref, smax_ref, s_bufs, p_bufs, *,
            n_keys, tk):
    n_kt = n_keys // tk
    arows = lambda h: slice(h * VT_ROWS, (h + 1) * VT_ROWS)
    m_ref[...] = jnp.full(m_ref.shape, NEG_INF, F32)
    acc_ref[...] = jnp.zeros(acc_ref.shape, F32)

    ch = min(tk, CHUNK)
    n_ch = tk // ch

    def key_start(t, c):
        return t * tk + c * ch if isinstance(t, int) else pl.multiple_of(t * tk + c * ch, ch)

    def step(sc=None, sm=None, va=None):
        if sm is not None:
            m = m_ref[sm:sm + 1, :]
            m_new = jnp.maximum(m, smax_ref[sm:sm + 1, :])
            m_ref[sm:sm + 1, :] = m_new
            al_ref[sm:sm + 1, :] = jnp.exp2(m - m_new)
        smax = pv = None
        for c in range(n_ch):
            rows = slice(c * ch, (c + 1) * ch)
            if sc is not None:
                t, h = sc
                s = _score_chunk(q8_ref, k8_ref, un_ref, key_start(t, c), ch, h)
                s_bufs[h % 2][rows, :] = s
                cmax = jnp.max(s, axis=0, keepdims=True)
                smax = cmax if smax is None else jnp.maximum(smax, cmax)
            if sm is not None:
                p_bufs[sm % 2][rows, :] = jnp.exp2(s_bufs[sm % 2][rows, :] - m_new).astype(BF16)
            if va is not None:
                t, h = va
                g = h // GROUP
                vt = vt_ref[g * VT_ROWS:(g + 1) * VT_ROWS, pl.ds(key_start(t, c), ch)]
                d = jnp.dot(vt, p_bufs[h % 2][rows, :], preferred_element_type=F32)
                pv = d if pv is None else pv + d
        if sc is not None:
            smax_ref[sc[1]:sc[1] + 1, :] = smax
        if va is not None:
            h = va[1]
            acc_ref[arows(h), :] = al_ref[h:h + 1, :] * acc_ref[arows(h), :] + pv

    def tile_steps(t, last):
        for h in range(Q_HEADS):
            if h + 2 < Q_HEADS:
                sc = (t, h + 2)
            else:
                sc = None if last else (t + 1, h + 2 - Q_HEADS)
            if h + 1 < Q_HEADS:
                sm = h + 1
            else:
                sm = None if last else 0
            step(sc, sm, (t, h))

    step(sc=(0, 0))
    step(sc=(0, 1), sm=0)
    if n_kt > 1:
        def body(t, carry):
            tile_steps(t, False)
            return carry
        lax.fori_loop(0, n_kt - 1, body, 0)
    tile_steps(n_kt - 1, True)


def _attend_bounded(q8_ref, k8_ref, un_ref, vt_ref, acc_ref, p_bufs, *, n_keys, tk):
    n_kt = n_keys // tk
    n_buf = len(p_bufs)
    arows = lambda h: slice(h * VT_ROWS, (h + 1) * VT_ROWS)
    acc_ref[...] = jnp.zeros(acc_ref.shape, F32)
    ch = min(tk, CHUNK)
    n_ch = tk // ch

    def key_start(t, c):
        return t * tk + c * ch if isinstance(t, int) else pl.multiple_of(t * tk + c * ch, ch)

    def step(sc=None, va=None):
        if va is not None:
            t, h = va
            g = h // GROUP
            vt = vt_ref[g * VT_ROWS:(g + 1) * VT_ROWS, pl.ds(key_start(t, 0), tk)]
            acc_ref[arows(h), :] = acc_ref[arows(h), :] + jnp.dot(
                vt, p_bufs[h % n_buf][0:tk, :], preferred_element_type=F32)
        if sc is not None:
            t, h = sc
            for c in range(n_ch):
                s = _score_chunk(q8_ref, k8_ref, un_ref, key_start(t, c), ch, h)
                p_bufs[h % n_buf][c * ch:(c + 1) * ch, :] = jnp.exp2(s).astype(BF16)

    def tile_steps(t, last):
        for h in range(Q_HEADS):
            if h + 1 < Q_HEADS:
                sc = (t, h + 1)
            else:
                sc = None if last else (t + 1, 0)
            step(sc, (t, h))

    step(sc=(0, 0))
    n_trips, n_rest = divmod(n_kt - 1, UNROLL)
    if n_trips > 0:
        def body(t, carry):
            for u in range(UNROLL):
                tile_steps(t * UNROLL + u, False)
            return carry
        lax.fori_loop(0, n_trips, body, 0)
    for u in range(n_rest):
        tile_steps(n_trips * UNROLL + u, False)
    tile_steps(n_kt - 1, True)


def _global_attn_kernel(q8_ref, k8_ref, un_ref, vt_ref, o_ref, acc_ref, *scratch, tk, bounded):
    i = pl.program_id(1)
    if bounded:
        attend = functools.partial(_attend_bounded, q8_ref, k8_ref, un_ref, vt_ref, acc_ref, scratch)
    else:
        m_ref, al_ref, smax_ref, s0_ref, s1_ref, p0_ref, p1_ref = scratch
        attend = functools.partial(_attend, q8_ref, k8_ref, un_ref, vt_ref, acc_ref, m_ref, al_ref,
                                   smax_ref, (s0_ref, s1_ref), (p0_ref, p1_ref))

    @pl.when(i == 0)
    def _():
        attend(n_keys=TM, tk=TM)

    @pl.when(i > 0)
    def _():
        attend(n_keys=k8_ref.shape[1], tk=tk)

    outs = []
    for h in range(Q_HEADS):
        a = acc_ref[h * VT_ROWS:(h + 1) * VT_ROWS, :]
        outs.append(a[0:HEAD_DIM, :] / a[HEAD_DIM:HEAD_DIM + 1, :])
    o_ref[...] = jnp.concatenate(outs, axis=0).T


def _key_tile(l):
    return next(t for t in (1280, 1024, 768, 512, 256) if l % t == 0)


def _global_attn(q8, k8, unscale, vt, score_bound):
    un = jnp.broadcast_to(unscale.astype(F32), (1, TM))
    return lax.cond(score_bound <= SCORE_BOUND,
                    functools.partial(_global_attn_call, bounded=True),
                    functools.partial(_global_attn_call, bounded=False), q8, k8, un, vt)


def _global_attn_call(q8, k8, un, vt, *, bounded):
    b, _, l = q8.shape
    tk = _key_tile(l)
    stat = pltpu.VMEM((Q_HEADS, TM), F32)
    p_buf = pltpu.VMEM((tk, TM), BF16)
    s_buf = pltpu.VMEM((tk, TM), F32)
    scratch = [p_buf] * P_BUFS if bounded else [stat, stat, stat, s_buf, s_buf, p_buf, p_buf]
    return pl.pallas_call(
        functools.partial(_global_attn_kernel, tk=tk, bounded=bounded),
        grid=(b, l // TM),
        in_specs=[pl.BlockSpec((None, Q_HEADS * F8_ROWS, TM), lambda bi, i: (bi, 0, i)),
                  pl.BlockSpec((None, KV_HEADS, l, F8_ROWS), lambda bi, i: (bi, 0, 0, 0)),
                  pl.BlockSpec((1, TM), lambda bi, i: (0, 0)),
                  pl.BlockSpec((None, KV_HEADS * VT_ROWS, l), lambda bi, i: (bi, 0, 0))],
        out_specs=pl.BlockSpec((None, TM, Q_COLS), lambda bi, i: (bi, i, 0)),
        out_shape=jax.ShapeDtypeStruct((b, l, Q_COLS), F32),
        scratch_shapes=[pltpu.VMEM((Q_HEADS * VT_ROWS, TM), F32)] + scratch,
        compiler_params=_params(2),
        name="global_attn_bounded" if bounded else "global_attn",
    )(q8, k8, un, vt)


def _window_geometry(i, tq, l_all):
    span = tq + 2 * WINDOW
    start = pl.multiple_of(jnp.clip(i * tq - WINDOW, 0, l_all - span), LANES)
    kpos = start - TM + lax.broadcasted_iota(jnp.int32, (span, 1), 0)
    qbase = jnp.where(i >= 1, i * tq - TM, -(1 << 20))
    qpos = qbase + lax.broadcasted_iota(jnp.int32, (1, tq), 1)
    ok = jnp.logical_and(kpos >= 0, jnp.abs(qpos - kpos) <= WINDOW)
    return start, jnp.where(ok, 0.0, NEG_INF).astype(F32)


def _window_attn_kernel(q8_ref, k8_ref, un_ref, vt_ref, sink_ref, o_ref, acc_ref):
    tq = q8_ref.shape[-1]
    span = tq + 2 * WINDOW
    start, bias = _window_geometry(pl.program_id(1), tq, k8_ref.shape[1])
    for h in range(Q_HEADS):
        g = h // GROUP
        s1 = _score_chunk(q8_ref, k8_ref, un_ref, 0, TM, h)
        s2 = _score_chunk(q8_ref, k8_ref, un_ref, start, span, h) + bias
        sink = sink_ref[h:h + 1, :]
        m = jnp.maximum(jnp.maximum(jnp.max(s1, axis=0, keepdims=True),
                                    jnp.max(s2, axis=0, keepdims=True)), sink)
        p1 = jnp.exp2(s1 - m)
        p2 = jnp.exp2(s2 - m)
        l = (jnp.sum(p1, axis=0, keepdims=True) + jnp.sum(p2, axis=0, keepdims=True)
             + jnp.exp2(sink - m))
        rows = slice(g * VT_ROWS, g * VT_ROWS + HEAD_DIM)
        acc = (jnp.dot(vt_ref[rows, 0:TM], p1.astype(BF16), preferred_element_type=F32)
               + jnp.dot(vt_ref[rows, pl.ds(start, span)], p2.astype(BF16),
                         preferred_element_type=F32))
        acc_ref[h * HEAD_DIM:(h + 1) * HEAD_DIM, :] = acc / l
    o_ref[...] = acc_ref[...].T


def _window_attn_bounded_kernel(q8_ref, k8_ref, un_ref, vt_ref, sink_ref, o_ref, bias_ref, *p_bufs):
    tq = q8_ref.shape[-1]
    start, bias = _window_geometry(pl.program_id(1), tq, k8_ref.shape[1])
    bias_ref[...] = bias
    n_buf = len(p_bufs)
    n_win = bias.shape[0] // CHUNK
    chunks = [(0, None)] + [(pl.multiple_of(start + c * CHUNK, LANES), c) for c in range(n_win)]
    accs = [None] * Q_HEADS

    def step(sc=None, va=None):
        if va is not None:
            g = va // GROUP
            pv = None
            for ci, (k0, _) in enumerate(chunks):
                vt = vt_ref[g * VT_ROWS:(g + 1) * VT_ROWS, pl.ds(k0, CHUNK)]
                d = jnp.dot(vt, p_bufs[va % n_buf][ci * CHUNK:(ci + 1) * CHUNK, :],
                            preferred_element_type=F32)
                pv = d if pv is None else pv + d
            accs[va] = pv
        if sc is not None:
            for ci, (k0, c) in enumerate(chunks):
                s = _score_chunk(q8_ref, k8_ref, un_ref, k0, CHUNK, sc)
                if c is not None:
                    s = s + bias_ref[c * CHUNK:(c + 1) * CHUNK, :]
                p_bufs[sc % n_buf][ci * CHUNK:(ci + 1) * CHUNK, :] = jnp.exp2(s).astype(BF16)

    for h in range(WIN_LAG):
        step(sc=h)
    for h in range(Q_HEADS):
        step(sc=h + WIN_LAG if h + WIN_LAG < Q_HEADS else None, va=h)
    outs = []
    for h in range(Q_HEADS):
        l = accs[h][HEAD_DIM:HEAD_DIM + 1, :] + jnp.exp2(sink_ref[h:h + 1, :])
        outs.append(accs[h][0:HEAD_DIM, :] / l)
    o_ref[...] = jnp.concatenate(outs, axis=0).T


def _window_attn(q8, k8, unscale, vt, sink, score_bound):
    sink2 = sink.astype(F32) * LOG2E
    ok = jnp.logical_and(score_bound <= SCORE_BOUND, jnp.max(jnp.abs(sink2)) <= SCORE_BOUND)
    sink_b = jnp.broadcast_to(sink2[:, None], (Q_HEADS, TM))
    un = jnp.broadcast_to(unscale.astype(F32), (1, TM))
    return lax.cond(ok, functools.partial(_window_attn_call, bounded=True),
                    functools.partial(_window_attn_call, bounded=False), q8, k8, un, vt, sink_b)


def _window_attn_call(q8, k8, un, vt, sink_b, *, bounded):
    b, _, l = q8.shape
    p_buf = pltpu.VMEM((2 * TM + 2 * WINDOW, TM), BF16)
    bounded_scratch = [pltpu.VMEM((TM + 2 * WINDOW, TM), F32)] + [p_buf] * (2 * WIN_LAG)
    return pl.pallas_call(
        _window_attn_bounded_kernel if bounded else _window_attn_kernel,
        grid=(b, l // TM),
        in_specs=[pl.BlockSpec((None, Q_HEADS * F8_ROWS, TM), lambda bi, i: (bi, 0, i)),
                  pl.BlockSpec((None, KV_HEADS, l, F8_ROWS), lambda bi, i: (bi, 0, 0, 0)),
                  pl.BlockSpec((1, TM), lambda bi, i: (0, 0)),
                  pl.BlockSpec((None, KV_HEADS * VT_ROWS, l), lambda bi, i: (bi, 0, 0)),
                  pl.BlockSpec((Q_HEADS, TM), lambda bi, i: (0, 0))],
        out_specs=pl.BlockSpec((None, TM, Q_COLS), lambda bi, i: (bi, i, 0)),
        out_shape=jax.ShapeDtypeStruct((b, l, Q_COLS), F32),
        scratch_shapes=bounded_scratch if bounded else [pltpu.VMEM((Q_COLS, TM), F32)],
        compiler_params=_params(2),
        name="window_attn_bounded" if bounded else "window_attn",
    )(q8, k8, un, vt, sink_b)


def _merge_kernel(x_ref, mod_ref, g_ref, oa_ref, oc_ref, b_ref, cu_ref, cup_ref, cun_ref, cw_ref,
                  wgate_ref, bgate_ref, wpa_ref, wpb_ref, wpc_ref, wo_ref, o_ref):
    i = pl.program_id(1)
    n_tiles = pl.num_programs(1)
    x = x_ref[...]
    tm, d = x.shape
    h = _rms_mod(x, g_ref[...], mod_ref[3:4, :], mod_ref[4:5, :]).astype(BF16)
    gates = _sigmoid(jnp.dot(h, wgate_ref[...], preferred_element_type=F32) + bgate_ref[...])

    cu = cu_ref[...]
    has_prev = (i >= 2).astype(F32)
    has_next = jnp.logical_and(i >= 1, i < n_tiles - 1).astype(F32)
    prev_row = cup_ref[7:8, :] * has_prev
    next_row = cun_ref[0:1, :] * has_next
    row = lax.broadcasted_iota(jnp.int32, (tm, 1), 0)
    cu_dn = jnp.where(row == 0, prev_row, pltpu.roll(cu, 1, 0))
    cu_up = jnp.where(row == tm - 1, next_row, pltpu.roll(cu, tm - 1, 0))
    y = cw_ref[0:1, :] * cu_dn + cw_ref[1:2, :] * cu + cw_ref[2:3, :] * cu_up
    o_b = (b_ref[...] * y).astype(BF16)

    pa = jnp.dot(oa_ref[...].astype(BF16), wpa_ref[...], preferred_element_type=F32)
    pb = jnp.dot(o_b, wpb_ref[...], preferred_element_type=F32)
    pc = jnp.dot(oc_ref[...].astype(BF16), wpc_ref[...], preferred_element_type=F32)
    mix = gates[:, 0:d] * pa + gates[:, d:2 * d] * pb + gates[:, 2 * d:3 * d] * pc
    out = jnp.dot(mix.astype(BF16), wo_ref[...], preferred_element_type=F32)
    o_ref[...] = x + mod_ref[5:6, :] * out


def _merge(xs, mod, norm_g, o_a, o_c, bb, cu, conv_w, w_gate, b_gate, w_pa, w_pb, w_pc, w_o,
           *, layer):
    b, l, d = xs.shape
    cw = cu.shape[-1]
    sub = 8
    tok = lambda w: pl.BlockSpec((None, TM, w), lambda bi, i: (bi, i, 0))
    lay = lambda *s: _resident((None,) + s, lambda bi, i: (layer,) + (0,) * len(s))
    return pl.pallas_call(
        _merge_kernel,
        grid=(b, l // TM),
        in_specs=[tok(d),
                  pl.BlockSpec((None, None, N_MOD, d), lambda bi, i: (bi, jnp.minimum(i, 1), 0, 0)),
                  _resident((None, None, 1, d), lambda bi, i: (layer, 1, 0, 0)),
                  tok(Q_COLS), tok(Q_COLS), tok(cw), tok(cw),
                  pl.BlockSpec((None, sub, cw),
                               lambda bi, i: (bi, jnp.maximum(i * (TM // sub) - 1, 0), 0)),
                  pl.BlockSpec((None, sub, cw),
                               lambda bi, i: (bi, jnp.minimum((i + 1) * (TM // sub), l // sub - 1), 0)),
                  lay(3, cw), lay(d, 3 * d), lay(1, 3 * d), lay(Q_COLS, d), lay(cw, d),
                  lay(Q_COLS, d), lay(d, d)],
        out_specs=tok(d),
        out_shape=jax.ShapeDtypeStruct(xs.shape, F32),
        input_output_aliases={0: 0},
        compiler_params=_params(2),
        name="merge",
    )(xs, mod, norm_g, o_a, o_c, bb, cu, cu, cu, conv_w, w_gate, b_gate, w_pa, w_pb, w_pc, w_o)


def _rope_tables(s, n_ctx):
    rows = s // GRID_W
    row = jnp.repeat(jnp.arange(rows), GRID_W).astype(F32)
    col = jnp.tile(jnp.arange(GRID_W), rows).astype(F32)
    half = HEAD_DIM // 2
    inv = ROPE_THETA ** (-jnp.arange(0, half, 2, dtype=F32) / half)
    ang_r = row[:, None] * inv
    ang_c = col[:, None] * inv
    tabs = jnp.concatenate([jnp.cos(ang_r), jnp.sin(ang_r), jnp.cos(ang_c), jnp.sin(ang_c)],
                           axis=-1)
    one, zero = jnp.ones((n_ctx, HEAD_DIM // 4), F32), jnp.zeros((n_ctx, HEAD_DIM // 4), F32)
    ident = jnp.concatenate([one, zero, one, zero], axis=-1)
    return jnp.concatenate([ident, tabs], axis=0).T


def kernel(x, c, ctx, c_ctx, w_ada, b_ada, norm_g, ffn_w_gate, ffn_w_up, ffn_w_down, w_in, qk_g,
           sink_a, conv_w, w_pa, w_pb, w_pc, w_gate, b_gate, w_o):
    bsz, s, d = x.shape
    n_ctx = ctx.shape[1]
    depth = w_ada.shape[0]
    assert n_ctx == TM and s % TM == 0 and s % GRID_W == 0 and bsz < 8

    cvec = jnp.zeros((8, d), F32).at[:bsz].set(c).at[bsz].set(c_ctx)
    mods = _ada(cvec, w_ada, b_ada).reshape(depth, 8, N_MOD, d)
    lat = mods[:, :bsz]
    con = jnp.broadcast_to(mods[:, bsz:bsz + 1], lat.shape)
    mod_all = jnp.stack([con, lat], axis=2)

    rope = _rope_tables(s, n_ctx)
    gmax = jnp.max(jnp.abs(qk_g.astype(F32)), axis=-1)
    vmax = gmax * (HEAD_DIM ** 0.5) * jnp.array([Q_SCALE, 1.0, Q_SCALE, 1.0], F32)
    pow2 = jnp.exp2(jnp.clip(jnp.floor(jnp.log2(0.98 * F8_MAX / vmax)), -20.0, 20.0))
    unscale = 1.0 / (pow2[:, 0::2] * pow2[:, 1::2])
    fold = (pow2 * jnp.array([Q_SCALE, 1.0, Q_SCALE, 1.0], F32))[:, :, None]
    qkg = jnp.broadcast_to((qk_g.astype(F32) * fold).reshape(depth, 4 * HEAD_DIM, 1),
                           (depth, 4 * HEAD_DIM, TM))
    ng = norm_g.astype(F32)[:, :, None, :]
    wg, wu, wd = (w.astype(BF16) for w in (ffn_w_gate, ffn_w_up, ffn_w_down))
    w_in_b, w_gate_b, w_pa_b, w_pb_b, w_pc_b, w_o_b = (
        w.astype(BF16) for w in (w_in, w_gate, w_pa, w_pb, w_pc, w_o))
    b_gate3 = b_gate.astype(F32)[:, None, :]

    xs = jnp.concatenate([ctx, x], axis=1)
    for i in range(depth):
        mod = mod_all[i]
        xs = _ffn(xs, mod, ng, wg, wu, wd, layer=i, which=0, j0=0)
        qa8, qc8, ka8, kc8, vta, vtc, bb, cu = _inproj(xs, mod, ng, w_in_b, qkg, rope, layer=i)
        o_a = _window_attn(qa8, ka8, unscale[i, 0], vta, sink_a[i], 1.02 * vmax[i, 0] * vmax[i, 1])
        o_c = _global_attn(qc8, kc8, unscale[i, 1], vtc, 1.02 * vmax[i, 2] * vmax[i, 3])
        xs = _merge(xs, mod, ng, o_a, o_c, bb, cu, conv_w.astype(F32), w_gate_b, b_gate3,
                    w_pa_b, w_pb_b, w_pc_b, w_o_b, layer=i)
        xs = _ffn(xs, mod, ng, wg, wu, wd, layer=i, which=1, j0=6, latents_out=i + 1 == depth)
    return xs
```

```python
import functools

import jax
import jax.numpy as jnp
from jax import lax
from jax.experimental import pallas as pl
from jax.experimental.pallas import tpu as pltpu

GRID_W = 64
HEAD_DIM = 64
Q_HEADS = 8
KV_HEADS = 2
GROUP = Q_HEADS // KV_HEADS
WINDOW = 128
N_MOD = 9
ROPE_THETA = 10000.0
EPS = 1e-6
NEG_INF = -1e30
SCALE = HEAD_DIM ** -0.5
LOG2E = 1.4426950408889634
Q_SCALE = SCALE * LOG2E
SCORE_BOUND = 48.0
Q_COLS = Q_HEADS * HEAD_DIM
KV_COLS = KV_HEADS * HEAD_DIM
VT_ROWS = HEAD_DIM + 16

LANES = 128
TM = 256
FFN_TM = 512
CHUNK = 256
P_BUFS = 2
UNROLL = 4
WIN_LAG = 2
WIN_SUB = 2
VMEM_LIMIT = 56 * 1024 * 1024

F32 = jnp.float32
BF16 = jnp.bfloat16
F8 = jnp.float8_e4m3fn
F8_MAX = 448.0
F8_ROWS = 4 * HEAD_DIM


def _params(n_axes):
    return pltpu.CompilerParams(dimension_semantics=("arbitrary",) * n_axes,
                                vmem_limit_bytes=VMEM_LIMIT)


def _resident(shape, index_map):
    return pl.BlockSpec(shape, index_map, pipeline_mode=pl.Buffered(1))


def _sigmoid(v):
    return 1.0 / (1.0 + jnp.exp(-v))


def _rms_mod(x, g, shift, scale):
    y = x * lax.rsqrt(jnp.mean(x * x, axis=-1, keepdims=True) + EPS)
    return (y * g) * (1.0 + scale) + shift


def _ada_kernel(c_ref, w_ref, b_ref, o_ref):
    c = c_ref[...]
    s = (c * _sigmoid(c)).astype(BF16)
    o_ref[...] = jnp.dot(s, w_ref[...].astype(BF16), preferred_element_type=F32) + b_ref[...]


def _ada(cvec, w_ada, b_ada):
    depth, d, n = w_ada.shape
    tn = 1024
    return pl.pallas_call(
        _ada_kernel,
        grid=(depth, n // tn),
        in_specs=[pl.BlockSpec((8, d), lambda l, j: (0, 0)),
                  pl.BlockSpec((None, d, tn), lambda l, j: (l, 0, j)),
                  pl.BlockSpec((None, 1, tn), lambda l, j: (l, 0, j))],
        out_specs=pl.BlockSpec((None, 8, tn), lambda l, j: (l, 0, j)),
        out_shape=jax.ShapeDtypeStruct((depth, 8, n), F32),
        compiler_params=_params(2),
        name="ada",
    )(cvec, w_ada, b_ada.reshape(depth, 1, n))


def _ffn_kernel(x_ref, mod_ref, g_ref, wg_ref, wu_ref, wd_ref, o_ref, *, j0):
    xb = x_ref[...].reshape(x_ref.shape[-2:])
    outs = []
    for r in range(0, xb.shape[0], TM):
        x = xb[r:r + TM, :]
        h = _rms_mod(x, g_ref[...], mod_ref[j0:j0 + 1, :], mod_ref[j0 + 1:j0 + 2, :]).astype(BF16)
        a = jnp.dot(h, wg_ref[...], preferred_element_type=F32)
        u = jnp.dot(h, wu_ref[...], preferred_element_type=F32)
        act = (a * _sigmoid(a) * u).astype(BF16)
        y = jnp.dot(act, wd_ref[...], preferred_element_type=F32)
        outs.append(x + (0.5 * mod_ref[j0 + 2:j0 + 3, :]) * y)
    o_ref[...] = jnp.concatenate(outs, axis=0).reshape(o_ref.shape)


def _ffn(xs, mod, norm_g, wg, wu, wd, *, layer, which, j0, latents_out=False):
    b, l, d = xs.shape
    dff = wg.shape[-1]

    def call(xs, x_spec, n_steps, stream, out_spec=None, out_rows=None):
        return pl.pallas_call(
            functools.partial(_ffn_kernel, j0=j0),
            grid=(b, n_steps),
            in_specs=[x_spec,
                      pl.BlockSpec((None, None, N_MOD, d), lambda bi, i: (bi, stream, 0, 0)),
                      _resident((None, None, 1, d), lambda bi, i: (layer, 2 * which, 0, 0)),
                      _resident((None, None, d, dff), lambda bi, i: (layer, which, 0, 0)),
                      _resident((None, None, d, dff), lambda bi, i: (layer, which, 0, 0)),
                      _resident((None, None, dff, d), lambda bi, i: (layer, which, 0, 0))],
            out_specs=x_spec if out_spec is None else out_spec,
            out_shape=jax.ShapeDtypeStruct((b, l if out_spec is None else out_rows, d), F32),
            input_output_aliases={0: 0} if out_spec is None else {},
            compiler_params=_params(2),
            name="ffn_ctx" if stream == 0 else "ffn",
        )(xs, mod, norm_g, wg, wu, wd)

    tl = FFN_TM if (l - TM) % FFN_TM == 0 else TM
    lat_spec = pl.BlockSpec((pl.Element(1), pl.Element(tl), pl.Element(d)),
                            lambda bi, i: (bi, pl.multiple_of(TM + i * tl, TM), 0))
    if latents_out:
        return call(xs, lat_spec, (l - TM) // tl, 1,
                    pl.BlockSpec((None, tl, d), lambda bi, i: (bi, i, 0)), l - TM)
    xs = call(xs, pl.BlockSpec((None, TM, d), lambda bi, i: (bi, 0, 0)), 1, 0)
    return call(xs, lat_spec, (l - TM) // tl, 1)


def _norm_rope_t(tt, gain, rope):
    ms = jnp.sum(tt * tt, axis=0, keepdims=True) * (1.0 / HEAD_DIM)
    y = tt * lax.rsqrt(ms + EPS) * gain
    q = HEAD_DIM // 4
    cr, sr, cc, sc = (rope[j * q:(j + 1) * q, :] for j in range(4))
    y1, y2, y3, y4 = (y[j * q:(j + 1) * q, :] for j in range(4))
    return jnp.concatenate([y1 * cr - y2 * sr, y2 * cr + y1 * sr,
                            y3 * cc - y4 * sc, y4 * cc + y3 * sc], axis=0)


def _split8(y):
    hi = y.astype(F8).astype(F32)
    return hi, y - hi


def _inproj_kernel(x_ref, mod_ref, g_ref, w_ref, qkg_ref, rope_ref,
                   qa8_ref, qc8_ref, ka8_ref, kc8_ref, vta_ref, vtc_ref, b_ref, cu_ref):
    x = x_ref[...]
    h = _rms_mod(x, g_ref[...], mod_ref[3:4, :], mod_ref[4:5, :]).astype(BF16)
    z = jnp.dot(h, w_ref[...], preferred_element_type=F32)
    rope = rope_ref[...]

    def heads(lo, j):
        tt = z[:, lo:lo + LANES].T
        gain = qkg_ref[j * HEAD_DIM:(j + 1) * HEAD_DIM, :]
        return [_norm_rope_t(tt[0:HEAD_DIM, :], gain, rope),
                _norm_rope_t(tt[HEAD_DIM:, :], gain, rope)]

    zeros = jnp.zeros((HEAD_DIM, x.shape[0]), F32)

    for k8_ref, col, j in ((ka8_ref, 0, 1), (kc8_ref, 2 * KV_COLS, 3)):
        for g, y in enumerate(heads(col, j)):
            hi, lo = _split8(y)
            k8_ref[g] = jnp.concatenate([hi, hi, lo, zeros], axis=0).T.astype(F8)
    ones = jnp.ones((VT_ROWS - HEAD_DIM, x.shape[0]), BF16)
    for vt_ref, lo in ((vta_ref, KV_COLS), (vtc_ref, 3 * KV_COLS)):
        vt = z[:, lo:lo + KV_COLS].T.astype(BF16)
        for g in range(KV_HEADS):
            vt_ref[g * VT_ROWS:g * VT_ROWS + HEAD_DIM, :] = vt[g * HEAD_DIM:(g + 1) * HEAD_DIM, :]
            vt_ref[g * VT_ROWS + HEAD_DIM:(g + 1) * VT_ROWS, :] = ones
    q0 = 4 * KV_COLS
    for c in range(Q_COLS // LANES):
        for q8_ref, col, j in ((qa8_ref, q0, 0), (qc8_ref, q0 + Q_COLS, 2)):
            for e, y in enumerate(heads(col + c * LANES, j)):
                hi, lo = _split8(y)
                r0 = (2 * c + e) * F8_ROWS
                q8_ref[r0:r0 + F8_ROWS, :] = jnp.concatenate([hi, lo, hi, zeros],
                                                             axis=0).astype(F8)
    c0 = q0 + 2 * Q_COLS
    cw = b_ref.shape[-1]
    b_ref[...] = z[:, c0:c0 + cw]
    cu_ref[...] = z[:, c0 + cw:c0 + 2 * cw] * z[:, c0 + 2 * cw:c0 + 3 * cw]


def _inproj(xs, mod, norm_g, w_in, qkg, rope, *, layer):
    b, l, d = xs.shape
    ncol = w_in.shape[-1]
    cw = (ncol - 4 * KV_COLS - 2 * Q_COLS) // 3
    tok = lambda w: pl.BlockSpec((None, TM, w), lambda bi, i: (bi, i, 0))
    tr = lambda w: pl.BlockSpec((None, w, TM), lambda bi, i: (bi, 0, i))
    k8_spec = pl.BlockSpec((None, KV_HEADS, TM, F8_ROWS), lambda bi, i: (bi, 0, i, 0))
    return pl.pallas_call(
        _inproj_kernel,
        grid=(b, l // TM),
        in_specs=[tok(d),
                  pl.BlockSpec((None, None, N_MOD, d), lambda bi, i: (bi, jnp.minimum(i, 1), 0, 0)),
                  _resident((None, None, 1, d), lambda bi, i: (layer, 1, 0, 0)),
                  _resident((None, d, ncol), lambda bi, i: (layer, 0, 0)),
                  _resident((None, 4 * HEAD_DIM, TM), lambda bi, i: (layer, 0, 0)),
                  pl.BlockSpec((HEAD_DIM, TM), lambda bi, i: (0, i))],
        out_specs=[tr(Q_HEADS * F8_ROWS), tr(Q_HEADS * F8_ROWS), k8_spec, k8_spec,
                   tr(KV_HEADS * VT_ROWS), tr(KV_HEADS * VT_ROWS), tok(cw), tok(cw)],
        out_shape=[jax.ShapeDtypeStruct((b, Q_HEADS * F8_ROWS, l), F8),
                   jax.ShapeDtypeStruct((b, Q_HEADS * F8_ROWS, l), F8),
                   jax.ShapeDtypeStruct((b, KV_HEADS, l, F8_ROWS), F8),
                   jax.ShapeDtypeStruct((b, KV_HEADS, l, F8_ROWS), F8),
                   jax.ShapeDtypeStruct((b, KV_HEADS * VT_ROWS, l), BF16),
                   jax.ShapeDtypeStruct((b, KV_HEADS * VT_ROWS, l), BF16),
                   jax.ShapeDtypeStruct((b, l, cw), F32),
                   jax.ShapeDtypeStruct((b, l, cw), F32)],
        compiler_params=_params(2),
        name="inproj",
    )(xs, mod, norm_g, w_in, qkg, rope)


def _score_chunk(q8_ref, k8_ref, un_ref, start, ch, h):
    s = jnp.dot(k8_ref[h // GROUP, pl.ds(start, ch), :], q8_ref[h * F8_ROWS:(h + 1) * F8_ROWS, :],
                preferred_element_type=F32)
    return s * un_ref[...]


def _attend(q8_ref, k8_ref, un_ref, vt_ref, acc_ref, m_ref, al_ref, smax_ref, s_bufs, p_bufs, *,
            n_keys, tk):
    n_kt = n_keys // tk
    arows = lambda h: slice(h * VT_ROWS, (h + 1) * VT_ROWS)
    m_ref[...] = jnp.full(m_ref.shape, NEG_INF, F32)
    acc_ref[...] = jnp.zeros(acc_ref.shape, F32)

    ch = min(tk, CHUNK)
    n_ch = tk // ch

    def key_start(t, c):
        return t * tk + c * ch if isinstance(t, int) else pl.multiple_of(t * tk + c * ch, ch)

    def step(sc=None, sm=None, va=None):
        if sm is not None:
            m = m_ref[sm:sm + 1, :]
            m_new = jnp.maximum(m, smax_ref[sm:sm + 1, :])
            m_ref[sm:sm + 1, :] = m_new
            al_ref[sm:sm + 1, :] = jnp.exp2(m - m_new)
        smax = pv = None
        for c in range(n_ch):
            rows = slice(c * ch, (c + 1) * ch)
            if sc is not None:
                t, h = sc
                s = _score_chunk(q8_ref, k8_ref, un_ref, key_start(t, c), ch, h)
                s_bufs[h % 2][rows, :] = s
                cmax = jnp.max(s, axis=0, keepdims=True)
                smax = cmax if smax is None else jnp.maximum(smax, cmax)
            if sm is not None:
                p_bufs[sm % 2][rows, :] = jnp.exp2(s_bufs[sm % 2][rows, :] - m_new).astype(BF16)
            if va is not None:
                t, h = va
                g = h // GROUP
                vt = vt_ref[g * VT_ROWS:(g + 1) * VT_ROWS, pl.ds(key_start(t, c), ch)]
                d = jnp.dot(vt, p_bufs[h % 2][rows, :], preferred_element_type=F32)
                pv = d if pv is None else pv + d
        if sc is not None:
            smax_ref[sc[1]:sc[1] + 1, :] = smax
        if va is not None:
            h = va[1]
            acc_ref[arows(h), :] = al_ref[h:h + 1, :] * acc_ref[arows(h), :] + pv

    def tile_steps(t, last):
        for h in range(Q_HEADS):
            if h + 2 < Q_HEADS:
                sc = (t, h + 2)
            else:
                sc = None if last else (t + 1, h + 2 - Q_HEADS)
            if h + 1 < Q_HEADS:
                sm = h + 1
            else:
                sm = None if last else 0
            step(sc, sm, (t, h))

    step(sc=(0, 0))
    step(sc=(0, 1), sm=0)
    if n_kt > 1:
        def body(t, carry):
            tile_steps(t, False)
            return carry
        lax.fori_loop(0, n_kt - 1, body, 0)
    tile_steps(n_kt - 1, True)


def _attend_bounded(q8_ref, k8_ref, un_ref, vt_ref, acc_ref, p_bufs, *, n_keys, tk):
    n_kt = n_keys // tk
    n_buf = len(p_bufs)
    arows = lambda h: slice(h * VT_ROWS, (h + 1) * VT_ROWS)
    acc_ref[...] = jnp.zeros(acc_ref.shape, F32)
    ch = min(tk, CHUNK)
    n_ch = tk // ch

    def key_start(t, c):
        return t * tk + c * ch if isinstance(t, int) else pl.multiple_of(t * tk + c * ch, ch)

    def step(sc=None, va=None):
        if va is not None:
            t, h = va
            g = h // GROUP
            vt = vt_ref[g * VT_ROWS:(g + 1) * VT_ROWS, pl.ds(key_start(t, 0), tk)]
            acc_ref[arows(h), :] = acc_ref[arows(h), :] + jnp.dot(
                vt, p_bufs[h % n_buf][0:tk, :], preferred_element_type=F32)
        if sc is not None:
            t, h = sc
            for c in range(n_ch):
                s = _score_chunk(q8_ref, k8_ref, un_ref, key_start(t, c), ch, h)
                p_bufs[h % n_buf][c * ch:(c + 1) * ch, :] = jnp.exp2(s).astype(BF16)

    def tile_steps(t, last):
        for h in range(Q_HEADS):
            if h + 1 < Q_HEADS:
                sc = (t, h + 1)
            else:
                sc = None if last else (t + 1, 0)
            step(sc, (t, h))

    step(sc=(0, 0))
    n_trips, n_rest = divmod(n_kt - 1, UNROLL)
    if n_trips > 0:
        def body(t, carry):
            for u in range(UNROLL):
                tile_steps(t * UNROLL + u, False)
            return carry
        lax.fori_loop(0, n_trips, body, 0)
    for u in range(n_rest):
        tile_steps(n_trips * UNROLL + u, False)
    tile_steps(n_kt - 1, True)


def _global_attn_kernel(q8_ref, k8_ref, un_ref, vt_ref, o_ref, acc_ref, *scratch, tk, bounded):
    i = pl.program_id(1)
    if bounded:
        attend = functools.partial(_attend_bounded, q8_ref, k8_ref, un_ref, vt_ref, acc_ref, scratch)
    else:
        m_ref, al_ref, smax_ref, s0_ref, s1_ref, p0_ref, p1_ref = scratch
        attend = functools.partial(_attend, q8_ref, k8_ref, un_ref, vt_ref, acc_ref, m_ref, al_ref,
                                   smax_ref, (s0_ref, s1_ref), (p0_ref, p1_ref))

    @pl.when(i == 0)
    def _():
        attend(n_keys=TM, tk=TM)

    @pl.when(i > 0)
    def _():
        attend(n_keys=k8_ref.shape[1], tk=tk)

    outs = []
    for h in range(Q_HEADS):
        a = acc_ref[h * VT_ROWS:(h + 1) * VT_ROWS, :]
        outs.append(a[0:HEAD_DIM, :] / a[HEAD_DIM:HEAD_DIM + 1, :])
    o_ref[...] = jnp.concatenate(outs, axis=0).T


def _key_tile(l):
    return next(t for t in (1280, 1024, 768, 512, 256) if l % t == 0)


def _global_attn(q8, k8, unscale, vt, score_bound):
    un = jnp.broadcast_to(unscale.astype(F32), (1, TM))
    return lax.cond(score_bound <= SCORE_BOUND,
                    functools.partial(_global_attn_call, bounded=True),
                    functools.partial(_global_attn_call, bounded=False), q8, k8, un, vt)


def _global_attn_call(q8, k8, un, vt, *, bounded):
    b, _, l = q8.shape
    tk = _key_tile(l)
    stat = pltpu.VMEM((Q_HEADS, TM), F32)
    p_buf = pltpu.VMEM((tk, TM), BF16)
    s_buf = pltpu.VMEM((tk, TM), F32)
    scratch = [p_buf] * P_BUFS if bounded else [stat, stat, stat, s_buf, s_buf, p_buf, p_buf]
    return pl.pallas_call(
        functools.partial(_global_attn_kernel, tk=tk, bounded=bounded),
        grid=(b, l // TM),
        in_specs=[pl.BlockSpec((None, Q_HEADS * F8_ROWS, TM), lambda bi, i: (bi, 0, i)),
                  pl.BlockSpec((None, KV_HEADS, l, F8_ROWS), lambda bi, i: (bi, 0, 0, 0)),
                  pl.BlockSpec((1, TM), lambda bi, i: (0, 0)),
                  pl.BlockSpec((None, KV_HEADS * VT_ROWS, l), lambda bi, i: (bi, 0, 0))],
        out_specs=pl.BlockSpec((None, TM, Q_COLS), lambda bi, i: (bi, i, 0)),
        out_shape=jax.ShapeDtypeStruct((b, l, Q_COLS), F32),
        scratch_shapes=[pltpu.VMEM((Q_HEADS * VT_ROWS, TM), F32)] + scratch,
        compiler_params=_params(2),
        name="global_attn_bounded" if bounded else "global_attn",
    )(q8, k8, un, vt)


def _window_geometry(i, tq, l_all):
    span = tq + 2 * WINDOW
    start = pl.multiple_of(jnp.clip(i * tq - WINDOW, 0, l_all - span), LANES)
    kpos = start - TM + lax.broadcasted_iota(jnp.int32, (span, 1), 0)
    qbase = jnp.where(i >= 1, i * tq - TM, -(1 << 20))
    qpos = qbase + lax.broadcasted_iota(jnp.int32, (1, tq), 1)
    ok = jnp.logical_and(kpos >= 0, jnp.abs(qpos - kpos) <= WINDOW)
    return start, jnp.where(ok, 0.0, NEG_INF).astype(F32)


def _window_attn_kernel(q8_ref, k8_ref, un_ref, vt_ref, sink_ref, o_ref, acc_ref):
    tq = q8_ref.shape[-1]
    span = tq + 2 * WINDOW
    start, bias = _window_geometry(pl.program_id(1), tq, k8_ref.shape[1])
    for h in range(Q_HEADS):
        g = h // GROUP
        s1 = _score_chunk(q8_ref, k8_ref, un_ref, 0, TM, h)
        s2 = _score_chunk(q8_ref, k8_ref, un_ref, start, span, h) + bias
        sink = sink_ref[h:h + 1, :]
        m = jnp.maximum(jnp.maximum(jnp.max(s1, axis=0, keepdims=True),
                                    jnp.max(s2, axis=0, keepdims=True)), sink)
        p1 = jnp.exp2(s1 - m)
        p2 = jnp.exp2(s2 - m)
        l = (jnp.sum(p1, axis=0, keepdims=True) + jnp.sum(p2, axis=0, keepdims=True)
             + jnp.exp2(sink - m))
        rows = slice(g * VT_ROWS, g * VT_ROWS + HEAD_DIM)
        acc = (jnp.dot(vt_ref[rows, 0:TM], p1.astype(BF16), preferred_element_type=F32)
               + jnp.dot(vt_ref[rows, pl.ds(start, span)], p2.astype(BF16),
                         preferred_element_type=F32))
        acc_ref[h * HEAD_DIM:(h + 1) * HEAD_DIM, :] = acc / l
    o_ref[...] = acc_ref[...].T


def _window_attn_bounded_kernel(*refs):
    q_refs = refs[:WIN_SUB]
    k8_ref, un_ref, vt_ref, sink_ref, o_ref, bias_ref, acc_ref = refs[WIN_SUB:WIN_SUB + 7]
    p_bufs = refs[WIN_SUB + 7:]
    tq = q_refs[0].shape[-1]
    l_all = k8_ref.shape[1]
    n_buf = len(p_bufs)
    starts = []
    for e in range(WIN_SUB):
        t = jnp.minimum(WIN_SUB * pl.program_id(1) + e, l_all // tq - 1)
        start, bias = _window_geometry(t, tq, l_all)
        bias_ref[e] = bias
        starts.append(start)
    n_win = bias_ref.shape[1] // CHUNK
    items = [(e, h) for h in range(Q_HEADS) for e in range(WIN_SUB)]

    def chunks(e):
        return [(0, None)] + [(pl.multiple_of(starts[e] + c * CHUNK, LANES), c)
                              for c in range(n_win)]

    def step(sc=None, va=None):
        if va is not None:
            e, h = items[va]
            g = h // GROUP
            pv = None
            for ci, (k0, _) in enumerate(chunks(e)):
                vt = vt_ref[g * VT_ROWS:(g + 1) * VT_ROWS, pl.ds(k0, CHUNK)]
                d = jnp.dot(vt, p_bufs[va % n_buf][ci * CHUNK:(ci + 1) * CHUNK, :],
                            preferred_element_type=F32)
                pv = d if pv is None else pv + d
            acc_ref[va * VT_ROWS:(va + 1) * VT_ROWS, :] = pv
        if sc is not None:
            e, h = items[sc]
            for ci, (k0, c) in enumerate(chunks(e)):
                s = _score_chunk(q_refs[e], k8_ref, un_ref, k0, CHUNK, h)
                if c is not None:
                    s = s + bias_ref[e, c * CHUNK:(c + 1) * CHUNK, :]
                p_bufs[sc % n_buf][ci * CHUNK:(ci + 1) * CHUNK, :] = jnp.exp2(s).astype(BF16)

    for n in range(WIN_LAG):
        step(sc=n)
    for n in range(len(items)):
        step(sc=n + WIN_LAG if n + WIN_LAG < len(items) else None, va=n)
    for e in range(WIN_SUB):
        outs = []
        for h in range(Q_HEADS):
            n = items.index((e, h))
            a = acc_ref[n * VT_ROWS:(n + 1) * VT_ROWS, :]
            l = a[HEAD_DIM:HEAD_DIM + 1, :] + jnp.exp2(sink_ref[h:h + 1, :])
            outs.append(a[0:HEAD_DIM, :] / l)
        o_ref[e * tq:(e + 1) * tq, :] = jnp.concatenate(outs, axis=0).T


def _window_attn(q8, k8, unscale, vt, sink, score_bound):
    sink2 = sink.astype(F32) * LOG2E
    ok = jnp.logical_and(score_bound <= SCORE_BOUND, jnp.max(jnp.abs(sink2)) <= SCORE_BOUND)
    sink_b = jnp.broadcast_to(sink2[:, None], (Q_HEADS, TM))
    un = jnp.broadcast_to(unscale.astype(F32), (1, TM))
    return lax.cond(ok, functools.partial(_window_attn_call, bounded=True),
                    functools.partial(_window_attn_call, bounded=False), q8, k8, un, vt, sink_b)


def _window_attn_call(q8, k8, un, vt, sink_b, *, bounded):
    b, _, l = q8.shape
    n_tiles = l // TM
    span = TM + 2 * WINDOW
    n_sub = WIN_SUB if bounded else 1
    q_spec = lambda e: pl.BlockSpec(
        (None, Q_HEADS * F8_ROWS, TM),
        lambda bi, i: (bi, 0, jnp.minimum(n_sub * i + e, n_tiles - 1)))
    if bounded:
        scratch = [pltpu.VMEM((WIN_SUB, span, TM), F32),
                   pltpu.VMEM((WIN_SUB * Q_HEADS * VT_ROWS, TM), F32)]
        scratch += [pltpu.VMEM((TM + span, TM), BF16)] * (2 * WIN_LAG)
    else:
        scratch = [pltpu.VMEM((Q_COLS, TM), F32)]
    return pl.pallas_call(
        _window_attn_bounded_kernel if bounded else _window_attn_kernel,
        grid=(b, pl.cdiv(n_tiles, n_sub)),
        in_specs=[q_spec(e) for e in range(n_sub)] + [
            pl.BlockSpec((None, KV_HEADS, l, F8_ROWS), lambda bi, i: (bi, 0, 0, 0)),
            pl.BlockSpec((1, TM), lambda bi, i: (0, 0)),
            pl.BlockSpec((None, KV_HEADS * VT_ROWS, l), lambda bi, i: (bi, 0, 0)),
            pl.BlockSpec((Q_HEADS, TM), lambda bi, i: (0, 0))],
        out_specs=pl.BlockSpec((None, n_sub * TM, Q_COLS), lambda bi, i: (bi, i, 0)),
        out_shape=jax.ShapeDtypeStruct((b, l, Q_COLS), F32),
        scratch_shapes=scratch,
        compiler_params=_params(2),
        name="window_attn_bounded" if bounded else "window_attn",
    )(*([q8] * n_sub), k8, un, vt, sink_b)


def _merge_kernel(x_ref, mod_ref, g_ref, oa_ref, oc_ref, b_ref, cu_ref, cup_ref, cun_ref, cw_ref,
                  wgate_ref, bgate_ref, wpa_ref, wpb_ref, wpc_ref, wo_ref, o_ref):
    i = pl.program_id(1)
    n_tiles = pl.num_programs(1)
    x = x_ref[...]
    tm, d = x.shape
    h = _rms_mod(x, g_ref[...], mod_ref[3:4, :], mod_ref[4:5, :]).astype(BF16)
    gates = _sigmoid(jnp.dot(h, wgate_ref[...], preferred_element_type=F32) + bgate_ref[...])

    cu = cu_ref[...]
    has_prev = (i >= 2).astype(F32)
    has_next = jnp.logical_and(i >= 1, i < n_tiles - 1).astype(F32)
    prev_row = cup_ref[7:8, :] * has_prev
    next_row = cun_ref[0:1, :] * has_next
    row = lax.broadcasted_iota(jnp.int32, (tm, 1), 0)
    cu_dn = jnp.where(row == 0, prev_row, pltpu.roll(cu, 1, 0))
    cu_up = jnp.where(row == tm - 1, next_row, pltpu.roll(cu, tm - 1, 0))
    y = cw_ref[0:1, :] * cu_dn + cw_ref[1:2, :] * cu + cw_ref[2:3, :] * cu_up
    o_b = (b_ref[...] * y).astype(BF16)

    pa = jnp.dot(oa_ref[...].astype(BF16), wpa_ref[...], preferred_element_type=F32)
    pb = jnp.dot(o_b, wpb_ref[...], preferred_element_type=F32)
    pc = jnp.dot(oc_ref[...].astype(BF16), wpc_ref[...], preferred_element_type=F32)
    mix = gates[:, 0:d] * pa + gates[:, d:2 * d] * pb + gates[:, 2 * d:3 * d] * pc
    out = jnp.dot(mix.astype(BF16), wo_ref[...], preferred_element_type=F32)
    o_ref[...] = x + mod_ref[5:6, :] * out


def _merge(xs, mod, norm_g, o_a, o_c, bb, cu, conv_w, w_gate, b_gate, w_pa, w_pb, w_pc, w_o,
           *, layer):
    b, l, d = xs.shape
    cw = cu.shape[-1]
    sub = 8
    tok = lambda w: pl.BlockSpec((None, TM, w), lambda bi, i: (bi, i, 0))
    lay = lambda *s: _resident((None,) + s, lambda bi, i: (layer,) + (0,) * len(s))
    return pl.pallas_call(
        _merge_kernel,
        grid=(b, l // TM),
        in_specs=[tok(d),
                  pl.BlockSpec((None, None, N_MOD, d), lambda bi, i: (bi, jnp.minimum(i, 1), 0, 0)),
                  _resident((None, None, 1, d), lambda bi, i: (layer, 1, 0, 0)),
                  tok(Q_COLS), tok(Q_COLS), tok(cw), tok(cw),
                  pl.BlockSpec((None, sub, cw),
                               lambda bi, i: (bi, jnp.maximum(i * (TM // sub) - 1, 0), 0)),
                  pl.BlockSpec((None, sub, cw),
                               lambda bi, i: (bi, jnp.minimum((i + 1) * (TM // sub), l // sub - 1), 0)),
                  lay(3, cw), lay(d, 3 * d), lay(1, 3 * d), lay(Q_COLS, d), lay(cw, d),
                  lay(Q_COLS, d), lay(d, d)],
        out_specs=tok(d),
        out_shape=jax.ShapeDtypeStruct(xs.shape, F32),
        input_output_aliases={0: 0},
        compiler_params=_params(2),
        name="merge",
    )(xs, mod, norm_g, o_a, o_c, bb, cu, cu, cu, conv_w, w_gate, b_gate, w_pa, w_pb, w_pc, w_o)


def _rope_tables(s, n_ctx):
    rows = s // GRID_W
    row = jnp.repeat(jnp.arange(rows), GRID_W).astype(F32)
    col = jnp.tile(jnp.arange(GRID_W), rows).astype(F32)
    half = HEAD_DIM // 2
    inv = ROPE_THETA ** (-jnp.arange(0, half, 2, dtype=F32) / half)
    ang_r = row[:, None] * inv
    ang_c = col[:, None] * inv
    tabs = jnp.concatenate([jnp.cos(ang_r), jnp.sin(ang_r), jnp.cos(ang_c), jnp.sin(ang_c)],
                           axis=-1)
    one, zero = jnp.ones((n_ctx, HEAD_DIM // 4), F32), jnp.zeros((n_ctx, HEAD_DIM // 4), F32)
    ident = jnp.concatenate([one, zero, one, zero], axis=-1)
    return jnp.concatenate([ident, tabs], axis=0).T


def kernel(x, c, ctx, c_ctx, w_ada, b_ada, norm_g, ffn_w_gate, ffn_w_up, ffn_w_down, w_in, qk_g,
           sink_a, conv_w, w_pa, w_pb, w_pc, w_gate, b_gate, w_o):
    bsz, s, d = x.shape
    n_ctx = ctx.shape[1]
    depth = w_ada.shape[0]
    assert n_ctx == TM and s % TM == 0 and s % GRID_W == 0 and bsz < 8

    cvec = jnp.zeros((8, d), F32).at[:bsz].set(c).at[bsz].set(c_ctx)
    mods = _ada(cvec, w_ada, b_ada).reshape(depth, 8, N_MOD, d)
    lat = mods[:, :bsz]
    con = jnp.broadcast_to(mods[:, bsz:bsz + 1], lat.shape)
    mod_all = jnp.stack([con, lat], axis=2)

    rope = _rope_tables(s, n_ctx)
    gmax = jnp.max(jnp.abs(qk_g.astype(F32)), axis=-1)
    vmax = gmax * (HEAD_DIM ** 0.5) * jnp.array([Q_SCALE, 1.0, Q_SCALE, 1.0], F32)
    pow2 = jnp.exp2(jnp.clip(jnp.floor(jnp.log2(0.98 * F8_MAX / vmax)), -20.0, 20.0))
    unscale = 1.0 / (pow2[:, 0::2] * pow2[:, 1::2])
    fold = (pow2 * jnp.array([Q_SCALE, 1.0, Q_SCALE, 1.0], F32))[:, :, None]
    qkg = jnp.broadcast_to((qk_g.astype(F32) * fold).reshape(depth, 4 * HEAD_DIM, 1),
                           (depth, 4 * HEAD_DIM, TM))
    ng = norm_g.astype(F32)[:, :, None, :]
    wg, wu, wd = (w.astype(BF16) for w in (ffn_w_gate, ffn_w_up, ffn_w_down))
    w_in_b, w_gate_b, w_pa_b, w_pb_b, w_pc_b, w_o_b = (
        w.astype(BF16) for w in (w_in, w_gate, w_pa, w_pb, w_pc, w_o))
    b_gate3 = b_gate.astype(F32)[:, None, :]

    xs = jnp.concatenate([ctx, x], axis=1)
    for i in range(depth):
        mod = mod_all[i]
        xs = _ffn(xs, mod, ng, wg, wu, wd, layer=i, which=0, j0=0)
        qa8, qc8, ka8, kc8, vta, vtc, bb, cu = _inproj(xs, mod, ng, w_in_b, qkg, rope, layer=i)
        o_a = _window_attn(qa8, ka8, unscale[i, 0], vta, sink_a[i], 1.02 * vmax[i, 0] * vmax[i, 1])
        o_c = _global_attn(qc8, kc8, unscale[i, 1], vtc, 1.02 * vmax[i, 2] * vmax[i, 3])
        xs = _merge(xs, mod, ng, o_a, o_c, bb, cu, conv_w.astype(F32), w_gate_b, b_gate3,
                    w_pa_b, w_pb_b, w_pc_b, w_o_b, layer=i)
        xs = _ffn(xs, mod, ng, wg, wu, wd, layer=i, which=1, j0=6, latents_out=i + 1 == depth)
    return xs
```

```python
import functools

import jax
import jax.numpy as jnp
from jax import lax
from jax.experimental import pallas as pl
from jax.experimental.pallas import tpu as pltpu

GRID_W = 64
HEAD_DIM = 64
Q_HEADS = 8
KV_HEADS = 2
GROUP = Q_HEADS // KV_HEADS
WINDOW = 128
N_MOD = 9
ROPE_THETA = 10000.0
EPS = 1e-6
NEG_INF = -1e30
SCALE = HEAD_DIM ** -0.5
LOG2E = 1.4426950408889634
Q_SCALE = SCALE * LOG2E
SCORE_BOUND = 48.0
Q_COLS = Q_HEADS * HEAD_DIM
KV_COLS = KV_HEADS * HEAD_DIM
VT_ROWS = HEAD_DIM + 16

LANES = 128
TM = 256
FFN_TM = 512
CHUNK = 256
P_BUFS = 2
UNROLL = 6
WIN_LAG = 2
WIN_SUB = 2
VMEM_LIMIT = 56 * 1024 * 1024

F32 = jnp.float32
BF16 = jnp.bfloat16
F8 = jnp.float8_e4m3fn
F8_MAX = 448.0
F8_ROWS = 4 * HEAD_DIM


def _params(n_axes):
    return pltpu.CompilerParams(dimension_semantics=("arbitrary",) * n_axes,
                                vmem_limit_bytes=VMEM_LIMIT)


def _resident(shape, index_map):
    return pl.BlockSpec(shape, index_map, pipeline_mode=pl.Buffered(1))


def _sigmoid(v):
    return 1.0 / (1.0 + jnp.exp(-v))


def _rms_mod(x, g, shift, scale):
    y = x * lax.rsqrt(jnp.mean(x * x, axis=-1, keepdims=True) + EPS)
    return (y * g) * (1.0 + scale) + shift


def _ada_kernel(c_ref, w_ref, b_ref, o_ref):
    c = c_ref[...]
    s = (c * _sigmoid(c)).astype(BF16)
    o_ref[...] = jnp.dot(s, w_ref[...].astype(BF16), preferred_element_type=F32) + b_ref[...]


def _ada(cvec, w_ada, b_ada):
    depth, d, n = w_ada.shape
    tn = 1024
    return pl.pallas_call(
        _ada_kernel,
        grid=(depth, n // tn),
        in_specs=[pl.BlockSpec((8, d), lambda l, j: (0, 0)),
                  pl.BlockSpec((None, d, tn), lambda l, j: (l, 0, j)),
                  pl.BlockSpec((None, 1, tn), lambda l, j: (l, 0, j))],
        out_specs=pl.BlockSpec((None, 8, tn), lambda l, j: (l, 0, j)),
        out_shape=jax.ShapeDtypeStruct((depth, 8, n), F32),
        compiler_params=_params(2),
        name="ada",
    )(cvec, w_ada, b_ada.reshape(depth, 1, n))


def _ffn_kernel(x_ref, mod_ref, g_ref, wg_ref, wu_ref, wd_ref, o_ref, *, j0):
    xb = x_ref[...].reshape(x_ref.shape[-2:])
    outs = []
    for r in range(0, xb.shape[0], TM):
        x = xb[r:r + TM, :]
        h = _rms_mod(x, g_ref[...], mod_ref[j0:j0 + 1, :], mod_ref[j0 + 1:j0 + 2, :]).astype(BF16)
        a = jnp.dot(h, wg_ref[...], preferred_element_type=F32)
        u = jnp.dot(h, wu_ref[...], preferred_element_type=F32)
        act = (a * _sigmoid(a) * u).astype(BF16)
        y = jnp.dot(act, wd_ref[...], preferred_element_type=F32)
        outs.append(x + (0.5 * mod_ref[j0 + 2:j0 + 3, :]) * y)
    o_ref[...] = jnp.concatenate(outs, axis=0).reshape(o_ref.shape)


def _ffn(xs, mod, norm_g, wg, wu, wd, *, layer, which, j0, latents_out=False):
    b, l, d = xs.shape
    dff = wg.shape[-1]

    def call(xs, x_spec, n_steps, stream, out_spec=None, out_rows=None):
        return pl.pallas_call(
            functools.partial(_ffn_kernel, j0=j0),
            grid=(b, n_steps),
            in_specs=[x_spec,
                      pl.BlockSpec((None, None, N_MOD, d), lambda bi, i: (bi, stream, 0, 0)),
                      _resident((None, None, 1, d), lambda bi, i: (layer, 2 * which, 0, 0)),
                      _resident((None, None, d, dff), lambda bi, i: (layer, which, 0, 0)),
                      _resident((None, None, d, dff), lambda bi, i: (layer, which, 0, 0)),
                      _resident((None, None, dff, d), lambda bi, i: (layer, which, 0, 0))],
            out_specs=x_spec if out_spec is None else out_spec,
            out_shape=jax.ShapeDtypeStruct((b, l if out_spec is None else out_rows, d), F32),
            input_output_aliases={0: 0} if out_spec is None else {},
            compiler_params=_params(2),
            name="ffn_ctx" if stream == 0 else "ffn",
        )(xs, mod, norm_g, wg, wu, wd)

    tl = FFN_TM if (l - TM) % FFN_TM == 0 else TM
    lat_spec = pl.BlockSpec((pl.Element(1), pl.Element(tl), pl.Element(d)),
                            lambda bi, i: (bi, pl.multiple_of(TM + i * tl, TM), 0))
    if latents_out:
        return call(xs, lat_spec, (l - TM) // tl, 1,
                    pl.BlockSpec((None, tl, d), lambda bi, i: (bi, i, 0)), l - TM)
    xs = call(xs, pl.BlockSpec((None, TM, d), lambda bi, i: (bi, 0, 0)), 1, 0)
    return call(xs, lat_spec, (l - TM) // tl, 1)


def _norm_rope_t(tt, gain, rope):
    ms = jnp.sum(tt * tt, axis=0, keepdims=True) * (1.0 / HEAD_DIM)
    y = tt * lax.rsqrt(ms + EPS) * gain
    q = HEAD_DIM // 4
    cr, sr, cc, sc = (rope[j * q:(j + 1) * q, :] for j in range(4))
    y1, y2, y3, y4 = (y[j * q:(j + 1) * q, :] for j in range(4))
    return jnp.concatenate([y1 * cr - y2 * sr, y2 * cr + y1 * sr,
                            y3 * cc - y4 * sc, y4 * cc + y3 * sc], axis=0)


def _split8(y):
    hi = y.astype(F8).astype(F32)
    return hi, y - hi


def _inproj_kernel(x_ref, mod_ref, g_ref, w_ref, qkg_ref, rope_ref,
                   qa8_ref, qc8_ref, ka8_ref, kc8_ref, vta_ref, vtc_ref, b_ref, cu_ref):
    x = x_ref[...]
    h = _rms_mod(x, g_ref[...], mod_ref[3:4, :], mod_ref[4:5, :]).astype(BF16)
    z = jnp.dot(h, w_ref[...], preferred_element_type=F32)
    rope = rope_ref[...]

    def heads(lo, j):
        tt = z[:, lo:lo + LANES].T
        gain = qkg_ref[j * HEAD_DIM:(j + 1) * HEAD_DIM, :]
        return [_norm_rope_t(tt[0:HEAD_DIM, :], gain, rope),
                _norm_rope_t(tt[HEAD_DIM:, :], gain, rope)]

    zeros = jnp.zeros((HEAD_DIM, x.shape[0]), F32)

    for k8_ref, col, j in ((ka8_ref, 0, 1), (kc8_ref, 2 * KV_COLS, 3)):
        for g, y in enumerate(heads(col, j)):
            hi, lo = _split8(y)
            k8_ref[g] = jnp.concatenate([hi, hi, lo, zeros], axis=0).T.astype(F8)
    ones = jnp.ones((VT_ROWS - HEAD_DIM, x.shape[0]), BF16)
    for vt_ref, lo in ((vta_ref, KV_COLS), (vtc_ref, 3 * KV_COLS)):
        vt = z[:, lo:lo + KV_COLS].T.astype(BF16)
        for g in range(KV_HEADS):
            vt_ref[g * VT_ROWS:g * VT_ROWS + HEAD_DIM, :] = vt[g * HEAD_DIM:(g + 1) * HEAD_DIM, :]
            vt_ref[g * VT_ROWS + HEAD_DIM:(g + 1) * VT_ROWS, :] = ones
    q0 = 4 * KV_COLS
    for c in range(Q_COLS // LANES):
        for q8_ref, col, j in ((qa8_ref, q0, 0), (qc8_ref, q0 + Q_COLS, 2)):
            for e, y in enumerate(heads(col + c * LANES, j)):
                hi, lo = _split8(y)
                r0 = (2 * c + e) * F8_ROWS
                q8_ref[r0:r0 + F8_ROWS, :] = jnp.concatenate([hi, lo, hi, zeros],
                                                             axis=0).astype(F8)
    c0 = q0 + 2 * Q_COLS
    cw = b_ref.shape[-1]
    b_ref[...] = z[:, c0:c0 + cw]
    cu_ref[...] = z[:, c0 + cw:c0 + 2 * cw] * z[:, c0 + 2 * cw:c0 + 3 * cw]


def _inproj(xs, mod, norm_g, w_in, qkg, rope, *, layer):
    b, l, d = xs.shape
    ncol = w_in.shape[-1]
    cw = (ncol - 4 * KV_COLS - 2 * Q_COLS) // 3
    tok = lambda w: pl.BlockSpec((None, TM, w), lambda bi, i: (bi, i, 0))
    tr = lambda w: pl.BlockSpec((None, w, TM), lambda bi, i: (bi, 0, i))
    k8_spec = pl.BlockSpec((None, KV_HEADS, TM, F8_ROWS), lambda bi, i: (bi, 0, i, 0))
    return pl.pallas_call(
        _inproj_kernel,
        grid=(b, l // TM),
        in_specs=[tok(d),
                  pl.BlockSpec((None, None, N_MOD, d), lambda bi, i: (bi, jnp.minimum(i, 1), 0, 0)),
                  _resident((None, None, 1, d), lambda bi, i: (layer, 1, 0, 0)),
                  _resident((None, d, ncol), lambda bi, i: (layer, 0, 0)),
                  _resident((None, 4 * HEAD_DIM, TM), lambda bi, i: (layer, 0, 0)),
                  pl.BlockSpec((HEAD_DIM, TM), lambda bi, i: (0, i))],
        out_specs=[tr(Q_HEADS * F8_ROWS), tr(Q_HEADS * F8_ROWS), k8_spec, k8_spec,
                   tr(KV_HEADS * VT_ROWS), tr(KV_HEADS * VT_ROWS), tok(cw), tok(cw)],
        out_shape=[jax.ShapeDtypeStruct((b, Q_HEADS * F8_ROWS, l), F8),
                   jax.ShapeDtypeStruct((b, Q_HEADS * F8_ROWS, l), F8),
                   jax.ShapeDtypeStruct((b, KV_HEADS, l, F8_ROWS), F8),
                   jax.ShapeDtypeStruct((b, KV_HEADS, l, F8_ROWS), F8),
                   jax.ShapeDtypeStruct((b, KV_HEADS * VT_ROWS, l), BF16),
                   jax.ShapeDtypeStruct((b, KV_HEADS * VT_ROWS, l), BF16),
                   jax.ShapeDtypeStruct((b, l, cw), F32),
                   jax.ShapeDtypeStruct((b, l, cw), F32)],
        compiler_params=_params(2),
        name="inproj",
    )(xs, mod, norm_g, w_in, qkg, rope)


def _score_chunk(q8_ref, k8_ref, un_ref, start, ch, h):
    s = jnp.dot(k8_ref[h // GROUP, pl.ds(start, ch), :], q8_ref[h * F8_ROWS:(h + 1) * F8_ROWS, :],
                preferred_element_type=F32)
    return s * un_ref[...]


def _attend(q8_ref, k8_ref, un_ref, vt_ref, acc_ref, m_ref, al_ref, smax_ref, s_bufs, p_bufs, *,
            n_keys, tk):
    n_kt = n_keys // tk
    arows = lambda h: slice(h * VT_ROWS, (h + 1) * VT_ROWS)
    m_ref[...] = jnp.full(m_ref.shape, NEG_INF, F32)
    acc_ref[...] = jnp.zeros(acc_ref.shape, F32)

    ch = min(tk, CHUNK)
    n_ch = tk // ch

    def key_start(t, c):
        return t * tk + c * ch if isinstance(t, int) else pl.multiple_of(t * tk + c * ch, ch)

    def step(sc=None, sm=None, va=None):
        if sm is not None:
            m = m_ref[sm:sm + 1, :]
            m_new = jnp.maximum(m, smax_ref[sm:sm + 1, :])
            m_ref[sm:sm + 1, :] = m_new
            al_ref[sm:sm + 1, :] = jnp.exp2(m - m_new)
        smax = pv = None
        for c in range(n_ch):
            rows = slice(c * ch, (c + 1) * ch)
            if sc is not None:
                t, h = sc
                s = _score_chunk(q8_ref, k8_ref, un_ref, key_start(t, c), ch, h)
                s_bufs[h % 2][rows, :] = s
                cmax = jnp.max(s, axis=0, keepdims=True)
                smax = cmax if smax is None else jnp.maximum(smax, cmax)
            if sm is not None:
                p_bufs[sm % 2][rows, :] = jnp.exp2(s_bufs[sm % 2][rows, :] - m_new).astype(BF16)
            if va is not None:
                t, h = va
                g = h // GROUP
                vt = vt_ref[g * VT_ROWS:(g + 1) * VT_ROWS, pl.ds(key_start(t, c), ch)]
                d = jnp.dot(vt, p_bufs[h % 2][rows, :], preferred_element_type=F32)
                pv = d if pv is None else pv + d
        if sc is not None:
            smax_ref[sc[1]:sc[1] + 1, :] = smax
        if va is not None:
            h = va[1]
            acc_ref[arows(h), :] = al_ref[h:h + 1, :] * acc_ref[arows(h), :] + pv

    def tile_steps(t, last):
        for h in range(Q_HEADS):
            if h + 2 < Q_HEADS:
                sc = (t, h + 2)
            else:
                sc = None if last else (t + 1, h + 2 - Q_HEADS)
            if h + 1 < Q_HEADS:
                sm = h + 1
            else:
                sm = None if last else 0
            step(sc, sm, (t, h))

    step(sc=(0, 0))
    step(sc=(0, 1), sm=0)
    if n_kt > 1:
        def body(t, carry):
            tile_steps(t, False)
            return carry
        lax.fori_loop(0, n_kt - 1, body, 0)
    tile_steps(n_kt - 1, True)


def _attend_bounded(q8_ref, k8_ref, un_ref, vt_ref, acc_ref, p_bufs, *, n_keys, tk):
    n_kt = n_keys // tk
    n_buf = len(p_bufs)
    arows = lambda h: slice(h * VT_ROWS, (h + 1) * VT_ROWS)
    acc_ref[...] = jnp.zeros(acc_ref.shape, F32)
    ch = min(tk, CHUNK)
    n_ch = tk // ch

    def key_start(t, c):
        return t * tk + c * ch if isinstance(t, int) else pl.multiple_of(t * tk + c * ch, ch)

    def step(sc=None, va=None):
        if va is not None:
            t, h = va
            g = h // GROUP
            vt = vt_ref[g * VT_ROWS:(g + 1) * VT_ROWS, pl.ds(key_start(t, 0), tk)]
            acc_ref[arows(h), :] = acc_ref[arows(h), :] + jnp.dot(
                vt, p_bufs[h % n_buf][0:tk, :], preferred_element_type=F32)
        if sc is not None:
            t, h = sc
            for c in range(n_ch):
                s = _score_chunk(q8_ref, k8_ref, un_ref, key_start(t, c), ch, h)
                p_bufs[h % n_buf][c * ch:(c + 1) * ch, :] = jnp.exp2(s).astype(BF16)

    def tile_steps(t, last):
        for h in range(Q_HEADS):
            if h + 1 < Q_HEADS:
                sc = (t, h + 1)
            else:
                sc = None if last else (t + 1, 0)
            step(sc, (t, h))

    step(sc=(0, 0))
    n_trips, n_rest = divmod(n_kt - 1, UNROLL)
    if n_trips > 0:
        def body(t, carry):
            for u in range(UNROLL):
                tile_steps(t * UNROLL + u, False)
            return carry
        lax.fori_loop(0, n_trips, body, 0)
    for u in range(n_rest):
        tile_steps(n_trips * UNROLL + u, False)
    tile_steps(n_kt - 1, True)


def _global_attn_kernel(q8_ref, k8_ref, un_ref, vt_ref, o_ref, acc_ref, *scratch, tk, bounded):
    i = pl.program_id(1)
    if bounded:
        attend = functools.partial(_attend_bounded, q8_ref, k8_ref, un_ref, vt_ref, acc_ref, scratch)
    else:
        m_ref, al_ref, smax_ref, s0_ref, s1_ref, p0_ref, p1_ref = scratch
        attend = functools.partial(_attend, q8_ref, k8_ref, un_ref, vt_ref, acc_ref, m_ref, al_ref,
                                   smax_ref, (s0_ref, s1_ref), (p0_ref, p1_ref))

    @pl.when(i == 0)
    def _():
        attend(n_keys=TM, tk=TM)

    @pl.when(i > 0)
    def _():
        attend(n_keys=k8_ref.shape[1], tk=tk)

    outs = []
    for h in range(Q_HEADS):
        a = acc_ref[h * VT_ROWS:(h + 1) * VT_ROWS, :]
        outs.append(a[0:HEAD_DIM, :] / a[HEAD_DIM:HEAD_DIM + 1, :])
    o_ref[...] = jnp.concatenate(outs, axis=0).T


def _key_tile(l):
    return next(t for t in (1280, 1024, 768, 512, 256) if l % t == 0)


def _global_attn(q8, k8, unscale, vt, score_bound):
    un = jnp.broadcast_to(unscale.astype(F32), (1, TM))
    return lax.cond(score_bound <= SCORE_BOUND,
                    functools.partial(_global_attn_call, bounded=True),
                    functools.partial(_global_attn_call, bounded=False), q8, k8, un, vt)


def _global_attn_call(q8, k8, un, vt, *, bounded):
    b, _, l = q8.shape
    tk = _key_tile(l)
    stat = pltpu.VMEM((Q_HEADS, TM), F32)
    p_buf = pltpu.VMEM((tk, TM), BF16)
    s_buf = pltpu.VMEM((tk, TM), F32)
    scratch = [p_buf] * P_BUFS if bounded else [stat, stat, stat, s_buf, s_buf, p_buf, p_buf]
    return pl.pallas_call(
        functools.partial(_global_attn_kernel, tk=tk, bounded=bounded),
        grid=(b, l // TM),
        in_specs=[pl.BlockSpec((None, Q_HEADS * F8_ROWS, TM), lambda bi, i: (bi, 0, i)),
                  pl.BlockSpec((None, KV_HEADS, l, F8_ROWS), lambda bi, i: (bi, 0, 0, 0)),
                  pl.BlockSpec((1, TM), lambda bi, i: (0, 0)),
                  pl.BlockSpec((None, KV_HEADS * VT_ROWS, l), lambda bi, i: (bi, 0, 0))],
        out_specs=pl.BlockSpec((None, TM, Q_COLS), lambda bi, i: (bi, i, 0)),
        out_shape=jax.ShapeDtypeStruct((b, l, Q_COLS), F32),
        scratch_shapes=[pltpu.VMEM((Q_HEADS * VT_ROWS, TM), F32)] + scratch,
        compiler_params=_params(2),
        name="global_attn_bounded" if bounded else "global_attn",
    )(q8, k8, un, vt)


def _window_geometry(i, tq, l_all):
    span = tq + 2 * WINDOW
    start = pl.multiple_of(jnp.clip(i * tq - WINDOW, 0, l_all - span), LANES)
    kpos = start - TM + lax.broadcasted_iota(jnp.int32, (span, 1), 0)
    qbase = jnp.where(i >= 1, i * tq - TM, -(1 << 20))
    qpos = qbase + lax.broadcasted_iota(jnp.int32, (1, tq), 1)
    ok = jnp.logical_and(kpos >= 0, jnp.abs(qpos - kpos) <= WINDOW)
    return start, jnp.where(ok, 0.0, NEG_INF).astype(F32)


def _window_attn_kernel(q8_ref, k8_ref, un_ref, vt_ref, sink_ref, o_ref, acc_ref):
    tq = q8_ref.shape[-1]
    span = tq + 2 * WINDOW
    start, bias = _window_geometry(pl.program_id(1), tq, k8_ref.shape[1])
    for h in range(Q_HEADS):
        g = h // GROUP
        s1 = _score_chunk(q8_ref, k8_ref, un_ref, 0, TM, h)
        s2 = _score_chunk(q8_ref, k8_ref, un_ref, start, span, h) + bias
        sink = sink_ref[h:h + 1, :]
        m = jnp.maximum(jnp.maximum(jnp.max(s1, axis=0, keepdims=True),
                                    jnp.max(s2, axis=0, keepdims=True)), sink)
        p1 = jnp.exp2(s1 - m)
        p2 = jnp.exp2(s2 - m)
        l = (jnp.sum(p1, axis=0, keepdims=True) + jnp.sum(p2, axis=0, keepdims=True)
             + jnp.exp2(sink - m))
        rows = slice(g * VT_ROWS, g * VT_ROWS + HEAD_DIM)
        acc = (jnp.dot(vt_ref[rows, 0:TM], p1.astype(BF16), preferred_element_type=F32)
               + jnp.dot(vt_ref[rows, pl.ds(start, span)], p2.astype(BF16),
                         preferred_element_type=F32))
        acc_ref[h * HEAD_DIM:(h + 1) * HEAD_DIM, :] = acc / l
    o_ref[...] = acc_ref[...].T


def _window_attn_bounded_kernel(*refs):
    q_refs = refs[:WIN_SUB]
    k8_ref, un_ref, vt_ref, sink_ref, o_ref, bias_ref, acc_ref = refs[WIN_SUB:WIN_SUB + 7]
    p_bufs = refs[WIN_SUB + 7:]
    tq = q_refs[0].shape[-1]
    l_all = k8_ref.shape[1]
    n_buf = len(p_bufs)
    starts = []
    for e in range(WIN_SUB):
        t = jnp.minimum(WIN_SUB * pl.program_id(1) + e, l_all // tq - 1)
        start, bias = _window_geometry(t, tq, l_all)
        bias_ref[e] = bias
        starts.append(start)
    n_win = bias_ref.shape[1] // CHUNK
    items = [(e, h) for h in range(Q_HEADS) for e in range(WIN_SUB)]

    def chunks(e):
        return [(0, None)] + [(pl.multiple_of(starts[e] + c * CHUNK, LANES), c)
                              for c in range(n_win)]

    def step(sc=None, va=None):
        if va is not None:
            e, h = items[va]
            g = h // GROUP
            pv = None
            for ci, (k0, _) in enumerate(chunks(e)):
                vt = vt_ref[g * VT_ROWS:(g + 1) * VT_ROWS, pl.ds(k0, CHUNK)]
                d = jnp.dot(vt, p_bufs[va % n_buf][ci * CHUNK:(ci + 1) * CHUNK, :],
                            preferred_element_type=F32)
                pv = d if pv is None else pv + d
            acc_ref[va * VT_ROWS:(va + 1) * VT_ROWS, :] = pv
        if sc is not None:
            e, h = items[sc]
            for ci, (k0, c) in enumerate(chunks(e)):
                s = _score_chunk(q_refs[e], k8_ref, un_ref, k0, CHUNK, h)
                if c is not None:
                    s = s + bias_ref[e, c * CHUNK:(c + 1) * CHUNK, :]
                p_bufs[sc % n_buf][ci * CHUNK:(ci + 1) * CHUNK, :] = jnp.exp2(s).astype(BF16)

    for n in range(WIN_LAG):
        step(sc=n)
    for n in range(len(items)):
        step(sc=n + WIN_LAG if n + WIN_LAG < len(items) else None, va=n)
    for e in range(WIN_SUB):
        outs = []
        for h in range(Q_HEADS):
            n = items.index((e, h))
            a = acc_ref[n * VT_ROWS:(n + 1) * VT_ROWS, :]
            l = a[HEAD_DIM:HEAD_DIM + 1, :] + jnp.exp2(sink_ref[h:h + 1, :])
            outs.append(a[0:HEAD_DIM, :] / l)
        o_ref[e * tq:(e + 1) * tq, :] = jnp.concatenate(outs, axis=0).T


def _window_attn(q8, k8, unscale, vt, sink, score_bound):
    sink2 = sink.astype(F32) * LOG2E
    ok = jnp.logical_and(score_bound <= SCORE_BOUND, jnp.max(jnp.abs(sink2)) <= SCORE_BOUND)
    sink_b = jnp.broadcast_to(sink2[:, None], (Q_HEADS, TM))
    un = jnp.broadcast_to(unscale.astype(F32), (1, TM))
    return lax.cond(ok, functools.partial(_window_attn_call, bounded=True),
                    functools.partial(_window_attn_call, bounded=False), q8, k8, un, vt, sink_b)


def _window_attn_call(q8, k8, un, vt, sink_b, *, bounded):
    b, _, l = q8.shape
    n_tiles = l // TM
    span = TM + 2 * WINDOW
    n_sub = WIN_SUB if bounded else 1
    q_spec = lambda e: pl.BlockSpec(
        (None, Q_HEADS * F8_ROWS, TM),
        lambda bi, i: (bi, 0, jnp.minimum(n_sub * i + e, n_tiles - 1)))
    if bounded:
        scratch = [pltpu.VMEM((WIN_SUB, span, TM), F32),
                   pltpu.VMEM((WIN_SUB * Q_HEADS * VT_ROWS, TM), F32)]
        scratch += [pltpu.VMEM((TM + span, TM), BF16)] * (2 * WIN_LAG)
    else:
        scratch = [pltpu.VMEM((Q_COLS, TM), F32)]
    return pl.pallas_call(
        _window_attn_bounded_kernel if bounded else _window_attn_kernel,
        grid=(b, pl.cdiv(n_tiles, n_sub)),
        in_specs=[q_spec(e) for e in range(n_sub)] + [
            pl.BlockSpec((None, KV_HEADS, l, F8_ROWS), lambda bi, i: (bi, 0, 0, 0)),
            pl.BlockSpec((1, TM), lambda bi, i: (0, 0)),
            pl.BlockSpec((None, KV_HEADS * VT_ROWS, l), lambda bi, i: (bi, 0, 0)),
            pl.BlockSpec((Q_HEADS, TM), lambda bi, i: (0, 0))],
        out_specs=pl.BlockSpec((None, n_sub * TM, Q_COLS), lambda bi, i: (bi, i, 0)),
        out_shape=jax.ShapeDtypeStruct((b, l, Q_COLS), F32),
        scratch_shapes=scratch,
        compiler_params=_params(2),
        name="window_attn_bounded" if bounded else "window_attn",
    )(*([q8] * n_sub), k8, un, vt, sink_b)


def _merge_kernel(x_ref, mod_ref, g_ref, oa_ref, oc_ref, b_ref, cu_ref, cup_ref, cun_ref, cw_ref,
                  wgate_ref, bgate_ref, wpa_ref, wpb_ref, wpc_ref, wo_ref, o_ref):
    i = pl.program_id(1)
    n_tiles = pl.num_programs(1)
    x = x_ref[...]
    tm, d = x.shape
    h = _rms_mod(x, g_ref[...], mod_ref[3:4, :], mod_ref[4:5, :]).astype(BF16)
    gates = _sigmoid(jnp.dot(h, wgate_ref[...], preferred_element_type=F32) + bgate_ref[...])

    cu = cu_ref[...]
    has_prev = (i >= 2).astype(F32)
    has_next = jnp.logical_and(i >= 1, i < n_tiles - 1).astype(F32)
    prev_row = cup_ref[7:8, :] * has_prev
    next_row = cun_ref[0:1, :] * has_next
    row = lax.broadcasted_iota(jnp.int32, (tm, 1), 0)
    cu_dn = jnp.where(row == 0, prev_row, pltpu.roll(cu, 1, 0))
    cu_up = jnp.where(row == tm - 1, next_row, pltpu.roll(cu, tm - 1, 0))
    y = cw_ref[0:1, :] * cu_dn + cw_ref[1:2, :] * cu + cw_ref[2:3, :] * cu_up
    o_b = (b_ref[...] * y).astype(BF16)

    pa = jnp.dot(oa_ref[...].astype(BF16), wpa_ref[...], preferred_element_type=F32)
    pb = jnp.dot(o_b, wpb_ref[...], preferred_element_type=F32)
    pc = jnp.dot(oc_ref[...].astype(BF16), wpc_ref[...], preferred_element_type=F32)
    mix = gates[:, 0:d] * pa + gates[:, d:2 * d] * pb + gates[:, 2 * d:3 * d] * pc
    out = jnp.dot(mix.astype(BF16), wo_ref[...], preferred_element_type=F32)
    o_ref[...] = x + mod_ref[5:6, :] * out


def _merge(xs, mod, norm_g, o_a, o_c, bb, cu, conv_w, w_gate, b_gate, w_pa, w_pb, w_pc, w_o,
           *, layer):
    b, l, d = xs.shape
    cw = cu.shape[-1]
    sub = 8
    tok = lambda w: pl.BlockSpec((None, TM, w), lambda bi, i: (bi, i, 0))
    lay = lambda *s: _resident((None,) + s, lambda bi, i: (layer,) + (0,) * len(s))
    return pl.pallas_call(
        _merge_kernel,
        grid=(b, l // TM),
        in_specs=[tok(d),
                  pl.BlockSpec((None, None, N_MOD, d), lambda bi, i: (bi, jnp.minimum(i, 1), 0, 0)),
                  _resident((None, None, 1, d), lambda bi, i: (layer, 1, 0, 0)),
                  tok(Q_COLS), tok(Q_COLS), tok(cw), tok(cw),
                  pl.BlockSpec((None, sub, cw),
                               lambda bi, i: (bi, jnp.maximum(i * (TM // sub) - 1, 0), 0)),
                  pl.BlockSpec((None, sub, cw),
                               lambda bi, i: (bi, jnp.minimum((i + 1) * (TM // sub), l // sub - 1), 0)),
                  lay(3, cw), lay(d, 3 * d), lay(1, 3 * d), lay(Q_COLS, d), lay(cw, d),
                  lay(Q_COLS, d), lay(d, d)],
        out_specs=tok(d),
        out_shape=jax.ShapeDtypeStruct(xs.shape, F32),
        input_output_aliases={0: 0},
        compiler_params=_params(2),
        name="merge",
    )(xs, mod, norm_g, o_a, o_c, bb, cu, cu, cu, conv_w, w_gate, b_gate, w_pa, w_pb, w_pc, w_o)


def _rope_tables(s, n_ctx):
    rows = s // GRID_W
    row = jnp.repeat(jnp.arange(rows), GRID_W).astype(F32)
    col = jnp.tile(jnp.arange(GRID_W), rows).astype(F32)
    half = HEAD_DIM // 2
    inv = ROPE_THETA ** (-jnp.arange(0, half, 2, dtype=F32) / half)
    ang_r = row[:, None] * inv
    ang_c = col[:, None] * inv
    tabs = jnp.concatenate([jnp.cos(ang_r), jnp.sin(ang_r), jnp.cos(ang_c), jnp.sin(ang_c)],
                           axis=-1)
    one, zero = jnp.ones((n_ctx, HEAD_DIM // 4), F32), jnp.zeros((n_ctx, HEAD_DIM // 4), F32)
    ident = jnp.concatenate([one, zero, one, zero], axis=-1)
    return jnp.concatenate([ident, tabs], axis=0).T


def kernel(x, c, ctx, c_ctx, w_ada, b_ada, norm_g, ffn_w_gate, ffn_w_up, ffn_w_down, w_in, qk_g,
           sink_a, conv_w, w_pa, w_pb, w_pc, w_gate, b_gate, w_o):
    bsz, s, d = x.shape
    n_ctx = ctx.shape[1]
    depth = w_ada.shape[0]
    assert n_ctx == TM and s % TM == 0 and s % GRID_W == 0 and bsz < 8

    cvec = jnp.zeros((8, d), F32).at[:bsz].set(c).at[bsz].set(c_ctx)
    mods = _ada(cvec, w_ada, b_ada).reshape(depth, 8, N_MOD, d)
    lat = mods[:, :bsz]
    con = jnp.broadcast_to(mods[:, bsz:bsz + 1], lat.shape)
    mod_all = jnp.stack([con, lat], axis=2)

    rope = _rope_tables(s, n_ctx)
    gmax = jnp.max(jnp.abs(qk_g.astype(F32)), axis=-1)
    vmax = gmax * (HEAD_DIM ** 0.5) * jnp.array([Q_SCALE, 1.0, Q_SCALE, 1.0], F32)
    pow2 = jnp.exp2(jnp.clip(jnp.floor(jnp.log2(0.98 * F8_MAX / vmax)), -20.0, 20.0))
    unscale = 1.0 / (pow2[:, 0::2] * pow2[:, 1::2])
    fold = (pow2 * jnp.array([Q_SCALE, 1.0, Q_SCALE, 1.0], F32))[:, :, None]
    qkg = jnp.broadcast_to((qk_g.astype(F32) * fold).reshape(depth, 4 * HEAD_DIM, 1),
                           (depth, 4 * HEAD_DIM, TM))
    ng = norm_g.astype(F32)[:, :, None, :]
    wg, wu, wd = (w.astype(BF16) for w in (ffn_w_gate, ffn_w_up, ffn_w_down))
    w_in_b, w_gate_b, w_pa_b, w_pb_b, w_pc_b, w_o_b = (
        w.astype(BF16) for w in (w_in, w_gate, w_pa, w_pb, w_pc, w_o))
    b_gate3 = b_gate.astype(F32)[:, None, :]

    xs = jnp.concatenate([ctx, x], axis=1)
    for i in range(depth):
        mod = mod_all[i]
        xs = _ffn(xs, mod, ng, wg, wu, wd, layer=i, which=0, j0=0)
        qa8, qc8, ka8, kc8, vta, vtc, bb, cu = _inproj(xs, mod, ng, w_in_b, qkg, rope, layer=i)
        o_a = _window_attn(qa8, ka8, unscale[i, 0], vta, sink_a[i], 1.02 * vmax[i, 0] * vmax[i, 1])
        o_c = _global_attn(qc8, kc8, unscale[i, 1], vtc, 1.02 * vmax[i, 2] * vmax[i, 3])
        xs = _merge(xs, mod, ng, o_a, o_c, bb, cu, conv_w.astype(F32), w_gate_b, b_gate3,
                    w_pa_b, w_pb_b, w_pc_b, w_o_b, layer=i)
        xs = _ffn(xs, mod, ng, wg, wu, wd, layer=i, which=1, j0=6, latents_out=i + 1 == depth)
    return xs
```

```python
import functools

import jax
import jax.numpy as jnp
from jax import lax
from jax.experimental import pallas as pl
from jax.experimental.pallas import tpu as pltpu

GRID_W = 64
HEAD_DIM = 64
Q_HEADS = 8
KV_HEADS = 2
GROUP = Q_HEADS // KV_HEADS
WINDOW = 128
N_MOD = 9
ROPE_THETA = 10000.0
EPS = 1e-6
NEG_INF = -1e30
SCALE = HEAD_DIM ** -0.5
LOG2E = 1.4426950408889634
Q_SCALE = SCALE * LOG2E
SCORE_BOUND = 48.0
Q_COLS = Q_HEADS * HEAD_DIM
KV_COLS = KV_HEADS * HEAD_DIM
VT_ROWS = HEAD_DIM + 16

LANES = 128
TM = 256
FFN_TM = 512
CHUNK = 256
P_BUFS = 2
UNROLL = 12
WIN_LAG = 2
WIN_SUB = 2
VMEM_LIMIT = 56 * 1024 * 1024

F32 = jnp.float32
BF16 = jnp.bfloat16
F8 = jnp.float8_e4m3fn
F8_MAX = 448.0
F8_ROWS = 4 * HEAD_DIM


def _params(n_axes):
    return pltpu.CompilerParams(dimension_semantics=("arbitrary",) * n_axes,
                                vmem_limit_bytes=VMEM_LIMIT)


def _resident(shape, index_map):
    return pl.BlockSpec(shape, index_map, pipeline_mode=pl.Buffered(1))


def _sigmoid(v):
    return 1.0 / (1.0 + jnp.exp(-v))


def _rms_mod(x, g, shift, scale):
    y = x * lax.rsqrt(jnp.mean(x * x, axis=-1, keepdims=True) + EPS)
    return (y * g) * (1.0 + scale) + shift


def _ada_kernel(c_ref, w_ref, b_ref, o_ref):
    c = c_ref[...]
    s = (c * _sigmoid(c)).astype(BF16)
    o_ref[...] = jnp.dot(s, w_ref[...].astype(BF16), preferred_element_type=F32) + b_ref[...]


def _ada(cvec, w_ada, b_ada):
    depth, d, n = w_ada.shape
    tn = 1024
    return pl.pallas_call(
        _ada_kernel,
        grid=(depth, n // tn),
        in_specs=[pl.BlockSpec((8, d), lambda l, j: (0, 0)),
                  pl.BlockSpec((None, d, tn), lambda l, j: (l, 0, j)),
                  pl.BlockSpec((None, 1, tn), lambda l, j: (l, 0, j))],
        out_specs=pl.BlockSpec((None, 8, tn), lambda l, j: (l, 0, j)),
        out_shape=jax.ShapeDtypeStruct((depth, 8, n), F32),
        compiler_params=_params(2),
        name="ada",
    )(cvec, w_ada, b_ada.reshape(depth, 1, n))


def _ffn_kernel(x_ref, mod_ref, g_ref, wg_ref, wu_ref, wd_ref, o_ref, *, j0):
    xb = x_ref[...].reshape(x_ref.shape[-2:])
    outs = []
    for r in range(0, xb.shape[0], TM):
        x = xb[r:r + TM, :]
        h = _rms_mod(x, g_ref[...], mod_ref[j0:j0 + 1, :], mod_ref[j0 + 1:j0 + 2, :]).astype(BF16)
        a = jnp.dot(h, wg_ref[...], preferred_element_type=F32)
        u = jnp.dot(h, wu_ref[...], preferred_element_type=F32)
        act = (a * _sigmoid(a) * u).astype(BF16)
        y = jnp.dot(act, wd_ref[...], preferred_element_type=F32)
        outs.append(x + (0.5 * mod_ref[j0 + 2:j0 + 3, :]) * y)
    o_ref[...] = jnp.concatenate(outs, axis=0).reshape(o_ref.shape)


def _ffn(xs, mod, norm_g, wg, wu, wd, *, layer, which, j0, latents_out=False):
    b, l, d = xs.shape
    dff = wg.shape[-1]

    def call(xs, x_spec, n_steps, stream, out_spec=None, out_rows=None):
        return pl.pallas_call(
            functools.partial(_ffn_kernel, j0=j0),
            grid=(b, n_steps),
            in_specs=[x_spec,
                      pl.BlockSpec((None, None, N_MOD, d), lambda bi, i: (bi, stream, 0, 0)),
                      _resident((None, None, 1, d), lambda bi, i: (layer, 2 * which, 0, 0)),
                      _resident((None, None, d, dff), lambda bi, i: (layer, which, 0, 0)),
                      _resident((None, None, d, dff), lambda bi, i: (layer, which, 0, 0)),
                      _resident((None, None, dff, d), lambda bi, i: (layer, which, 0, 0))],
            out_specs=x_spec if out_spec is None else out_spec,
            out_shape=jax.ShapeDtypeStruct((b, l if out_spec is None else out_rows, d), F32),
            input_output_aliases={0: 0} if out_spec is None else {},
            compiler_params=_params(2),
            name="ffn_ctx" if stream == 0 else "ffn",
        )(xs, mod, norm_g, wg, wu, wd)

    tl = FFN_TM if (l - TM) % FFN_TM == 0 else TM
    lat_spec = pl.BlockSpec((pl.Element(1), pl.Element(tl), pl.Element(d)),
                            lambda bi, i: (bi, pl.multiple_of(TM + i * tl, TM), 0))
    if latents_out:
        return call(xs, lat_spec, (l - TM) // tl, 1,
                    pl.BlockSpec((None, tl, d), lambda bi, i: (bi, i, 0)), l - TM)
    xs = call(xs, pl.BlockSpec((None, TM, d), lambda bi, i: (bi, 0, 0)), 1, 0)
    return call(xs, lat_spec, (l - TM) // tl, 1)


def _norm_rope_t(tt, gain, rope):
    ms = jnp.sum(tt * tt, axis=0, keepdims=True) * (1.0 / HEAD_DIM)
    y = tt * lax.rsqrt(ms + EPS) * gain
    q = HEAD_DIM // 4
    cr, sr, cc, sc = (rope[j * q:(j + 1) * q, :] for j in range(4))
    y1, y2, y3, y4 = (y[j * q:(j + 1) * q, :] for j in range(4))
    return jnp.concatenate([y1 * cr - y2 * sr, y2 * cr + y1 * sr,
                            y3 * cc - y4 * sc, y4 * cc + y3 * sc], axis=0)


def _split8(y):
    hi = y.astype(F8).astype(F32)
    return hi, y - hi


def _inproj_kernel(x_ref, mod_ref, g_ref, w_ref, qkg_ref, rope_ref,
                   qa8_ref, qc8_ref, ka8_ref, kc8_ref, vta_ref, vtc_ref, b_ref, cu_ref):
    x = x_ref[...]
    h = _rms_mod(x, g_ref[...], mod_ref[3:4, :], mod_ref[4:5, :]).astype(BF16)
    z = jnp.dot(h, w_ref[...], preferred_element_type=F32)
    rope = rope_ref[...]

    def heads(lo, j):
        tt = z[:, lo:lo + LANES].T
        gain = qkg_ref[j * HEAD_DIM:(j + 1) * HEAD_DIM, :]
        return [_norm_rope_t(tt[0:HEAD_DIM, :], gain, rope),
                _norm_rope_t(tt[HEAD_DIM:, :], gain, rope)]

    zeros = jnp.zeros((HEAD_DIM, x.shape[0]), F32)

    for k8_ref, col, j in ((ka8_ref, 0, 1), (kc8_ref, 2 * KV_COLS, 3)):
        for g, y in enumerate(heads(col, j)):
            hi, lo = _split8(y)
            k8_ref[g] = jnp.concatenate([hi, hi, lo, zeros], axis=0).T.astype(F8)
    ones = jnp.ones((VT_ROWS - HEAD_DIM, x.shape[0]), BF16)
    for vt_ref, lo in ((vta_ref, KV_COLS), (vtc_ref, 3 * KV_COLS)):
        vt = z[:, lo:lo + KV_COLS].T.astype(BF16)
        for g in range(KV_HEADS):
            vt_ref[g * VT_ROWS:g * VT_ROWS + HEAD_DIM, :] = vt[g * HEAD_DIM:(g + 1) * HEAD_DIM, :]
            vt_ref[g * VT_ROWS + HEAD_DIM:(g + 1) * VT_ROWS, :] = ones
    q0 = 4 * KV_COLS
    for c in range(Q_COLS // LANES):
        for q8_ref, col, j in ((qa8_ref, q0, 0), (qc8_ref, q0 + Q_COLS, 2)):
            for e, y in enumerate(heads(col + c * LANES, j)):
                hi, lo = _split8(y)
                r0 = (2 * c + e) * F8_ROWS
                q8_ref[r0:r0 + F8_ROWS, :] = jnp.concatenate([hi, lo, hi, zeros],
                                                             axis=0).astype(F8)
    c0 = q0 + 2 * Q_COLS
    cw = b_ref.shape[-1]
    b_ref[...] = z[:, c0:c0 + cw]
    cu_ref[...] = z[:, c0 + cw:c0 + 2 * cw] * z[:, c0 + 2 * cw:c0 + 3 * cw]


def _inproj(xs, mod, norm_g, w_in, qkg, rope, *, layer):
    b, l, d = xs.shape
    ncol = w_in.shape[-1]
    cw = (ncol - 4 * KV_COLS - 2 * Q_COLS) // 3
    tok = lambda w: pl.BlockSpec((None, TM, w), lambda bi, i: (bi, i, 0))
    tr = lambda w: pl.BlockSpec((None, w, TM), lambda bi, i: (bi, 0, i))
    k8_spec = pl.BlockSpec((None, KV_HEADS, TM, F8_ROWS), lambda bi, i: (bi, 0, i, 0))
    return pl.pallas_call(
        _inproj_kernel,
        grid=(b, l // TM),
        in_specs=[tok(d),
                  pl.BlockSpec((None, None, N_MOD, d), lambda bi, i: (bi, jnp.minimum(i, 1), 0, 0)),
                  _resident((None, None, 1, d), lambda bi, i: (layer, 1, 0, 0)),
                  _resident((None, d, ncol), lambda bi, i: (layer, 0, 0)),
                  _resident((None, 4 * HEAD_DIM, TM), lambda bi, i: (layer, 0, 0)),
                  pl.BlockSpec((HEAD_DIM, TM), lambda bi, i: (0, i))],
        out_specs=[tr(Q_HEADS * F8_ROWS), tr(Q_HEADS * F8_ROWS), k8_spec, k8_spec,
                   tr(KV_HEADS * VT_ROWS), tr(KV_HEADS * VT_ROWS), tok(cw), tok(cw)],
        out_shape=[jax.ShapeDtypeStruct((b, Q_HEADS * F8_ROWS, l), F8),
                   jax.ShapeDtypeStruct((b, Q_HEADS * F8_ROWS, l), F8),
                   jax.ShapeDtypeStruct((b, KV_HEADS, l, F8_ROWS), F8),
                   jax.ShapeDtypeStruct((b, KV_HEADS, l, F8_ROWS), F8),
                   jax.ShapeDtypeStruct((b, KV_HEADS * VT_ROWS, l), BF16),
                   jax.ShapeDtypeStruct((b, KV_HEADS * VT_ROWS, l), BF16),
                   jax.ShapeDtypeStruct((b, l, cw), F32),
                   jax.ShapeDtypeStruct((b, l, cw), F32)],
        compiler_params=_params(2),
        name="inproj",
    )(xs, mod, norm_g, w_in, qkg, rope)


def _score_chunk(q8_ref, k8_ref, un_ref, start, ch, h):
    s = jnp.dot(k8_ref[h // GROUP, pl.ds(start, ch), :], q8_ref[h * F8_ROWS:(h + 1) * F8_ROWS, :],
                preferred_element_type=F32)
    return s * un_ref[...]


def _attend(q8_ref, k8_ref, un_ref, vt_ref, acc_ref, m_ref, al_ref, smax_ref, s_bufs, p_bufs, *,
            n_keys, tk):
    n_kt = n_keys // tk
    arows = lambda h: slice(h * VT_ROWS, (h + 1) * VT_ROWS)
    m_ref[...] = jnp.full(m_ref.shape, NEG_INF, F32)
    acc_ref[...] = jnp.zeros(acc_ref.shape, F32)

    ch = min(tk, CHUNK)
    n_ch = tk // ch

    def key_start(t, c):
        return t * tk + c * ch if isinstance(t, int) else pl.multiple_of(t * tk + c * ch, ch)

    def step(sc=None, sm=None, va=None):
        if sm is not None:
            m = m_ref[sm:sm + 1, :]
            m_new = jnp.maximum(m, smax_ref[sm:sm + 1, :])
            m_ref[sm:sm + 1, :] = m_new
            al_ref[sm:sm + 1, :] = jnp.exp2(m - m_new)
        smax = pv = None
        for c in range(n_ch):
            rows = slice(c * ch, (c + 1) * ch)
            if sc is not None:
                t, h = sc
                s = _score_chunk(q8_ref, k8_ref, un_ref, key_start(t, c), ch, h)
                s_bufs[h % 2][rows, :] = s
                cmax = jnp.max(s, axis=0, keepdims=True)
                smax = cmax if smax is None else jnp.maximum(smax, cmax)
            if sm is not None:
                p_bufs[sm % 2][rows, :] = jnp.exp2(s_bufs[sm % 2][rows, :] - m_new).astype(BF16)
            if va is not None:
                t, h = va
                g = h // GROUP
                vt = vt_ref[g * VT_ROWS:(g + 1) * VT_ROWS, pl.ds(key_start(t, c), ch)]
                d = jnp.dot(vt, p_bufs[h % 2][rows, :], preferred_element_type=F32)
                pv = d if pv is None else pv + d
        if sc is not None:
            smax_ref[sc[1]:sc[1] + 1, :] = smax
        if va is not None:
            h = va[1]
            acc_ref[arows(h), :] = al_ref[h:h + 1, :] * acc_ref[arows(h), :] + pv

    def tile_steps(t, last):
        for h in range(Q_HEADS):
            if h + 2 < Q_HEADS:
                sc = (t, h + 2)
            else:
                sc = None if last else (t + 1, h + 2 - Q_HEADS)
            if h + 1 < Q_HEADS:
                sm = h + 1
            else:
                sm = None if last else 0
            step(sc, sm, (t, h))

    step(sc=(0, 0))
    step(sc=(0, 1), sm=0)
    if n_kt > 1:
        def body(t, carry):
            tile_steps(t, False)
            return carry
        lax.fori_loop(0, n_kt - 1, body, 0)
    tile_steps(n_kt - 1, True)


def _attend_bounded(q8_ref, k8_ref, un_ref, vt_ref, acc_ref, p_bufs, *, n_keys, tk):
    n_kt = n_keys // tk
    n_buf = len(p_bufs)
    arows = lambda h: slice(h * VT_ROWS, (h + 1) * VT_ROWS)
    acc_ref[...] = jnp.zeros(acc_ref.shape, F32)
    ch = min(tk, CHUNK)
    n_ch = tk // ch

    def key_start(t, c):
        return t * tk + c * ch if isinstance(t, int) else pl.multiple_of(t * tk + c * ch, ch)

    def step(sc=None, va=None):
        if va is not None:
            t, h = va
            g = h // GROUP
            vt = vt_ref[g * VT_ROWS:(g + 1) * VT_ROWS, pl.ds(key_start(t, 0), tk)]
            acc_ref[arows(h), :] = acc_ref[arows(h), :] + jnp.dot(
                vt, p_bufs[h % n_buf][0:tk, :], preferred_element_type=F32)
        if sc is not None:
            t, h = sc
            for c in range(n_ch):
                s = _score_chunk(q8_ref, k8_ref, un_ref, key_start(t, c), ch, h)
                p_bufs[h % n_buf][c * ch:(c + 1) * ch, :] = jnp.exp2(s).astype(BF16)

    def tile_steps(t, last):
        for h in range(Q_HEADS):
            if h + 1 < Q_HEADS:
                sc = (t, h + 1)
            else:
                sc = None if last else (t + 1, 0)
            step(sc, (t, h))

    step(sc=(0, 0))
    n_trips, n_rest = divmod(n_kt - 1, UNROLL)
    if n_trips > 0:
        def body(t, carry):
            for u in range(UNROLL):
                tile_steps(t * UNROLL + u, False)
            return carry
        lax.fori_loop(0, n_trips, body, 0)
    for u in range(n_rest):
        tile_steps(n_trips * UNROLL + u, False)
    tile_steps(n_kt - 1, True)


def _global_attn_kernel(q8_ref, k8_ref, un_ref, vt_ref, o_ref, acc_ref, *scratch, tk, bounded):
    i = pl.program_id(1)
    if bounded:
        attend = functools.partial(_attend_bounded, q8_ref, k8_ref, un_ref, vt_ref, acc_ref, scratch)
    else:
        m_ref, al_ref, smax_ref, s0_ref, s1_ref, p0_ref, p1_ref = scratch
        attend = functools.partial(_attend, q8_ref, k8_ref, un_ref, vt_ref, acc_ref, m_ref, al_ref,
                                   smax_ref, (s0_ref, s1_ref), (p0_ref, p1_ref))

    @pl.when(i == 0)
    def _():
        attend(n_keys=TM, tk=TM)

    @pl.when(i > 0)
    def _():
        attend(n_keys=k8_ref.shape[1], tk=tk)

    outs = []
    for h in range(Q_HEADS):
        a = acc_ref[h * VT_ROWS:(h + 1) * VT_ROWS, :]
        outs.append(a[0:HEAD_DIM, :] / a[HEAD_DIM:HEAD_DIM + 1, :])
    o_ref[...] = jnp.concatenate(outs, axis=0).T


def _key_tile(l):
    return next(t for t in (1280, 1024, 768, 512, 256) if l % t == 0)


def _global_attn(q8, k8, unscale, vt, score_bound):
    un = jnp.broadcast_to(unscale.astype(F32), (1, TM))
    return lax.cond(score_bound <= SCORE_BOUND,
                    functools.partial(_global_attn_call, bounded=True),
                    functools.partial(_global_attn_call, bounded=False), q8, k8, un, vt)


def _global_attn_call(q8, k8, un, vt, *, bounded):
    b, _, l = q8.shape
    tk = _key_tile(l)
    stat = pltpu.VMEM((Q_HEADS, TM), F32)
    p_buf = pltpu.VMEM((tk, TM), BF16)
    s_buf = pltpu.VMEM((tk, TM), F32)
    scratch = [p_buf] * P_BUFS if bounded else [stat, stat, stat, s_buf, s_buf, p_buf, p_buf]
    return pl.pallas_call(
        functools.partial(_global_attn_kernel, tk=tk, bounded=bounded),
        grid=(b, l // TM),
        in_specs=[pl.BlockSpec((None, Q_HEADS * F8_ROWS, TM), lambda bi, i: (bi, 0, i)),
                  pl.BlockSpec((None, KV_HEADS, l, F8_ROWS), lambda bi, i: (bi, 0, 0, 0)),
                  pl.BlockSpec((1, TM), lambda bi, i: (0, 0)),
                  pl.BlockSpec((None, KV_HEADS * VT_ROWS, l), lambda bi, i: (bi, 0, 0))],
        out_specs=pl.BlockSpec((None, TM, Q_COLS), lambda bi, i: (bi, i, 0)),
        out_shape=jax.ShapeDtypeStruct((b, l, Q_COLS), F32),
        scratch_shapes=[pltpu.VMEM((Q_HEADS * VT_ROWS, TM), F32)] + scratch,
        compiler_params=_params(2),
        name="global_attn_bounded" if bounded else "global_attn",
    )(q8, k8, un, vt)


def _window_geometry(i, tq, l_all):
    span = tq + 2 * WINDOW
    start = pl.multiple_of(jnp.clip(i * tq - WINDOW, 0, l_all - span), LANES)
    kpos = start - TM + lax.broadcasted_iota(jnp.int32, (span, 1), 0)
    qbase = jnp.where(i >= 1, i * tq - TM, -(1 << 20))
    qpos = qbase + lax.broadcasted_iota(jnp.int32, (1, tq), 1)
    ok = jnp.logical_and(kpos >= 0, jnp.abs(qpos - kpos) <= WINDOW)
    return start, jnp.where(ok, 0.0, NEG_INF).astype(F32)


def _window_attn_kernel(q8_ref, k8_ref, un_ref, vt_ref, sink_ref, o_ref, acc_ref):
    tq = q8_ref.shape[-1]
    span = tq + 2 * WINDOW
    start, bias = _window_geometry(pl.program_id(1), tq, k8_ref.shape[1])
    for h in range(Q_HEADS):
        g = h // GROUP
        s1 = _score_chunk(q8_ref, k8_ref, un_ref, 0, TM, h)
        s2 = _score_chunk(q8_ref, k8_ref, un_ref, start, span, h) + bias
        sink = sink_ref[h:h + 1, :]
        m = jnp.maximum(jnp.maximum(jnp.max(s1, axis=0, keepdims=True),
                                    jnp.max(s2, axis=0, keepdims=True)), sink)
        p1 = jnp.exp2(s1 - m)
        p2 = jnp.exp2(s2 - m)
        l = (jnp.sum(p1, axis=0, keepdims=True) + jnp.sum(p2, axis=0, keepdims=True)
             + jnp.exp2(sink - m))
        rows = slice(g * VT_ROWS, g * VT_ROWS + HEAD_DIM)
        acc = (jnp.dot(vt_ref[rows, 0:TM], p1.astype(BF16), preferred_element_type=F32)
               + jnp.dot(vt_ref[rows, pl.ds(start, span)], p2.astype(BF16),
                         preferred_element_type=F32))
        acc_ref[h * HEAD_DIM:(h + 1) * HEAD_DIM, :] = acc / l
    o_ref[...] = acc_ref[...].T


def _window_attn_bounded_kernel(*refs):
    q_refs = refs[:WIN_SUB]
    k8_ref, un_ref, vt_ref, sink_ref, o_ref, bias_ref, acc_ref = refs[WIN_SUB:WIN_SUB + 7]
    p_bufs = refs[WIN_SUB + 7:]
    tq = q_refs[0].shape[-1]
    l_all = k8_ref.shape[1]
    n_buf = len(p_bufs)
    starts = []
    for e in range(WIN_SUB):
        t = jnp.minimum(WIN_SUB * pl.program_id(1) + e, l_all // tq - 1)
        start, bias = _window_geometry(t, tq, l_all)
        bias_ref[e] = bias
        starts.append(start)
    n_win = bias_ref.shape[1] // CHUNK
    items = [(e, h) for h in range(Q_HEADS) for e in range(WIN_SUB)]

    def chunks(e):
        return [(0, None)] + [(pl.multiple_of(starts[e] + c * CHUNK, LANES), c)
                              for c in range(n_win)]

    def step(sc=None, va=None):
        if va is not None:
            e, h = items[va]
            g = h // GROUP
            pv = None
            for ci, (k0, _) in enumerate(chunks(e)):
                vt = vt_ref[g * VT_ROWS:(g + 1) * VT_ROWS, pl.ds(k0, CHUNK)]
                d = jnp.dot(vt, p_bufs[va % n_buf][ci * CHUNK:(ci + 1) * CHUNK, :],
                            preferred_element_type=F32)
                pv = d if pv is None else pv + d
            acc_ref[va * VT_ROWS:(va + 1) * VT_ROWS, :] = pv
        if sc is not None:
            e, h = items[sc]
            for ci, (k0, c) in enumerate(chunks(e)):
                s = _score_chunk(q_refs[e], k8_ref, un_ref, k0, CHUNK, h)
                if c is not None:
                    s = s + bias_ref[e, c * CHUNK:(c + 1) * CHUNK, :]
                p_bufs[sc % n_buf][ci * CHUNK:(ci + 1) * CHUNK, :] = jnp.exp2(s).astype(BF16)

    for n in range(WIN_LAG):
        step(sc=n)
    for n in range(len(items)):
        step(sc=n + WIN_LAG if n + WIN_LAG < len(items) else None, va=n)
    for e in range(WIN_SUB):
        outs = []
        for h in range(Q_HEADS):
            n = items.index((e, h))
            a = acc_ref[n * VT_ROWS:(n + 1) * VT_ROWS, :]
            l = a[HEAD_DIM:HEAD_DIM + 1, :] + jnp.exp2(sink_ref[h:h + 1, :])
            outs.append(a[0:HEAD_DIM, :] / l)
        o_ref[e * tq:(e + 1) * tq, :] = jnp.concatenate(outs, axis=0).T


def _window_attn(q8, k8, unscale, vt, sink, score_bound):
    sink2 = sink.astype(F32) * LOG2E
    ok = jnp.logical_and(score_bound <= SCORE_BOUND, jnp.max(jnp.abs(sink2)) <= SCORE_BOUND)
    sink_b = jnp.broadcast_to(sink2[:, None], (Q_HEADS, TM))
    un = jnp.broadcast_to(unscale.astype(F32), (1, TM))
    return lax.cond(ok, functools.partial(_window_attn_call, bounded=True),
                    functools.partial(_window_attn_call, bounded=False), q8, k8, un, vt, sink_b)


def _window_attn_call(q8, k8, un, vt, sink_b, *, bounded):
    b, _, l = q8.shape
    n_tiles = l // TM
    span = TM + 2 * WINDOW
    n_sub = WIN_SUB if bounded else 1
    q_spec = lambda e: pl.BlockSpec(
        (None, Q_HEADS * F8_ROWS, TM),
        lambda bi, i: (bi, 0, jnp.minimum(n_sub * i + e, n_tiles - 1)))
    if bounded:
        scratch = [pltpu.VMEM((WIN_SUB, span, TM), F32),
                   pltpu.VMEM((WIN_SUB * Q_HEADS * VT_ROWS, TM), F32)]
        scratch += [pltpu.VMEM((TM + span, TM), BF16)] * (2 * WIN_LAG)
    else:
        scratch = [pltpu.VMEM((Q_COLS, TM), F32)]
    return pl.pallas_call(
        _window_attn_bounded_kernel if bounded else _window_attn_kernel,
        grid=(b, pl.cdiv(n_tiles, n_sub)),
        in_specs=[q_spec(e) for e in range(n_sub)] + [
            pl.BlockSpec((None, KV_HEADS, l, F8_ROWS), lambda bi, i: (bi, 0, 0, 0)),
            pl.BlockSpec((1, TM), lambda bi, i: (0, 0)),
            pl.BlockSpec((None, KV_HEADS * VT_ROWS, l), lambda bi, i: (bi, 0, 0)),
            pl.BlockSpec((Q_HEADS, TM), lambda bi, i: (0, 0))],
        out_specs=pl.BlockSpec((None, n_sub * TM, Q_COLS), lambda bi, i: (bi, i, 0)),
        out_shape=jax.ShapeDtypeStruct((b, l, Q_COLS), F32),
        scratch_shapes=scratch,
        compiler_params=_params(2),
        name="window_attn_bounded" if bounded else "window_attn",
    )(*([q8] * n_sub), k8, un, vt, sink_b)


def _merge_kernel(x_ref, mod_ref, g_ref, oa_ref, oc_ref, b_ref, cu_ref, cup_ref, cun_ref, cw_ref,
                  wgate_ref, bgate_ref, wpa_ref, wpb_ref, wpc_ref, wo_ref, o_ref):
    i = pl.program_id(1)
    n_tiles = pl.num_programs(1)
    x = x_ref[...]
    tm, d = x.shape
    h = _rms_mod(x, g_ref[...], mod_ref[3:4, :], mod_ref[4:5, :]).astype(BF16)
    gates = _sigmoid(jnp.dot(h, wgate_ref[...], preferred_element_type=F32) + bgate_ref[...])

    cu = cu_ref[...]
    has_prev = (i >= 2).astype(F32)
    has_next = jnp.logical_and(i >= 1, i < n_tiles - 1).astype(F32)
    prev_row = cup_ref[7:8, :] * has_prev
    next_row = cun_ref[0:1, :] * has_next
    row = lax.broadcasted_iota(jnp.int32, (tm, 1), 0)
    cu_dn = jnp.where(row == 0, prev_row, pltpu.roll(cu, 1, 0))
    cu_up = jnp.where(row == tm - 1, next_row, pltpu.roll(cu, tm - 1, 0))
    y = cw_ref[0:1, :] * cu_dn + cw_ref[1:2, :] * cu + cw_ref[2:3, :] * cu_up
    o_b = (b_ref[...] * y).astype(BF16)

    pa = jnp.dot(oa_ref[...].astype(BF16), wpa_ref[...], preferred_element_type=F32)
    pb = jnp.dot(o_b, wpb_ref[...], preferred_element_type=F32)
    pc = jnp.dot(oc_ref[...].astype(BF16), wpc_ref[...], preferred_element_type=F32)
    mix = gates[:, 0:d] * pa + gates[:, d:2 * d] * pb + gates[:, 2 * d:3 * d] * pc
    out = jnp.dot(mix.astype(BF16), wo_ref[...], preferred_element_type=F32)
    o_ref[...] = x + mod_ref[5:6, :] * out


def _merge(xs, mod, norm_g, o_a, o_c, bb, cu, conv_w, w_gate, b_gate, w_pa, w_pb, w_pc, w_o,
           *, layer):
    b, l, d = xs.shape
    cw = cu.shape[-1]
    sub = 8
    tok = lambda w: pl.BlockSpec((None, TM, w), lambda bi, i: (bi, i, 0))
    lay = lambda *s: _resident((None,) + s, lambda bi, i: (layer,) + (0,) * len(s))
    return pl.pallas_call(
        _merge_kernel,
        grid=(b, l // TM),
        in_specs=[tok(d),
                  pl.BlockSpec((None, None, N_MOD, d), lambda bi, i: (bi, jnp.minimum(i, 1), 0, 0)),
                  _resident((None, None, 1, d), lambda bi, i: (layer, 1, 0, 0)),
                  tok(Q_COLS), tok(Q_COLS), tok(cw), tok(cw),
                  pl.BlockSpec((None, sub, cw),
                               lambda bi, i: (bi, jnp.maximum(i * (TM // sub) - 1, 0), 0)),
                  pl.BlockSpec((None, sub, cw),
                               lambda bi, i: (bi, jnp.minimum((i + 1) * (TM // sub), l // sub - 1), 0)),
                  lay(3, cw), lay(d, 3 * d), lay(1, 3 * d), lay(Q_COLS, d), lay(cw, d),
                  lay(Q_COLS, d), lay(d, d)],
        out_specs=tok(d),
        out_shape=jax.ShapeDtypeStruct(xs.shape, F32),
        input_output_aliases={0: 0},
        compiler_params=_params(2),
        name="merge",
    )(xs, mod, norm_g, o_a, o_c, bb, cu, cu, cu, conv_w, w_gate, b_gate, w_pa, w_pb, w_pc, w_o)


def _rope_tables(s, n_ctx):
    rows = s // GRID_W
    row = jnp.repeat(jnp.arange(rows), GRID_W).astype(F32)
    col = jnp.tile(jnp.arange(GRID_W), rows).astype(F32)
    half = HEAD_DIM // 2
    inv = ROPE_THETA ** (-jnp.arange(0, half, 2, dtype=F32) / half)
    ang_r = row[:, None] * inv
    ang_c = col[:, None] * inv
    tabs = jnp.concatenate([jnp.cos(ang_r), jnp.sin(ang_r), jnp.cos(ang_c), jnp.sin(ang_c)],
                           axis=-1)
    one, zero = jnp.ones((n_ctx, HEAD_DIM // 4), F32), jnp.zeros((n_ctx, HEAD_DIM // 4), F32)
    ident = jnp.concatenate([one, zero, one, zero], axis=-1)
    return jnp.concatenate([ident, tabs], axis=0).T


def kernel(x, c, ctx, c_ctx, w_ada, b_ada, norm_g, ffn_w_gate, ffn_w_up, ffn_w_down, w_in, qk_g,
           sink_a, conv_w, w_pa, w_pb, w_pc, w_gate, b_gate, w_o):
    bsz, s, d = x.shape
    n_ctx = ctx.shape[1]
    depth = w_ada.shape[0]
    assert n_ctx == TM and s % TM == 0 and s % GRID_W == 0 and bsz < 8

    cvec = jnp.zeros((8, d), F32).at[:bsz].set(c).at[bsz].set(c_ctx)
    mods = _ada(cvec, w_ada, b_ada).reshape(depth, 8, N_MOD, d)
    lat = mods[:, :bsz]
    con = jnp.broadcast_to(mods[:, bsz:bsz + 1], lat.shape)
    mod_all = jnp.stack([con, lat], axis=2)

    rope = _rope_tables(s, n_ctx)
    gmax = jnp.max(jnp.abs(qk_g.astype(F32)), axis=-1)
    vmax = gmax * (HEAD_DIM ** 0.5) * jnp.array([Q_SCALE, 1.0, Q_SCALE, 1.0], F32)
    pow2 = jnp.exp2(jnp.clip(jnp.floor(jnp.log2(0.98 * F8_MAX / vmax)), -20.0, 20.0))
    unscale = 1.0 / (pow2[:, 0::2] * pow2[:, 1::2])
    fold = (pow2 * jnp.array([Q_SCALE, 1.0, Q_SCALE, 1.0], F32))[:, :, None]
    qkg = jnp.broadcast_to((qk_g.astype(F32) * fold).reshape(depth, 4 * HEAD_DIM, 1),
                           (depth, 4 * HEAD_DIM, TM))
    ng = norm_g.astype(F32)[:, :, None, :]
    wg, wu, wd = (w.astype(BF16) for w in (ffn_w_gate, ffn_w_up, ffn_w_down))
    w_in_b, w_gate_b, w_pa_b, w_pb_b, w_pc_b, w_o_b = (
        w.astype(BF16) for w in (w_in, w_gate, w_pa, w_pb, w_pc, w_o))
    b_gate3 = b_gate.astype(F32)[:, None, :]

    xs = jnp.concatenate([ctx, x], axis=1)
    for i in range(depth):
        mod = mod_all[i]
        xs = _ffn(xs, mod, ng, wg, wu, wd, layer=i, which=0, j0=0)
        qa8, qc8, ka8, kc8, vta, vtc, bb, cu = _inproj(xs, mod, ng, w_in_b, qkg, rope, layer=i)
        o_a = _window_attn(qa8, ka8, unscale[i, 0], vta, sink_a[i], 1.02 * vmax[i, 0] * vmax[i, 1])
        o_c = _global_attn(qc8, kc8, unscale[i, 1], vtc, 1.02 * vmax[i, 2] * vmax[i, 3])
        xs = _merge(xs, mod, ng, o_a, o_c, bb, cu, conv_w.astype(F32), w_gate_b, b_gate3,
                    w_pa_b, w_pb_b, w_pc_b, w_o_b, layer=i)
        xs = _ffn(xs, mod, ng, wg, wu, wd, layer=i, which=1, j0=6, latents_out=i + 1 == depth)
    return xs
```

```python
import functools

import jax
import jax.numpy as jnp
from jax import lax
from jax.experimental import pallas as pl
from jax.experimental.pallas import tpu as pltpu

GRID_W = 64
HEAD_DIM = 64
Q_HEADS = 8
KV_HEADS = 2
GROUP = Q_HEADS // KV_HEADS
WINDOW = 128
N_MOD = 9
ROPE_THETA = 10000.0
EPS = 1e-6
NEG_INF = -1e30
SCALE = HEAD_DIM ** -0.5
LOG2E = 1.4426950408889634
Q_SCALE = SCALE * LOG2E
SCORE_BOUND = 48.0
Q_COLS = Q_HEADS * HEAD_DIM
KV_COLS = KV_HEADS * HEAD_DIM
VT_ROWS = HEAD_DIM + 16

LANES = 128
TM = 256
FFN_TM = 512
CHUNK = 256
P_BUFS = 2
UNROLL = 6
WIN_LAG = 2
WIN_SUB = 2
VMEM_LIMIT = 56 * 1024 * 1024

F32 = jnp.float32
BF16 = jnp.bfloat16
F8 = jnp.float8_e4m3fn
F8_MAX = 448.0
F8_ROWS = 4 * HEAD_DIM


def _params(n_axes):
    return pltpu.CompilerParams(dimension_semantics=("arbitrary",) * n_axes,
                                vmem_limit_bytes=VMEM_LIMIT)


def _resident(shape, index_map):
    return pl.BlockSpec(shape, index_map, pipeline_mode=pl.Buffered(1))


def _sigmoid(v):
    return 1.0 / (1.0 + jnp.exp(-v))


def _rms_mod(x, g, shift, scale):
    y = x * lax.rsqrt(jnp.mean(x * x, axis=-1, keepdims=True) + EPS)
    return (y * g) * (1.0 + scale) + shift


def _ada_kernel(c_ref, w_ref, b_ref, o_ref):
    c = c_ref[...]
    s = (c * _sigmoid(c)).astype(BF16)
    o_ref[...] = jnp.dot(s, w_ref[...].astype(BF16), preferred_element_type=F32) + b_ref[...]


def _ada(cvec, w_ada, b_ada):
    depth, d, n = w_ada.shape
    tn = 1024
    return pl.pallas_call(
        _ada_kernel,
        grid=(depth, n // tn),
        in_specs=[pl.BlockSpec((8, d), lambda l, j: (0, 0)),
                  pl.BlockSpec((None, d, tn), lambda l, j: (l, 0, j)),
                  pl.BlockSpec((None, 1, tn), lambda l, j: (l, 0, j))],
        out_specs=pl.BlockSpec((None, 8, tn), lambda l, j: (l, 0, j)),
        out_shape=jax.ShapeDtypeStruct((depth, 8, n), F32),
        compiler_params=_params(2),
        name="ada",
    )(cvec, w_ada, b_ada.reshape(depth, 1, n))


def _ffn_kernel(x_ref, mod_ref, g_ref, wg_ref, wu_ref, wd_ref, o_ref, *, j0):
    xb = x_ref[...].reshape(x_ref.shape[-2:])
    outs = []
    for r in range(0, xb.shape[0], TM):
        x = xb[r:r + TM, :]
        h = _rms_mod(x, g_ref[...], mod_ref[j0:j0 + 1, :], mod_ref[j0 + 1:j0 + 2, :]).astype(BF16)
        a = jnp.dot(h, wg_ref[...], preferred_element_type=F32)
        u = jnp.dot(h, wu_ref[...], preferred_element_type=F32)
        act = (a * _sigmoid(a) * u).astype(BF16)
        y = jnp.dot(act, wd_ref[...], preferred_element_type=F32)
        outs.append(x + (0.5 * mod_ref[j0 + 2:j0 + 3, :]) * y)
    o_ref[...] = jnp.concatenate(outs, axis=0).reshape(o_ref.shape)


def _ffn(xs, mod, norm_g, wg, wu, wd, *, layer, which, j0, latents_out=False):
    b, l, d = xs.shape
    dff = wg.shape[-1]

    def call(xs, x_spec, n_steps, stream, out_spec=None, out_rows=None):
        return pl.pallas_call(
            functools.partial(_ffn_kernel, j0=j0),
            grid=(b, n_steps),
            in_specs=[x_spec,
                      pl.BlockSpec((None, None, N_MOD, d), lambda bi, i: (bi, stream, 0, 0)),
                      _resident((None, None, 1, d), lambda bi, i: (layer, 2 * which, 0, 0)),
                      _resident((None, None, d, dff), lambda bi, i: (layer, which, 0, 0)),
                      _resident((None, None, d, dff), lambda bi, i: (layer, which, 0, 0)),
                      _resident((None, None, dff, d), lambda bi, i: (layer, which, 0, 0))],
            out_specs=x_spec if out_spec is None else out_spec,
            out_shape=jax.ShapeDtypeStruct((b, l if out_spec is None else out_rows, d), F32),
            input_output_aliases={0: 0} if out_spec is None else {},
            compiler_params=_params(2),
            name="ffn_ctx" if stream == 0 else "ffn",
        )(xs, mod, norm_g, wg, wu, wd)

    tl = FFN_TM if (l - TM) % FFN_TM == 0 else TM
    lat_spec = pl.BlockSpec((pl.Element(1), pl.Element(tl), pl.Element(d)),
                            lambda bi, i: (bi, pl.multiple_of(TM + i * tl, TM), 0))
    if latents_out:
        return call(xs, lat_spec, (l - TM) // tl, 1,
                    pl.BlockSpec((None, tl, d), lambda bi, i: (bi, i, 0)), l - TM)
    xs = call(xs, pl.BlockSpec((None, TM, d), lambda bi, i: (bi, 0, 0)), 1, 0)
    return call(xs, lat_spec, (l - TM) // tl, 1)


def _norm_rope_t(tt, gain, rope):
    ms = jnp.sum(tt * tt, axis=0, keepdims=True) * (1.0 / HEAD_DIM)
    y = tt * lax.rsqrt(ms + EPS) * gain
    q = HEAD_DIM // 4
    cr, sr, cc, sc = (rope[j * q:(j + 1) * q, :] for j in range(4))
    y1, y2, y3, y4 = (y[j * q:(j + 1) * q, :] for j in range(4))
    return jnp.concatenate([y1 * cr - y2 * sr, y2 * cr + y1 * sr,
                            y3 * cc - y4 * sc, y4 * cc + y3 * sc], axis=0)


def _split8(y):
    hi = y.astype(F8).astype(F32)
    return hi, y - hi


def _inproj_kernel(x_ref, mod_ref, g_ref, w_ref, qkg_ref, rope_ref,
                   qa8_ref, qc8_ref, ka8_ref, kc8_ref, vta_ref, vtc_ref, b_ref, cu_ref):
    x = x_ref[...]
    h = _rms_mod(x, g_ref[...], mod_ref[3:4, :], mod_ref[4:5, :]).astype(BF16)
    z = jnp.dot(h, w_ref[...], preferred_element_type=F32)
    rope = rope_ref[...]

    def heads(lo, j):
        tt = z[:, lo:lo + LANES].T
        gain = qkg_ref[j * HEAD_DIM:(j + 1) * HEAD_DIM, :]
        return [_norm_rope_t(tt[0:HEAD_DIM, :], gain, rope),
                _norm_rope_t(tt[HEAD_DIM:, :], gain, rope)]

    zeros = jnp.zeros((HEAD_DIM, x.shape[0]), F32)

    for k8_ref, col, j in ((ka8_ref, 0, 1), (kc8_ref, 2 * KV_COLS, 3)):
        for g, y in enumerate(heads(col, j)):
            hi, lo = _split8(y)
            k8_ref[g] = jnp.concatenate([hi, hi, lo, zeros], axis=0).T.astype(F8)
    ones = jnp.ones((VT_ROWS - HEAD_DIM, x.shape[0]), BF16)
    for vt_ref, lo in ((vta_ref, KV_COLS), (vtc_ref, 3 * KV_COLS)):
        vt = z[:, lo:lo + KV_COLS].T.astype(BF16)
        for g in range(KV_HEADS):
            vt_ref[g * VT_ROWS:g * VT_ROWS + HEAD_DIM, :] = vt[g * HEAD_DIM:(g + 1) * HEAD_DIM, :]
            vt_ref[g * VT_ROWS + HEAD_DIM:(g + 1) * VT_ROWS, :] = ones
    q0 = 4 * KV_COLS
    for c in range(Q_COLS // LANES):
        for q8_ref, col, j in ((qa8_ref, q0, 0), (qc8_ref, q0 + Q_COLS, 2)):
            for e, y in enumerate(heads(col + c * LANES, j)):
                hi, lo = _split8(y)
                r0 = (2 * c + e) * F8_ROWS
                q8_ref[r0:r0 + F8_ROWS, :] = jnp.concatenate([hi, lo, hi, zeros],
                                                             axis=0).astype(F8)
    c0 = q0 + 2 * Q_COLS
    cw = b_ref.shape[-1]
    b_ref[...] = z[:, c0:c0 + cw]
    cu_ref[...] = z[:, c0 + cw:c0 + 2 * cw] * z[:, c0 + 2 * cw:c0 + 3 * cw]


def _inproj(xs, mod, norm_g, w_in, qkg, rope, *, layer):
    b, l, d = xs.shape
    ncol = w_in.shape[-1]
    cw = (ncol - 4 * KV_COLS - 2 * Q_COLS) // 3
    tok = lambda w: pl.BlockSpec((None, TM, w), lambda bi, i: (bi, i, 0))
    tr = lambda w: pl.BlockSpec((None, w, TM), lambda bi, i: (bi, 0, i))
    k8_spec = pl.BlockSpec((None, KV_HEADS, TM, F8_ROWS), lambda bi, i: (bi, 0, i, 0))
    q8_spec = pl.BlockSpec((None, None, Q_HEADS * F8_ROWS, TM), lambda bi, i: (bi, i, 0, 0))
    return pl.pallas_call(
        _inproj_kernel,
        grid=(b, l // TM),
        in_specs=[tok(d),
                  pl.BlockSpec((None, None, N_MOD, d), lambda bi, i: (bi, jnp.minimum(i, 1), 0, 0)),
                  _resident((None, None, 1, d), lambda bi, i: (layer, 1, 0, 0)),
                  _resident((None, d, ncol), lambda bi, i: (layer, 0, 0)),
                  _resident((None, 4 * HEAD_DIM, TM), lambda bi, i: (layer, 0, 0)),
                  pl.BlockSpec((HEAD_DIM, TM), lambda bi, i: (0, i))],
        out_specs=[q8_spec, q8_spec, k8_spec, k8_spec,
                   tr(KV_HEADS * VT_ROWS), tr(KV_HEADS * VT_ROWS), tok(cw), tok(cw)],
        out_shape=[jax.ShapeDtypeStruct((b, l // TM, Q_HEADS * F8_ROWS, TM), F8),
                   jax.ShapeDtypeStruct((b, l // TM, Q_HEADS * F8_ROWS, TM), F8),
                   jax.ShapeDtypeStruct((b, KV_HEADS, l, F8_ROWS), F8),
                   jax.ShapeDtypeStruct((b, KV_HEADS, l, F8_ROWS), F8),
                   jax.ShapeDtypeStruct((b, KV_HEADS * VT_ROWS, l), BF16),
                   jax.ShapeDtypeStruct((b, KV_HEADS * VT_ROWS, l), BF16),
                   jax.ShapeDtypeStruct((b, l, cw), F32),
                   jax.ShapeDtypeStruct((b, l, cw), F32)],
        compiler_params=_params(2),
        name="inproj",
    )(xs, mod, norm_g, w_in, qkg, rope)


def _score_chunk(q8_ref, k8_ref, un_ref, start, ch, h):
    s = jnp.dot(k8_ref[h // GROUP, pl.ds(start, ch), :], q8_ref[h * F8_ROWS:(h + 1) * F8_ROWS, :],
                preferred_element_type=F32)
    return s * un_ref[...]


def _attend(q8_ref, k8_ref, un_ref, vt_ref, acc_ref, m_ref, al_ref, smax_ref, s_bufs, p_bufs, *,
            n_keys, tk):
    n_kt = n_keys // tk
    arows = lambda h: slice(h * VT_ROWS, (h + 1) * VT_ROWS)
    m_ref[...] = jnp.full(m_ref.shape, NEG_INF, F32)
    acc_ref[...] = jnp.zeros(acc_ref.shape, F32)

    ch = min(tk, CHUNK)
    n_ch = tk // ch

    def key_start(t, c):
        return t * tk + c * ch if isinstance(t, int) else pl.multiple_of(t * tk + c * ch, ch)

    def step(sc=None, sm=None, va=None):
        if sm is not None:
            m = m_ref[sm:sm + 1, :]
            m_new = jnp.maximum(m, smax_ref[sm:sm + 1, :])
            m_ref[sm:sm + 1, :] = m_new
            al_ref[sm:sm + 1, :] = jnp.exp2(m - m_new)
        smax = pv = None
        for c in range(n_ch):
            rows = slice(c * ch, (c + 1) * ch)
            if sc is not None:
                t, h = sc
                s = _score_chunk(q8_ref, k8_ref, un_ref, key_start(t, c), ch, h)
                s_bufs[h % 2][rows, :] = s
                cmax = jnp.max(s, axis=0, keepdims=True)
                smax = cmax if smax is None else jnp.maximum(smax, cmax)
            if sm is not None:
                p_bufs[sm % 2][rows, :] = jnp.exp2(s_bufs[sm % 2][rows, :] - m_new).astype(BF16)
            if va is not None:
                t, h = va
                g = h // GROUP
                vt = vt_ref[g * VT_ROWS:(g + 1) * VT_ROWS, pl.ds(key_start(t, c), ch)]
                d = jnp.dot(vt, p_bufs[h % 2][rows, :], preferred_element_type=F32)
                pv = d if pv is None else pv + d
        if sc is not None:
            smax_ref[sc[1]:sc[1] + 1, :] = smax
        if va is not None:
            h = va[1]
            acc_ref[arows(h), :] = al_ref[h:h + 1, :] * acc_ref[arows(h), :] + pv

    def tile_steps(t, last):
        for h in range(Q_HEADS):
            if h + 2 < Q_HEADS:
                sc = (t, h + 2)
            else:
                sc = None if last else (t + 1, h + 2 - Q_HEADS)
            if h + 1 < Q_HEADS:
                sm = h + 1
            else:
                sm = None if last else 0
            step(sc, sm, (t, h))

    step(sc=(0, 0))
    step(sc=(0, 1), sm=0)
    if n_kt > 1:
        def body(t, carry):
            tile_steps(t, False)
            return carry
        lax.fori_loop(0, n_kt - 1, body, 0)
    tile_steps(n_kt - 1, True)


def _attend_bounded(q8_ref, k8_ref, un_ref, vt_ref, acc_ref, p_bufs, *, n_keys, tk):
    n_kt = n_keys // tk
    n_buf = len(p_bufs)
    arows = lambda h: slice(h * VT_ROWS, (h + 1) * VT_ROWS)
    acc_ref[...] = jnp.zeros(acc_ref.shape, F32)
    ch = min(tk, CHUNK // 2)
    n_ch = tk // ch

    def key_start(t, c):
        return t * tk + c * ch if isinstance(t, int) else pl.multiple_of(t * tk + c * ch, ch)

    def step(sc=None, va=None):
        if va is not None:
            t, h = va
            g = h // GROUP
            vt = vt_ref[g * VT_ROWS:(g + 1) * VT_ROWS, pl.ds(key_start(t, 0), tk)]
            acc_ref[arows(h), :] = acc_ref[arows(h), :] + jnp.dot(
                vt, p_bufs[h % n_buf][0:tk, :], preferred_element_type=F32)
        if sc is not None:
            t, h = sc
            for c in range(n_ch):
                s = _score_chunk(q8_ref, k8_ref, un_ref, key_start(t, c), ch, h)
                p_bufs[h % n_buf][c * ch:(c + 1) * ch, :] = jnp.exp2(s).astype(BF16)

    def tile_steps(t, last):
        for h in range(Q_HEADS):
            if h + 1 < Q_HEADS:
                sc = (t, h + 1)
            else:
                sc = None if last else (t + 1, 0)
            step(sc, (t, h))

    step(sc=(0, 0))
    n_trips, n_rest = divmod(n_kt - 1, UNROLL)
    if n_trips > 0:
        def body(t, carry):
            for u in range(UNROLL):
                tile_steps(t * UNROLL + u, False)
            return carry
        lax.fori_loop(0, n_trips, body, 0)
    for u in range(n_rest):
        tile_steps(n_trips * UNROLL + u, False)
    tile_steps(n_kt - 1, True)


def _global_attn_kernel(q8_ref, k8_ref, un_ref, vt_ref, o_ref, acc_ref, *scratch, tk, bounded):
    i = pl.program_id(1)
    if bounded:
        attend = functools.partial(_attend_bounded, q8_ref, k8_ref, un_ref, vt_ref, acc_ref, scratch)
    else:
        m_ref, al_ref, smax_ref, s0_ref, s1_ref, p0_ref, p1_ref = scratch
        attend = functools.partial(_attend, q8_ref, k8_ref, un_ref, vt_ref, acc_ref, m_ref, al_ref,
                                   smax_ref, (s0_ref, s1_ref), (p0_ref, p1_ref))

    @pl.when(i == 0)
    def _():
        attend(n_keys=TM, tk=TM)

    @pl.when(i > 0)
    def _():
        attend(n_keys=k8_ref.shape[1], tk=tk)

    outs = []
    for h in range(Q_HEADS):
        a = acc_ref[h * VT_ROWS:(h + 1) * VT_ROWS, :]
        outs.append(a[0:HEAD_DIM, :] / a[HEAD_DIM:HEAD_DIM + 1, :])
    o_ref[...] = jnp.concatenate(outs, axis=0).T


def _key_tile(l):
    return next(t for t in (1280, 1024, 768, 512, 256) if l % t == 0)


def _global_attn(q8, k8, unscale, vt, score_bound):
    un = jnp.broadcast_to(unscale.astype(F32), (1, TM))
    return lax.cond(score_bound <= SCORE_BOUND,
                    functools.partial(_global_attn_call, bounded=True),
                    functools.partial(_global_attn_call, bounded=False), q8, k8, un, vt)


def _global_attn_call(q8, k8, un, vt, *, bounded):
    b, _, l, _ = k8.shape
    tk = _key_tile(l)
    stat = pltpu.VMEM((Q_HEADS, TM), F32)
    p_buf = pltpu.VMEM((tk, TM), BF16)
    s_buf = pltpu.VMEM((tk, TM), F32)
    scratch = [p_buf] * P_BUFS if bounded else [stat, stat, stat, s_buf, s_buf, p_buf, p_buf]
    return pl.pallas_call(
        functools.partial(_global_attn_kernel, tk=tk, bounded=bounded),
        grid=(b, l // TM),
        in_specs=[pl.BlockSpec((None, None, Q_HEADS * F8_ROWS, TM), lambda bi, i: (bi, i, 0, 0)),
                  pl.BlockSpec((None, KV_HEADS, l, F8_ROWS), lambda bi, i: (bi, 0, 0, 0)),
                  pl.BlockSpec((1, TM), lambda bi, i: (0, 0)),
                  pl.BlockSpec((None, KV_HEADS * VT_ROWS, l), lambda bi, i: (bi, 0, 0))],
        out_specs=pl.BlockSpec((None, TM, Q_COLS), lambda bi, i: (bi, i, 0)),
        out_shape=jax.ShapeDtypeStruct((b, l, Q_COLS), F32),
        scratch_shapes=[pltpu.VMEM((Q_HEADS * VT_ROWS, TM), F32)] + scratch,
        compiler_params=_params(2),
        name="global_attn_bounded" if bounded else "global_attn",
    )(q8, k8, un, vt)


def _window_geometry(i, tq, l_all):
    span = tq + 2 * WINDOW
    start = pl.multiple_of(jnp.clip(i * tq - WINDOW, 0, l_all - span), LANES)
    kpos = start - TM + lax.broadcasted_iota(jnp.int32, (span, 1), 0)
    qbase = jnp.where(i >= 1, i * tq - TM, -(1 << 20))
    qpos = qbase + lax.broadcasted_iota(jnp.int32, (1, tq), 1)
    ok = jnp.logical_and(kpos >= 0, jnp.abs(qpos - kpos) <= WINDOW)
    return start, jnp.where(ok, 0.0, NEG_INF).astype(F32)


def _window_attn_kernel(q8_ref, k8_ref, un_ref, vt_ref, sink_ref, o_ref, acc_ref):
    tq = q8_ref.shape[-1]
    span = tq + 2 * WINDOW
    start, bias = _window_geometry(pl.program_id(1), tq, k8_ref.shape[1])
    for h in range(Q_HEADS):
        g = h // GROUP
        s1 = _score_chunk(q8_ref, k8_ref, un_ref, 0, TM, h)
        s2 = _score_chunk(q8_ref, k8_ref, un_ref, start, span, h) + bias
        sink = sink_ref[h:h + 1, :]
        m = jnp.maximum(jnp.maximum(jnp.max(s1, axis=0, keepdims=True),
                                    jnp.max(s2, axis=0, keepdims=True)), sink)
        p1 = jnp.exp2(s1 - m)
        p2 = jnp.exp2(s2 - m)
        l = (jnp.sum(p1, axis=0, keepdims=True) + jnp.sum(p2, axis=0, keepdims=True)
             + jnp.exp2(sink - m))
        rows = slice(g * VT_ROWS, g * VT_ROWS + HEAD_DIM)
        acc = (jnp.dot(vt_ref[rows, 0:TM], p1.astype(BF16), preferred_element_type=F32)
               + jnp.dot(vt_ref[rows, pl.ds(start, span)], p2.astype(BF16),
                         preferred_element_type=F32))
        acc_ref[h * HEAD_DIM:(h + 1) * HEAD_DIM, :] = acc / l
    o_ref[...] = acc_ref[...].T


def _window_attn_bounded_kernel(*refs):
    q_refs = refs[:WIN_SUB]
    k8_ref, un_ref, vt_ref, sink_ref, o_ref, bias_ref, acc_ref = refs[WIN_SUB:WIN_SUB + 7]
    p_bufs = refs[WIN_SUB + 7:]
    tq = q_refs[0].shape[-1]
    l_all = k8_ref.shape[1]
    n_buf = len(p_bufs)
    starts = []
    for e in range(WIN_SUB):
        t = jnp.minimum(WIN_SUB * pl.program_id(1) + e, l_all // tq - 1)
        start, bias = _window_geometry(t, tq, l_all)
        bias_ref[e] = bias
        starts.append(start)
    n_win = bias_ref.shape[1] // CHUNK
    items = [(e, h) for h in range(Q_HEADS) for e in range(WIN_SUB)]

    def chunks(e):
        return [(0, None)] + [(pl.multiple_of(starts[e] + c * CHUNK, LANES), c)
                              for c in range(n_win)]

    def step(sc=None, va=None):
        if va is not None:
            e, h = items[va]
            g = h // GROUP
            pv = None
            for ci, (k0, _) in enumerate(chunks(e)):
                vt = vt_ref[g * VT_ROWS:(g + 1) * VT_ROWS, pl.ds(k0, CHUNK)]
                d = jnp.dot(vt, p_bufs[va % n_buf][ci * CHUNK:(ci + 1) * CHUNK, :],
                            preferred_element_type=F32)
                pv = d if pv is None else pv + d
            acc_ref[va * VT_ROWS:(va + 1) * VT_ROWS, :] = pv
        if sc is not None:
            e, h = items[sc]
            for ci, (k0, c) in enumerate(chunks(e)):
                s = _score_chunk(q_refs[e], k8_ref, un_ref, k0, CHUNK, h)
                if c is not None:
                    s = s + bias_ref[e, c * CHUNK:(c + 1) * CHUNK, :]
                p_bufs[sc % n_buf][ci * CHUNK:(ci + 1) * CHUNK, :] = jnp.exp2(s).astype(BF16)

    for n in range(WIN_LAG):
        step(sc=n)
    for n in range(len(items)):
        step(sc=n + WIN_LAG if n + WIN_LAG < len(items) else None, va=n)
    for e in range(WIN_SUB):
        outs = []
        for h in range(Q_HEADS):
            n = items.index((e, h))
            a = acc_ref[n * VT_ROWS:(n + 1) * VT_ROWS, :]
            l = a[HEAD_DIM:HEAD_DIM + 1, :] + jnp.exp2(sink_ref[h:h + 1, :])
            outs.append(a[0:HEAD_DIM, :] / l)
        o_ref[e * tq:(e + 1) * tq, :] = jnp.concatenate(outs, axis=0).T


def _window_attn(q8, k8, unscale, vt, sink, score_bound):
    sink2 = sink.astype(F32) * LOG2E
    ok = jnp.logical_and(score_bound <= SCORE_BOUND, jnp.max(jnp.abs(sink2)) <= SCORE_BOUND)
    sink_b = jnp.broadcast_to(sink2[:, None], (Q_HEADS, TM))
    un = jnp.broadcast_to(unscale.astype(F32), (1, TM))
    return lax.cond(ok, functools.partial(_window_attn_call, bounded=True),
                    functools.partial(_window_attn_call, bounded=False), q8, k8, un, vt, sink_b)


def _window_attn_call(q8, k8, un, vt, sink_b, *, bounded):
    b, _, l, _ = k8.shape
    n_tiles = l // TM
    span = TM + 2 * WINDOW
    n_sub = WIN_SUB if bounded else 1
    q_spec = lambda e: pl.BlockSpec(
        (None, None, Q_HEADS * F8_ROWS, TM),
        lambda bi, i: (bi, jnp.minimum(n_sub * i + e, n_tiles - 1), 0, 0))
    if bounded:
        scratch = [pltpu.VMEM((WIN_SUB, span, TM), F32),
                   pltpu.VMEM((WIN_SUB * Q_HEADS * VT_ROWS, TM), F32)]
        scratch += [pltpu.VMEM((TM + span, TM), BF16)] * (2 * WIN_LAG)
    else:
        scratch = [pltpu.VMEM((Q_COLS, TM), F32)]
    return pl.pallas_call(
        _window_attn_bounded_kernel if bounded else _window_attn_kernel,
        grid=(b, pl.cdiv(n_tiles, n_sub)),
        in_specs=[q_spec(e) for e in range(n_sub)] + [
            pl.BlockSpec((None, KV_HEADS, l, F8_ROWS), lambda bi, i: (bi, 0, 0, 0)),
            pl.BlockSpec((1, TM), lambda bi, i: (0, 0)),
            pl.BlockSpec((None, KV_HEADS * VT_ROWS, l), lambda bi, i: (bi, 0, 0)),
            pl.BlockSpec((Q_HEADS, TM), lambda bi, i: (0, 0))],
        out_specs=pl.BlockSpec((None, n_sub * TM, Q_COLS), lambda bi, i: (bi, i, 0)),
        out_shape=jax.ShapeDtypeStruct((b, l, Q_COLS), F32),
        scratch_shapes=scratch,
        compiler_params=_params(2),
        name="window_attn_bounded" if bounded else "window_attn",
    )(*([q8] * n_sub), k8, un, vt, sink_b)


def _merge_kernel(x_ref, mod_ref, g_ref, oa_ref, oc_ref, b_ref, cu_ref, cup_ref, cun_ref, cw_ref,
                  wgate_ref, bgate_ref, wpa_ref, wpb_ref, wpc_ref, wo_ref, o_ref):
    i = pl.program_id(1)
    n_tiles = pl.num_programs(1)
    x = x_ref[...]
    tm, d = x.shape
    h = _rms_mod(x, g_ref[...], mod_ref[3:4, :], mod_ref[4:5, :]).astype(BF16)
    gates = _sigmoid(jnp.dot(h, wgate_ref[...], preferred_element_type=F32) + bgate_ref[...])

    cu = cu_ref[...]
    has_prev = (i >= 2).astype(F32)
    has_next = jnp.logical_and(i >= 1, i < n_tiles - 1).astype(F32)
    prev_row = cup_ref[7:8, :] * has_prev
    next_row = cun_ref[0:1, :] * has_next
    row = lax.broadcasted_iota(jnp.int32, (tm, 1), 0)
    cu_dn = jnp.where(row == 0, prev_row, pltpu.roll(cu, 1, 0))
    cu_up = jnp.where(row == tm - 1, next_row, pltpu.roll(cu, tm - 1, 0))
    y = cw_ref[0:1, :] * cu_dn + cw_ref[1:2, :] * cu + cw_ref[2:3, :] * cu_up
    o_b = (b_ref[...] * y).astype(BF16)

    pa = jnp.dot(oa_ref[...].astype(BF16), wpa_ref[...], preferred_element_type=F32)
    pb = jnp.dot(o_b, wpb_ref[...], preferred_element_type=F32)
    pc = jnp.dot(oc_ref[...].astype(BF16), wpc_ref[...], preferred_element_type=F32)
    mix = gates[:, 0:d] * pa + gates[:, d:2 * d] * pb + gates[:, 2 * d:3 * d] * pc
    out = jnp.dot(mix.astype(BF16), wo_ref[...], preferred_element_type=F32)
    o_ref[...] = x + mod_ref[5:6, :] * out


def _merge(xs, mod, norm_g, o_a, o_c, bb, cu, conv_w, w_gate, b_gate, w_pa, w_pb, w_pc, w_o,
           *, layer):
    b, l, d = xs.shape
    cw = cu.shape[-1]
    sub = 8
    tok = lambda w: pl.BlockSpec((None, TM, w), lambda bi, i: (bi, i, 0))
    lay = lambda *s: _resident((None,) + s, lambda bi, i: (layer,) + (0,) * len(s))
    return pl.pallas_call(
        _merge_kernel,
        grid=(b, l // TM),
        in_specs=[tok(d),
                  pl.BlockSpec((None, None, N_MOD, d), lambda bi, i: (bi, jnp.minimum(i, 1), 0, 0)),
                  _resident((None, None, 1, d), lambda bi, i: (layer, 1, 0, 0)),
                  tok(Q_COLS), tok(Q_COLS), tok(cw), tok(cw),
                  pl.BlockSpec((None, sub, cw),
                               lambda bi, i: (bi, jnp.maximum(i * (TM // sub) - 1, 0), 0)),
                  pl.BlockSpec((None, sub, cw),
                               lambda bi, i: (bi, jnp.minimum((i + 1) * (TM // sub), l // sub - 1), 0)),
                  lay(3, cw), lay(d, 3 * d), lay(1, 3 * d), lay(Q_COLS, d), lay(cw, d),
                  lay(Q_COLS, d), lay(d, d)],
        out_specs=tok(d),
        out_shape=jax.ShapeDtypeStruct(xs.shape, F32),
        input_output_aliases={0: 0},
        compiler_params=_params(2),
        name="merge",
    )(xs, mod, norm_g, o_a, o_c, bb, cu, cu, cu, conv_w, w_gate, b_gate, w_pa, w_pb, w_pc, w_o)


def _rope_tables(s, n_ctx):
    rows = s // GRID_W
    row = jnp.repeat(jnp.arange(rows), GRID_W).astype(F32)
    col = jnp.tile(jnp.arange(GRID_W), rows).astype(F32)
    half = HEAD_DIM // 2
    inv = ROPE_THETA ** (-jnp.arange(0, half, 2, dtype=F32) / half)
    ang_r = row[:, None] * inv
    ang_c = col[:, None] * inv
    tabs = jnp.concatenate([jnp.cos(ang_r), jnp.sin(ang_r), jnp.cos(ang_c), jnp.sin(ang_c)],
                           axis=-1)
    one, zero = jnp.ones((n_ctx, HEAD_DIM // 4), F32), jnp.zeros((n_ctx, HEAD_DIM // 4), F32)
    ident = jnp.concatenate([one, zero, one, zero], axis=-1)
    return jnp.concatenate([ident, tabs], axis=0).T


def kernel(x, c, ctx, c_ctx, w_ada, b_ada, norm_g, ffn_w_gate, ffn_w_up, ffn_w_down, w_in, qk_g,
           sink_a, conv_w, w_pa, w_pb, w_pc, w_gate, b_gate, w_o):
    bsz, s, d = x.shape
    n_ctx = ctx.shape[1]
    depth = w_ada.shape[0]
    assert n_ctx == TM and s % TM == 0 and s % GRID_W == 0 and bsz < 8

    cvec = jnp.zeros((8, d), F32).at[:bsz].set(c).at[bsz].set(c_ctx)
    mods = _ada(cvec, w_ada, b_ada).reshape(depth, 8, N_MOD, d)
    lat = mods[:, :bsz]
    con = jnp.broadcast_to(mods[:, bsz:bsz + 1], lat.shape)
    mod_all = jnp.stack([con, lat], axis=2)

    rope = _rope_tables(s, n_ctx)
    gmax = jnp.max(jnp.abs(qk_g.astype(F32)), axis=-1)
    vmax = gmax * (HEAD_DIM ** 0.5) * jnp.array([Q_SCALE, 1.0, Q_SCALE, 1.0], F32)
    pow2 = jnp.exp2(jnp.clip(jnp.floor(jnp.log2(0.98 * F8_MAX / vmax)), -20.0, 20.0))
    unscale = 1.0 / (pow2[:, 0::2] * pow2[:, 1::2])
    fold = (pow2 * jnp.array([Q_SCALE, 1.0, Q_SCALE, 1.0], F32))[:, :, None]
    qkg = jnp.broadcast_to((qk_g.astype(F32) * fold).reshape(depth, 4 * HEAD_DIM, 1),
                           (depth, 4 * HEAD_DIM, TM))
    ng = norm_g.astype(F32)[:, :, None, :]
    wg, wu, wd = (w.astype(BF16) for w in (ffn_w_gate, ffn_w_up, ffn_w_down))
    w_in_b, w_gate_b, w_pa_b, w_pb_b, w_pc_b, w_o_b = (
        w.astype(BF16) for w in (w_in, w_gate, w_pa, w_pb, w_pc, w_o))
    b_gate3 = b_gate.astype(F32)[:, None, :]

    xs = jnp.concatenate([ctx, x], axis=1)
    for i in range(depth):
        mod = mod_all[i]
        xs = _ffn(xs, mod, ng, wg, wu, wd, layer=i, which=0, j0=0)
        qa8, qc8, ka8, kc8, vta, vtc, bb, cu = _inproj(xs, mod, ng, w_in_b, qkg, rope, layer=i)
        o_a = _window_attn(qa8, ka8, unscale[i, 0], vta, sink_a[i], 1.02 * vmax[i, 0] * vmax[i, 1])
        o_c = _global_attn(qc8, kc8, unscale[i, 1], vtc, 1.02 * vmax[i, 2] * vmax[i, 3])
        xs = _merge(xs, mod, ng, o_a, o_c, bb, cu, conv_w.astype(F32), w_gate_b, b_gate3,
                    w_pa_b, w_pb_b, w_pc_b, w_o_b, layer=i)
        xs = _ffn(xs, mod, ng, wg, wu, wd, layer=i, which=1, j0=6, latents_out=i + 1 == depth)
    return xs
```

```python
import functools

import jax
import jax.numpy as jnp
from jax import lax
from jax.experimental import pallas as pl
from jax.experimental.pallas import tpu as pltpu

GRID_W = 64
HEAD_DIM = 64
Q_HEADS = 8
KV_HEADS = 2
GROUP = Q_HEADS // KV_HEADS
WINDOW = 128
N_MOD = 9
ROPE_THETA = 10000.0
EPS = 1e-6
NEG_INF = -1e30
SCALE = HEAD_DIM ** -0.5
LOG2E = 1.4426950408889634
Q_SCALE = SCALE * LOG2E
SCORE_BOUND = 48.0
Q_COLS = Q_HEADS * HEAD_DIM
KV_COLS = KV_HEADS * HEAD_DIM
VT_ROWS = HEAD_DIM + 16

LANES = 128
TM = 256
FFN_TM = 512
CHUNK = 256
P_BUFS = 2
UNROLL = 6
WIN_LAG = 2
WIN_SUB = 2
VMEM_LIMIT = 56 * 1024 * 1024

F32 = jnp.float32
BF16 = jnp.bfloat16
F8 = jnp.float8_e4m3fn
F8_MAX = 448.0
F8_ROWS = 4 * HEAD_DIM


def _params(n_axes):
    return pltpu.CompilerParams(dimension_semantics=("arbitrary",) * n_axes,
                                vmem_limit_bytes=VMEM_LIMIT)


def _resident(shape, index_map):
    return pl.BlockSpec(shape, index_map, pipeline_mode=pl.Buffered(1))


def _sigmoid(v):
    return 1.0 / (1.0 + jnp.exp(-v))


def _rms_mod(x, g, shift, scale):
    y = x * lax.rsqrt(jnp.mean(x * x, axis=-1, keepdims=True) + EPS)
    return (y * g) * (1.0 + scale) + shift


def _ada_kernel(c_ref, w_ref, b_ref, o_ref):
    c = c_ref[...]
    s = (c * _sigmoid(c)).astype(BF16)
    o_ref[...] = jnp.dot(s, w_ref[...].astype(BF16), preferred_element_type=F32) + b_ref[...]


def _ada(cvec, w_ada, b_ada):
    depth, d, n = w_ada.shape
    tn = 1024
    return pl.pallas_call(
        _ada_kernel,
        grid=(depth, n // tn),
        in_specs=[pl.BlockSpec((8, d), lambda l, j: (0, 0)),
                  pl.BlockSpec((None, d, tn), lambda l, j: (l, 0, j)),
                  pl.BlockSpec((None, 1, tn), lambda l, j: (l, 0, j))],
        out_specs=pl.BlockSpec((None, 8, tn), lambda l, j: (l, 0, j)),
        out_shape=jax.ShapeDtypeStruct((depth, 8, n), F32),
        compiler_params=_params(2),
        name="ada",
    )(cvec, w_ada, b_ada.reshape(depth, 1, n))


def _ffn_kernel(x_ref, mod_ref, g_ref, wg_ref, wu_ref, wd_ref, o_ref, *, j0):
    xb = x_ref[...].reshape(x_ref.shape[-2:])
    outs = []
    for r in range(0, xb.shape[0], FFN_TM):
        x = xb[r:r + FFN_TM, :]
        h = _rms_mod(x, g_ref[...], mod_ref[j0:j0 + 1, :], mod_ref[j0 + 1:j0 + 2, :]).astype(BF16)
        a = jnp.dot(h, wg_ref[...], preferred_element_type=F32)
        u = jnp.dot(h, wu_ref[...], preferred_element_type=F32)
        act = (a * _sigmoid(a) * u).astype(BF16)
        y = jnp.dot(act, wd_ref[...], preferred_element_type=F32)
        outs.append(x + (0.5 * mod_ref[j0 + 2:j0 + 3, :]) * y)
    o_ref[...] = jnp.concatenate(outs, axis=0).reshape(o_ref.shape)


def _ffn(xs, mod, norm_g, wg, wu, wd, *, layer, which, j0, latents_out=False):
    b, l, d = xs.shape
    dff = wg.shape[-1]

    def call(xs, x_spec, n_steps, stream, out_spec=None, out_rows=None):
        return pl.pallas_call(
            functools.partial(_ffn_kernel, j0=j0),
            grid=(b, n_steps),
            in_specs=[x_spec,
                      pl.BlockSpec((None, None, N_MOD, d), lambda bi, i: (bi, stream, 0, 0)),
                      _resident((None, None, 1, d), lambda bi, i: (layer, 2 * which, 0, 0)),
                      _resident((None, None, d, dff), lambda bi, i: (layer, which, 0, 0)),
                      _resident((None, None, d, dff), lambda bi, i: (layer, which, 0, 0)),
                      _resident((None, None, dff, d), lambda bi, i: (layer, which, 0, 0))],
            out_specs=x_spec if out_spec is None else out_spec,
            out_shape=jax.ShapeDtypeStruct((b, l if out_spec is None else out_rows, d), F32),
            input_output_aliases={0: 0} if out_spec is None else {},
            compiler_params=_params(2),
            name="ffn_ctx" if stream == 0 else "ffn",
        )(xs, mod, norm_g, wg, wu, wd)

    tl = FFN_TM if (l - TM) % FFN_TM == 0 else TM
    lat_spec = pl.BlockSpec((pl.Element(1), pl.Element(tl), pl.Element(d)),
                            lambda bi, i: (bi, pl.multiple_of(TM + i * tl, TM), 0))
    if latents_out:
        return call(xs, lat_spec, (l - TM) // tl, 1,
                    pl.BlockSpec((None, tl, d), lambda bi, i: (bi, i, 0)), l - TM)
    xs = call(xs, pl.BlockSpec((None, TM, d), lambda bi, i: (bi, 0, 0)), 1, 0)
    return call(xs, lat_spec, (l - TM) // tl, 1)


def _norm_rope_t(tt, gain, rope):
    ms = jnp.sum(tt * tt, axis=0, keepdims=True) * (1.0 / HEAD_DIM)
    y = tt * lax.rsqrt(ms + EPS) * gain
    q = HEAD_DIM // 4
    cr, sr, cc, sc = (rope[j * q:(j + 1) * q, :] for j in range(4))
    y1, y2, y3, y4 = (y[j * q:(j + 1) * q, :] for j in range(4))
    return jnp.concatenate([y1 * cr - y2 * sr, y2 * cr + y1 * sr,
                            y3 * cc - y4 * sc, y4 * cc + y3 * sc], axis=0)


def _split8(y):
    hi = y.astype(F8).astype(F32)
    return hi, y - hi


def _inproj_kernel(x_ref, mod_ref, g_ref, w_ref, qkg_ref, rope_ref,
                   qa8_ref, qc8_ref, ka8_ref, kc8_ref, vta_ref, vtc_ref, b_ref, cu_ref):
    x = x_ref[...]
    h = _rms_mod(x, g_ref[...], mod_ref[3:4, :], mod_ref[4:5, :]).astype(BF16)
    z = jnp.dot(h, w_ref[...], preferred_element_type=F32)
    rope = rope_ref[...]

    def heads(lo, j):
        tt = z[:, lo:lo + LANES].T
        gain = qkg_ref[j * HEAD_DIM:(j + 1) * HEAD_DIM, :]
        return [_norm_rope_t(tt[0:HEAD_DIM, :], gain, rope),
                _norm_rope_t(tt[HEAD_DIM:, :], gain, rope)]

    zeros = jnp.zeros((HEAD_DIM, x.shape[0]), F32)

    for k8_ref, col, j in ((ka8_ref, 0, 1), (kc8_ref, 2 * KV_COLS, 3)):
        for g, y in enumerate(heads(col, j)):
            hi, lo = _split8(y)
            k8_ref[g] = jnp.concatenate([hi, hi, lo, zeros], axis=0).T.astype(F8)
    ones = jnp.ones((VT_ROWS - HEAD_DIM, x.shape[0]), BF16)
    for vt_ref, lo in ((vta_ref, KV_COLS), (vtc_ref, 3 * KV_COLS)):
        vt = z[:, lo:lo + KV_COLS].T.astype(BF16)
        for g in range(KV_HEADS):
            vt_ref[g * VT_ROWS:g * VT_ROWS + HEAD_DIM, :] = vt[g * HEAD_DIM:(g + 1) * HEAD_DIM, :]
            vt_ref[g * VT_ROWS + HEAD_DIM:(g + 1) * VT_ROWS, :] = ones
    q0 = 4 * KV_COLS
    for c in range(Q_COLS // LANES):
        for q8_ref, col, j in ((qa8_ref, q0, 0), (qc8_ref, q0 + Q_COLS, 2)):
            for e, y in enumerate(heads(col + c * LANES, j)):
                hi, lo = _split8(y)
                r0 = (2 * c + e) * F8_ROWS
                q8_ref[r0:r0 + F8_ROWS, :] = jnp.concatenate([hi, lo, hi, zeros],
                                                             axis=0).astype(F8)
    c0 = q0 + 2 * Q_COLS
    cw = b_ref.shape[-1]
    b_ref[...] = z[:, c0:c0 + cw]
    cu_ref[...] = z[:, c0 + cw:c0 + 2 * cw] * z[:, c0 + 2 * cw:c0 + 3 * cw]


def _inproj(xs, mod, norm_g, w_in, qkg, rope, *, layer):
    b, l, d = xs.shape
    ncol = w_in.shape[-1]
    cw = (ncol - 4 * KV_COLS - 2 * Q_COLS) // 3
    tok = lambda w: pl.BlockSpec((None, TM, w), lambda bi, i: (bi, i, 0))
    tr = lambda w: pl.BlockSpec((None, w, TM), lambda bi, i: (bi, 0, i))
    k8_spec = pl.BlockSpec((None, KV_HEADS, TM, F8_ROWS), lambda bi, i: (bi, 0, i, 0))
    q8_spec = pl.BlockSpec((None, None, Q_HEADS * F8_ROWS, TM), lambda bi, i: (bi, i, 0, 0))
    return pl.pallas_call(
        _inproj_kernel,
        grid=(b, l // TM),
        in_specs=[tok(d),
                  pl.BlockSpec((None, None, N_MOD, d), lambda bi, i: (bi, jnp.minimum(i, 1), 0, 0)),
                  _resident((None, None, 1, d), lambda bi, i: (layer, 1, 0, 0)),
                  _resident((None, d, ncol), lambda bi, i: (layer, 0, 0)),
                  _resident((None, 4 * HEAD_DIM, TM), lambda bi, i: (layer, 0, 0)),
                  pl.BlockSpec((HEAD_DIM, TM), lambda bi, i: (0, i))],
        out_specs=[q8_spec, q8_spec, k8_spec, k8_spec,
                   tr(KV_HEADS * VT_ROWS), tr(KV_HEADS * VT_ROWS), tok(cw), tok(cw)],
        out_shape=[jax.ShapeDtypeStruct((b, l // TM, Q_HEADS * F8_ROWS, TM), F8),
                   jax.ShapeDtypeStruct((b, l // TM, Q_HEADS * F8_ROWS, TM), F8),
                   jax.ShapeDtypeStruct((b, KV_HEADS, l, F8_ROWS), F8),
                   jax.ShapeDtypeStruct((b, KV_HEADS, l, F8_ROWS), F8),
                   jax.ShapeDtypeStruct((b, KV_HEADS * VT_ROWS, l), BF16),
                   jax.ShapeDtypeStruct((b, KV_HEADS * VT_ROWS, l), BF16),
                   jax.ShapeDtypeStruct((b, l, cw), F32),
                   jax.ShapeDtypeStruct((b, l, cw), F32)],
        compiler_params=_params(2),
        name="inproj",
    )(xs, mod, norm_g, w_in, qkg, rope)


def _score_chunk(q8_ref, k8_ref, un_ref, start, ch, h):
    s = jnp.dot(k8_ref[h // GROUP, pl.ds(start, ch), :], q8_ref[h * F8_ROWS:(h + 1) * F8_ROWS, :],
                preferred_element_type=F32)
    return s * un_ref[...]


def _attend(q8_ref, k8_ref, un_ref, vt_ref, acc_ref, m_ref, al_ref, smax_ref, s_bufs, p_bufs, *,
            n_keys, tk):
    n_kt = n_keys // tk
    arows = lambda h: slice(h * VT_ROWS, (h + 1) * VT_ROWS)
    m_ref[...] = jnp.full(m_ref.shape, NEG_INF, F32)
    acc_ref[...] = jnp.zeros(acc_ref.shape, F32)

    ch = min(tk, CHUNK)
    n_ch = tk // ch

    def key_start(t, c):
        return t * tk + c * ch if isinstance(t, int) else pl.multiple_of(t * tk + c * ch, ch)

    def step(sc=None, sm=None, va=None):
        if sm is not None:
            m = m_ref[sm:sm + 1, :]
            m_new = jnp.maximum(m, smax_ref[sm:sm + 1, :])
            m_ref[sm:sm + 1, :] = m_new
            al_ref[sm:sm + 1, :] = jnp.exp2(m - m_new)
        smax = pv = None
        for c in range(n_ch):
            rows = slice(c * ch, (c + 1) * ch)
            if sc is not None:
                t, h = sc
                s = _score_chunk(q8_ref, k8_ref, un_ref, key_start(t, c), ch, h)
                s_bufs[h % 2][rows, :] = s
                cmax = jnp.max(s, axis=0, keepdims=True)
                smax = cmax if smax is None else jnp.maximum(smax, cmax)
            if sm is not None:
                p_bufs[sm % 2][rows, :] = jnp.exp2(s_bufs[sm % 2][rows, :] - m_new).astype(BF16)
            if va is not None:
                t, h = va
                g = h // GROUP
                vt = vt_ref[g * VT_ROWS:(g + 1) * VT_ROWS, pl.ds(key_start(t, c), ch)]
                d = jnp.dot(vt, p_bufs[h % 2][rows, :], preferred_element_type=F32)
                pv = d if pv is None else pv + d
        if sc is not None:
            smax_ref[sc[1]:sc[1] + 1, :] = smax
        if va is not None:
            h = va[1]
            acc_ref[arows(h), :] = al_ref[h:h + 1, :] * acc_ref[arows(h), :] + pv

    def tile_steps(t, last):
        for h in range(Q_HEADS):
            if h + 2 < Q_HEADS:
                sc = (t, h + 2)
            else:
                sc = None if last else (t + 1, h + 2 - Q_HEADS)
            if h + 1 < Q_HEADS:
                sm = h + 1
            else:
                sm = None if last else 0
            step(sc, sm, (t, h))

    step(sc=(0, 0))
    step(sc=(0, 1), sm=0)
    if n_kt > 1:
        def body(t, carry):
            tile_steps(t, False)
            return carry
        lax.fori_loop(0, n_kt - 1, body, 0)
    tile_steps(n_kt - 1, True)


def _attend_bounded(q8_ref, k8_ref, un_ref, vt_ref, acc_ref, p_bufs, *, n_keys, tk):
    n_kt = n_keys // tk
    n_buf = len(p_bufs)
    arows = lambda h: slice(h * VT_ROWS, (h + 1) * VT_ROWS)
    acc_ref[...] = jnp.zeros(acc_ref.shape, F32)
    ch = min(tk, CHUNK)
    n_ch = tk // ch

    def key_start(t, c):
        return t * tk + c * ch if isinstance(t, int) else pl.multiple_of(t * tk + c * ch, ch)

    def step(sc=None, va=None):
        if va is not None:
            t, h = va
            g = h // GROUP
            vt = vt_ref[g * VT_ROWS:(g + 1) * VT_ROWS, pl.ds(key_start(t, 0), tk)]
            acc_ref[arows(h), :] = acc_ref[arows(h), :] + jnp.dot(
                vt, p_bufs[h % n_buf][0:tk, :], preferred_element_type=F32)
        if sc is not None:
            t, h = sc
            for c in range(n_ch):
                s = _score_chunk(q8_ref, k8_ref, un_ref, key_start(t, c), ch, h)
                p_bufs[h % n_buf][c * ch:(c + 1) * ch, :] = jnp.exp2(s).astype(BF16)

    def tile_steps(t, last):
        for h in range(Q_HEADS):
            if h + 1 < Q_HEADS:
                sc = (t, h + 1)
            else:
                sc = None if last else (t + 1, 0)
            step(sc, (t, h))

    step(sc=(0, 0))
    n_trips, n_rest = divmod(n_kt - 1, UNROLL)
    if n_trips > 0:
        def body(t, carry):
            for u in range(UNROLL):
                tile_steps(t * UNROLL + u, False)
            return carry
        lax.fori_loop(0, n_trips, body, 0)
    for u in range(n_rest):
        tile_steps(n_trips * UNROLL + u, False)
    tile_steps(n_kt - 1, True)


def _global_attn_kernel(q8_ref, k8_ref, un_ref, vt_ref, o_ref, acc_ref, *scratch, tk, bounded):
    i = pl.program_id(1)
    if bounded:
        attend = functools.partial(_attend_bounded, q8_ref, k8_ref, un_ref, vt_ref, acc_ref, scratch)
    else:
        m_ref, al_ref, smax_ref, s0_ref, s1_ref, p0_ref, p1_ref = scratch
        attend = functools.partial(_attend, q8_ref, k8_ref, un_ref, vt_ref, acc_ref, m_ref, al_ref,
                                   smax_ref, (s0_ref, s1_ref), (p0_ref, p1_ref))

    @pl.when(i == 0)
    def _():
        attend(n_keys=TM, tk=TM)

    @pl.when(i > 0)
    def _():
        attend(n_keys=k8_ref.shape[1], tk=tk)

    outs = []
    for h in range(Q_HEADS):
        a = acc_ref[h * VT_ROWS:(h + 1) * VT_ROWS, :]
        outs.append(a[0:HEAD_DIM, :] / a[HEAD_DIM:HEAD_DIM + 1, :])
    o_ref[...] = jnp.concatenate(outs, axis=0).T


def _key_tile(l):
    return next(t for t in (1280, 1024, 768, 512, 256) if l % t == 0)


def _global_attn(q8, k8, unscale, vt, score_bound):
    un = jnp.broadcast_to(unscale.astype(F32), (1, TM))
    return lax.cond(score_bound <= SCORE_BOUND,
                    functools.partial(_global_attn_call, bounded=True),
                    functools.partial(_global_attn_call, bounded=False), q8, k8, un, vt)


def _global_attn_call(q8, k8, un, vt, *, bounded):
    b, _, l, _ = k8.shape
    tk = _key_tile(l)
    stat = pltpu.VMEM((Q_HEADS, TM), F32)
    p_buf = pltpu.VMEM((tk, TM), BF16)
    s_buf = pltpu.VMEM((tk, TM), F32)
    scratch = [p_buf] * P_BUFS if bounded else [stat, stat, stat, s_buf, s_buf, p_buf, p_buf]
    return pl.pallas_call(
        functools.partial(_global_attn_kernel, tk=tk, bounded=bounded),
        grid=(b, l // TM),
        in_specs=[pl.BlockSpec((None, None, Q_HEADS * F8_ROWS, TM), lambda bi, i: (bi, i, 0, 0)),
                  pl.BlockSpec((None, KV_HEADS, l, F8_ROWS), lambda bi, i: (bi, 0, 0, 0)),
                  pl.BlockSpec((1, TM), lambda bi, i: (0, 0)),
                  pl.BlockSpec((None, KV_HEADS * VT_ROWS, l), lambda bi, i: (bi, 0, 0))],
        out_specs=pl.BlockSpec((None, TM, Q_COLS), lambda bi, i: (bi, i, 0)),
        out_shape=jax.ShapeDtypeStruct((b, l, Q_COLS), F32),
        scratch_shapes=[pltpu.VMEM((Q_HEADS * VT_ROWS, TM), F32)] + scratch,
        compiler_params=_params(2),
        name="global_attn_bounded" if bounded else "global_attn",
    )(q8, k8, un, vt)


def _window_geometry(i, tq, l_all):
    span = tq + 2 * WINDOW
    start = pl.multiple_of(jnp.clip(i * tq - WINDOW, 0, l_all - span), LANES)
    kpos = start - TM + lax.broadcasted_iota(jnp.int32, (span, 1), 0)
    qbase = jnp.where(i >= 1, i * tq - TM, -(1 << 20))
    qpos = qbase + lax.broadcasted_iota(jnp.int32, (1, tq), 1)
    ok = jnp.logical_and(kpos >= 0, jnp.abs(qpos - kpos) <= WINDOW)
    return start, jnp.where(ok, 0.0, NEG_INF).astype(F32)


def _window_attn_kernel(q8_ref, k8_ref, un_ref, vt_ref, sink_ref, o_ref, acc_ref):
    tq = q8_ref.shape[-1]
    span = tq + 2 * WINDOW
    start, bias = _window_geometry(pl.program_id(1), tq, k8_ref.shape[1])
    for h in range(Q_HEADS):
        g = h // GROUP
        s1 = _score_chunk(q8_ref, k8_ref, un_ref, 0, TM, h)
        s2 = _score_chunk(q8_ref, k8_ref, un_ref, start, span, h) + bias
        sink = sink_ref[h:h + 1, :]
        m = jnp.maximum(jnp.maximum(jnp.max(s1, axis=0, keepdims=True),
                                    jnp.max(s2, axis=0, keepdims=True)), sink)
        p1 = jnp.exp2(s1 - m)
        p2 = jnp.exp2(s2 - m)
        l = (jnp.sum(p1, axis=0, keepdims=True) + jnp.sum(p2, axis=0, keepdims=True)
             + jnp.exp2(sink - m))
        rows = slice(g * VT_ROWS, g * VT_ROWS + HEAD_DIM)
        acc = (jnp.dot(vt_ref[rows, 0:TM], p1.astype(BF16), preferred_element_type=F32)
               + jnp.dot(vt_ref[rows, pl.ds(start, span)], p2.astype(BF16),
                         preferred_element_type=F32))
        acc_ref[h * HEAD_DIM:(h + 1) * HEAD_DIM, :] = acc / l
    o_ref[...] = acc_ref[...].T


def _window_attn_bounded_kernel(*refs):
    q_refs = refs[:WIN_SUB]
    k8_ref, un_ref, vt_ref, sink_ref, o_ref, bias_ref, acc_ref = refs[WIN_SUB:WIN_SUB + 7]
    p_bufs = refs[WIN_SUB + 7:]
    tq = q_refs[0].shape[-1]
    l_all = k8_ref.shape[1]
    n_buf = len(p_bufs)
    starts = []
    for e in range(WIN_SUB):
        t = jnp.minimum(WIN_SUB * pl.program_id(1) + e, l_all // tq - 1)
        start, bias = _window_geometry(t, tq, l_all)
        bias_ref[e] = bias
        starts.append(start)
    n_win = bias_ref.shape[1] // CHUNK
    items = [(e, h) for h in range(Q_HEADS) for e in range(WIN_SUB)]

    def chunks(e):
        return [(0, None)] + [(pl.multiple_of(starts[e] + c * CHUNK, LANES), c)
                              for c in range(n_win)]

    def step(sc=None, va=None):
        if va is not None:
            e, h = items[va]
            g = h // GROUP
            pv = None
            for ci, (k0, _) in enumerate(chunks(e)):
                vt = vt_ref[g * VT_ROWS:(g + 1) * VT_ROWS, pl.ds(k0, CHUNK)]
                d = jnp.dot(vt, p_bufs[va % n_buf][ci * CHUNK:(ci + 1) * CHUNK, :],
                            preferred_element_type=F32)
                pv = d if pv is None else pv + d
            acc_ref[va * VT_ROWS:(va + 1) * VT_ROWS, :] = pv
        if sc is not None:
            e, h = items[sc]
            for ci, (k0, c) in enumerate(chunks(e)):
                s = _score_chunk(q_refs[e], k8_ref, un_ref, k0, CHUNK, h)
                if c is not None:
                    s = s + bias_ref[e, c * CHUNK:(c + 1) * CHUNK, :]
                p_bufs[sc % n_buf][ci * CHUNK:(ci + 1) * CHUNK, :] = jnp.exp2(s).astype(BF16)

    for n in range(WIN_LAG):
        step(sc=n)
    for n in range(len(items)):
        step(sc=n + WIN_LAG if n + WIN_LAG < len(items) else None, va=n)
    for e in range(WIN_SUB):
        outs = []
        for h in range(Q_HEADS):
            n = items.index((e, h))
            a = acc_ref[n * VT_ROWS:(n + 1) * VT_ROWS, :]
            l = a[HEAD_DIM:HEAD_DIM + 1, :] + jnp.exp2(sink_ref[h:h + 1, :])
            outs.append(a[0:HEAD_DIM, :] / l)
        o_ref[e * tq:(e + 1) * tq, :] = jnp.concatenate(outs, axis=0).T


def _window_attn(q8, k8, unscale, vt, sink, score_bound):
    sink2 = sink.astype(F32) * LOG2E
    ok = jnp.logical_and(score_bound <= SCORE_BOUND, jnp.max(jnp.abs(sink2)) <= SCORE_BOUND)
    sink_b = jnp.broadcast_to(sink2[:, None], (Q_HEADS, TM))
    un = jnp.broadcast_to(unscale.astype(F32), (1, TM))
    return lax.cond(ok, functools.partial(_window_attn_call, bounded=True),
                    functools.partial(_window_attn_call, bounded=False), q8, k8, un, vt, sink_b)


def _window_attn_call(q8, k8, un, vt, sink_b, *, bounded):
    b, _, l, _ = k8.shape
    n_tiles = l // TM
    span = TM + 2 * WINDOW
    n_sub = WIN_SUB if bounded else 1
    q_spec = lambda e: pl.BlockSpec(
        (None, None, Q_HEADS * F8_ROWS, TM),
        lambda bi, i: (bi, jnp.minimum(n_sub * i + e, n_tiles - 1), 0, 0))
    if bounded:
        scratch = [pltpu.VMEM((WIN_SUB, span, TM), F32),
                   pltpu.VMEM((WIN_SUB * Q_HEADS * VT_ROWS, TM), F32)]
        scratch += [pltpu.VMEM((TM + span, TM), BF16)] * (2 * WIN_LAG)
    else:
        scratch = [pltpu.VMEM((Q_COLS, TM), F32)]
    return pl.pallas_call(
        _window_attn_bounded_kernel if bounded else _window_attn_kernel,
        grid=(b, pl.cdiv(n_tiles, n_sub)),
        in_specs=[q_spec(e) for e in range(n_sub)] + [
            pl.BlockSpec((None, KV_HEADS, l, F8_ROWS), lambda bi, i: (bi, 0, 0, 0)),
            pl.BlockSpec((1, TM), lambda bi, i: (0, 0)),
            pl.BlockSpec((None, KV_HEADS * VT_ROWS, l), lambda bi, i: (bi, 0, 0)),
            pl.BlockSpec((Q_HEADS, TM), lambda bi, i: (0, 0))],
        out_specs=pl.BlockSpec((None, n_sub * TM, Q_COLS), lambda bi, i: (bi, i, 0)),
        out_shape=jax.ShapeDtypeStruct((b, l, Q_COLS), F32),
        scratch_shapes=scratch,
        compiler_params=_params(2),
        name="window_attn_bounded" if bounded else "window_attn",
    )(*([q8] * n_sub), k8, un, vt, sink_b)


def _merge_kernel(x_ref, mod_ref, g_ref, oa_ref, oc_ref, b_ref, cu_ref, cup_ref, cun_ref, cw_ref,
                  wgate_ref, bgate_ref, wpa_ref, wpb_ref, wpc_ref, wo_ref, o_ref):
    i = pl.program_id(1)
    n_tiles = pl.num_programs(1)
    x = x_ref[...]
    tm, d = x.shape
    h = _rms_mod(x, g_ref[...], mod_ref[3:4, :], mod_ref[4:5, :]).astype(BF16)
    gates = _sigmoid(jnp.dot(h, wgate_ref[...], preferred_element_type=F32) + bgate_ref[...])

    cu = cu_ref[...]
    has_prev = (i >= 2).astype(F32)
    has_next = jnp.logical_and(i >= 1, i < n_tiles - 1).astype(F32)
    prev_row = cup_ref[7:8, :] * has_prev
    next_row = cun_ref[0:1, :] * has_next
    row = lax.broadcasted_iota(jnp.int32, (tm, 1), 0)
    cu_dn = jnp.where(row == 0, prev_row, pltpu.roll(cu, 1, 0))
    cu_up = jnp.where(row == tm - 1, next_row, pltpu.roll(cu, tm - 1, 0))
    y = cw_ref[0:1, :] * cu_dn + cw_ref[1:2, :] * cu + cw_ref[2:3, :] * cu_up
    o_b = (b_ref[...] * y).astype(BF16)

    pa = jnp.dot(oa_ref[...].astype(BF16), wpa_ref[...], preferred_element_type=F32)
    pb = jnp.dot(o_b, wpb_ref[...], preferred_element_type=F32)
    pc = jnp.dot(oc_ref[...].astype(BF16), wpc_ref[...], preferred_element_type=F32)
    mix = gates[:, 0:d] * pa + gates[:, d:2 * d] * pb + gates[:, 2 * d:3 * d] * pc
    out = jnp.dot(mix.astype(BF16), wo_ref[...], preferred_element_type=F32)
    o_ref[...] = x + mod_ref[5:6, :] * out


def _merge(xs, mod, norm_g, o_a, o_c, bb, cu, conv_w, w_gate, b_gate, w_pa, w_pb, w_pc, w_o,
           *, layer):
    b, l, d = xs.shape
    cw = cu.shape[-1]
    sub = 8
    tok = lambda w: pl.BlockSpec((None, TM, w), lambda bi, i: (bi, i, 0))
    lay = lambda *s: _resident((None,) + s, lambda bi, i: (layer,) + (0,) * len(s))
    return pl.pallas_call(
        _merge_kernel,
        grid=(b, l // TM),
        in_specs=[tok(d),
                  pl.BlockSpec((None, None, N_MOD, d), lambda bi, i: (bi, jnp.minimum(i, 1), 0, 0)),
                  _resident((None, None, 1, d), lambda bi, i: (layer, 1, 0, 0)),
                  tok(Q_COLS), tok(Q_COLS), tok(cw), tok(cw),
                  pl.BlockSpec((None, sub, cw),
                               lambda bi, i: (bi, jnp.maximum(i * (TM // sub) - 1, 0), 0)),
                  pl.BlockSpec((None, sub, cw),
                               lambda bi, i: (bi, jnp.minimum((i + 1) * (TM // sub), l // sub - 1), 0)),
                  lay(3, cw), lay(d, 3 * d), lay(1, 3 * d), lay(Q_COLS, d), lay(cw, d),
                  lay(Q_COLS, d), lay(d, d)],
        out_specs=tok(d),
        out_shape=jax.ShapeDtypeStruct(xs.shape, F32),
        input_output_aliases={0: 0},
        compiler_params=_params(2),
        name="merge",
    )(xs, mod, norm_g, o_a, o_c, bb, cu, cu, cu, conv_w, w_gate, b_gate, w_pa, w_pb, w_pc, w_o)


def _rope_tables(s, n_ctx):
    rows = s // GRID_W
    row = jnp.repeat(jnp.arange(rows), GRID_W).astype(F32)
    col = jnp.tile(jnp.arange(GRID_W), rows).astype(F32)
    half = HEAD_DIM // 2
    inv = ROPE_THETA ** (-jnp.arange(0, half, 2, dtype=F32) / half)
    ang_r = row[:, None] * inv
    ang_c = col[:, None] * inv
    tabs = jnp.concatenate([jnp.cos(ang_r), jnp.sin(ang_r), jnp.cos(ang_c), jnp.sin(ang_c)],
                           axis=-1)
    one, zero = jnp.ones((n_ctx, HEAD_DIM // 4), F32), jnp.zeros((n_ctx, HEAD_DIM // 4), F32)
    ident = jnp.concatenate([one, zero, one, zero], axis=-1)
    return jnp.concatenate([ident, tabs], axis=0).T


def kernel(x, c, ctx, c_ctx, w_ada, b_ada, norm_g, ffn_w_gate, ffn_w_up, ffn_w_down, w_in, qk_g,
           sink_a, conv_w, w_pa, w_pb, w_pc, w_gate, b_gate, w_o):
    bsz, s, d = x.shape
    n_ctx = ctx.shape[1]
    depth = w_ada.shape[0]
    assert n_ctx == TM and s % TM == 0 and s % GRID_W == 0 and bsz < 8

    cvec = jnp.zeros((8, d), F32).at[:bsz].set(c).at[bsz].set(c_ctx)
    mods = _ada(cvec, w_ada, b_ada).reshape(depth, 8, N_MOD, d)
    lat = mods[:, :bsz]
    con = jnp.broadcast_to(mods[:, bsz:bsz + 1], lat.shape)
    mod_all = jnp.stack([con, lat], axis=2)

    rope = _rope_tables(s, n_ctx)
    gmax = jnp.max(jnp.abs(qk_g.astype(F32)), axis=-1)
    vmax = gmax * (HEAD_DIM ** 0.5) * jnp.array([Q_SCALE, 1.0, Q_SCALE, 1.0], F32)
    pow2 = jnp.exp2(jnp.clip(jnp.floor(jnp.log2(0.98 * F8_MAX / vmax)), -20.0, 20.0))
    unscale = 1.0 / (pow2[:, 0::2] * pow2[:, 1::2])
    fold = (pow2 * jnp.array([Q_SCALE, 1.0, Q_SCALE, 1.0], F32))[:, :, None]
    qkg = jnp.broadcast_to((qk_g.astype(F32) * fold).reshape(depth, 4 * HEAD_DIM, 1),
                           (depth, 4 * HEAD_DIM, TM))
    ng = norm_g.astype(F32)[:, :, None, :]
    wg, wu, wd = (w.astype(BF16) for w in (ffn_w_gate, ffn_w_up, ffn_w_down))
    w_in_b, w_gate_b, w_pa_b, w_pb_b, w_pc_b, w_o_b = (
        w.astype(BF16) for w in (w_in, w_gate, w_pa, w_pb, w_pc, w_o))
    b_gate3 = b_gate.astype(F32)[:, None, :]

    xs = jnp.concatenate([ctx, x], axis=1)
    for i in range(depth):
        mod = mod_all[i]
        xs = _ffn(xs, mod, ng, wg, wu, wd, layer=i, which=0, j0=0)
        qa8, qc8, ka8, kc8, vta, vtc, bb, cu = _inproj(xs, mod, ng, w_in_b, qkg, rope, layer=i)
        o_a = _window_attn(qa8, ka8, unscale[i, 0], vta, sink_a[i], 1.02 * vmax[i, 0] * vmax[i, 1])
        o_c = _global_attn(qc8, kc8, unscale[i, 1], vtc, 1.02 * vmax[i, 2] * vmax[i, 3])
        xs = _merge(xs, mod, ng, o_a, o_c, bb, cu, conv_w.astype(F32), w_gate_b, b_gate3,
                    w_pa_b, w_pb_b, w_pc_b, w_o_b, layer=i)
        xs = _ffn(xs, mod, ng, wg, wu, wd, layer=i, which=1, j0=6, latents_out=i + 1 == depth)
    return xs
```

```python
import functools

import jax
import jax.numpy as jnp
from jax import lax
from jax.experimental import pallas as pl
from jax.experimental.pallas import tpu as pltpu

GRID_W = 64
HEAD_DIM = 64
Q_HEADS = 8
KV_HEADS = 2
GROUP = Q_HEADS // KV_HEADS
WINDOW = 128
N_MOD = 9
ROPE_THETA = 10000.0
EPS = 1e-6
NEG_INF = -1e30
SCALE = HEAD_DIM ** -0.5
LOG2E = 1.4426950408889634
Q_SCALE = SCALE * LOG2E
SCORE_BOUND = 48.0
Q_COLS = Q_HEADS * HEAD_DIM
KV_COLS = KV_HEADS * HEAD_DIM
VT_ROWS = HEAD_DIM + 16

LANES = 128
TM = 256
FFN_TM = 512
CHUNK = 256
P_BUFS = 2
UNROLL = 6
WIN_LAG = 2
WIN_SUB = 2
VMEM_LIMIT = 56 * 1024 * 1024

F32 = jnp.float32
BF16 = jnp.bfloat16
F8 = jnp.float8_e4m3fn
F8_MAX = 448.0
F8_ROWS = 4 * HEAD_DIM


def _params(n_axes):
    return pltpu.CompilerParams(dimension_semantics=("arbitrary",) * n_axes,
                                vmem_limit_bytes=VMEM_LIMIT)


def _resident(shape, index_map):
    return pl.BlockSpec(shape, index_map, pipeline_mode=pl.Buffered(1))


def _sigmoid(v):
    return 1.0 / (1.0 + jnp.exp(-v))


def _rms_mod(x, g, shift, scale):
    y = x * lax.rsqrt(jnp.mean(x * x, axis=-1, keepdims=True) + EPS)
    return (y * g) * (1.0 + scale) + shift


def _ada_kernel(c_ref, w_ref, b_ref, o_ref):
    c = c_ref[...]
    s = (c * _sigmoid(c)).astype(BF16)
    o_ref[...] = jnp.dot(s, w_ref[...].astype(BF16), preferred_element_type=F32) + b_ref[...]


def _ada(cvec, w_ada, b_ada):
    depth, d, n = w_ada.shape
    tn = 1024
    return pl.pallas_call(
        _ada_kernel,
        grid=(depth, n // tn),
        in_specs=[pl.BlockSpec((8, d), lambda l, j: (0, 0)),
                  pl.BlockSpec((None, d, tn), lambda l, j: (l, 0, j)),
                  pl.BlockSpec((None, 1, tn), lambda l, j: (l, 0, j))],
        out_specs=pl.BlockSpec((None, 8, tn), lambda l, j: (l, 0, j)),
        out_shape=jax.ShapeDtypeStruct((depth, 8, n), F32),
        compiler_params=_params(2),
        name="ada",
    )(cvec, w_ada, b_ada.reshape(depth, 1, n))


def _ffn_kernel(x_ref, mod_ref, g_ref, wg_ref, wu_ref, wd_ref, o_ref, *, j0):
    xb = x_ref[...].reshape(x_ref.shape[-2:])
    outs = []
    for r in range(0, xb.shape[0], TM):
        x = xb[r:r + TM, :]
        h = _rms_mod(x, g_ref[...], mod_ref[j0:j0 + 1, :], mod_ref[j0 + 1:j0 + 2, :]).astype(BF16)
        a = jnp.dot(h, wg_ref[...], preferred_element_type=F32)
        u = jnp.dot(h, wu_ref[...], preferred_element_type=F32)
        act = (a * _sigmoid(a) * u).astype(BF16)
        y = jnp.dot(act, wd_ref[...], preferred_element_type=F32)
        outs.append(x + (0.5 * mod_ref[j0 + 2:j0 + 3, :]) * y)
    o_ref[...] = jnp.concatenate(outs, axis=0).reshape(o_ref.shape)


def _ffn(xs, mod, norm_g, wg, wu, wd, *, layer, which, j0, latents_out=False):
    b, l, d = xs.shape
    dff = wg.shape[-1]

    def call(xs, x_spec, n_steps, stream, out_spec=None, out_rows=None):
        return pl.pallas_call(
            functools.partial(_ffn_kernel, j0=j0),
            grid=(b, n_steps),
            in_specs=[x_spec,
                      pl.BlockSpec((None, None, N_MOD, d), lambda bi, i: (bi, stream, 0, 0)),
                      _resident((None, None, 1, d), lambda bi, i: (layer, 2 * which, 0, 0)),
                      _resident((None, None, d, dff), lambda bi, i: (layer, which, 0, 0)),
                      _resident((None, None, d, dff), lambda bi, i: (layer, which, 0, 0)),
                      _resident((None, None, dff, d), lambda bi, i: (layer, which, 0, 0))],
            out_specs=x_spec if out_spec is None else out_spec,
            out_shape=jax.ShapeDtypeStruct((b, l if out_spec is None else out_rows, d), F32),
            input_output_aliases={0: 0} if out_spec is None else {},
            compiler_params=_params(2),
            name="ffn_ctx" if stream == 0 else "ffn",
        )(xs, mod, norm_g, wg, wu, wd)

    tl = FFN_TM if (l - TM) % FFN_TM == 0 else TM
    lat_spec = pl.BlockSpec((pl.Element(1), pl.Element(tl), pl.Element(d)),
                            lambda bi, i: (bi, pl.multiple_of(TM + i * tl, TM), 0))
    if latents_out:
        return call(xs, lat_spec, (l - TM) // tl, 1,
                    pl.BlockSpec((None, tl, d), lambda bi, i: (bi, i, 0)), l - TM)
    xs = call(xs, pl.BlockSpec((None, TM, d), lambda bi, i: (bi, 0, 0)), 1, 0)
    return call(xs, lat_spec, (l - TM) // tl, 1)


def _norm_rope_t(tt, gain, rope):
    ms = jnp.sum(tt * tt, axis=0, keepdims=True) * (1.0 / HEAD_DIM)
    y = tt * lax.rsqrt(ms + EPS) * gain
    q = HEAD_DIM // 4
    cr, sr, cc, sc = (rope[j * q:(j + 1) * q, :] for j in range(4))
    y1, y2, y3, y4 = (y[j * q:(j + 1) * q, :] for j in range(4))
    return jnp.concatenate([y1 * cr - y2 * sr, y2 * cr + y1 * sr,
                            y3 * cc - y4 * sc, y4 * cc + y3 * sc], axis=0)


def _split8(y):
    hi = y.astype(F8).astype(F32)
    return hi, y - hi


def _inproj_kernel(x_ref, mod_ref, g_ref, w_ref, qkg_ref, rope_ref,
                   qa8_ref, qc8_ref, ka8_ref, kc8_ref, vta_ref, vtc_ref, b_ref, cu_ref):
    x = x_ref[...]
    h = _rms_mod(x, g_ref[...], mod_ref[3:4, :], mod_ref[4:5, :]).astype(BF16)
    z = jnp.dot(h, w_ref[...], preferred_element_type=F32)
    rope = rope_ref[...]

    def heads(lo, j):
        tt = z[:, lo:lo + LANES].T
        gain = qkg_ref[j * HEAD_DIM:(j + 1) * HEAD_DIM, :]
        return [_norm_rope_t(tt[0:HEAD_DIM, :], gain, rope),
                _norm_rope_t(tt[HEAD_DIM:, :], gain, rope)]

    zeros = jnp.zeros((HEAD_DIM, x.shape[0]), F32)

    for k8_ref, col, j in ((ka8_ref, 0, 1), (kc8_ref, 2 * KV_COLS, 3)):
        for g, y in enumerate(heads(col, j)):
            hi, lo = _split8(y)
            k8_ref[g] = jnp.concatenate([hi, hi, lo, zeros], axis=0).T.astype(F8)
    ones = jnp.ones((VT_ROWS - HEAD_DIM, x.shape[0]), BF16)
    for vt_ref, lo in ((vta_ref, KV_COLS), (vtc_ref, 3 * KV_COLS)):
        vt = z[:, lo:lo + KV_COLS].T.astype(BF16)
        for g in range(KV_HEADS):
            vt_ref[g * VT_ROWS:g * VT_ROWS + HEAD_DIM, :] = vt[g * HEAD_DIM:(g + 1) * HEAD_DIM, :]
            vt_ref[g * VT_ROWS + HEAD_DIM:(g + 1) * VT_ROWS, :] = ones
    q0 = 4 * KV_COLS
    for c in range(Q_COLS // LANES):
        for q8_ref, col, j in ((qa8_ref, q0, 0), (qc8_ref, q0 + Q_COLS, 2)):
            for e, y in enumerate(heads(col + c * LANES, j)):
                hi, lo = _split8(y)
                r0 = (2 * c + e) * F8_ROWS
                q8_ref[r0:r0 + F8_ROWS, :] = jnp.concatenate([hi, lo, hi, zeros],
                                                             axis=0).astype(F8)
    c0 = q0 + 2 * Q_COLS
    cw = b_ref.shape[-1]
    b_ref[...] = z[:, c0:c0 + cw]
    cu_ref[...] = z[:, c0 + cw:c0 + 2 * cw] * z[:, c0 + 2 * cw:c0 + 3 * cw]


def _inproj(xs, mod, norm_g, w_in, qkg, rope, *, layer):
    b, l, d = xs.shape
    ncol = w_in.shape[-1]
    cw = (ncol - 4 * KV_COLS - 2 * Q_COLS) // 3
    tok = lambda w: pl.BlockSpec((None, TM, w), lambda bi, i: (bi, i, 0))
    tr = lambda w: pl.BlockSpec((None, w, TM), lambda bi, i: (bi, 0, i))
    k8_spec = pl.BlockSpec((None, KV_HEADS, TM, F8_ROWS), lambda bi, i: (bi, 0, i, 0))
    q8_spec = pl.BlockSpec((None, None, Q_HEADS * F8_ROWS, TM), lambda bi, i: (bi, i, 0, 0))
    return pl.pallas_call(
        _inproj_kernel,
        grid=(b, l // TM),
        in_specs=[tok(d),
                  pl.BlockSpec((None, None, N_MOD, d), lambda bi, i: (bi, jnp.minimum(i, 1), 0, 0)),
                  _resident((None, None, 1, d), lambda bi, i: (layer, 1, 0, 0)),
                  _resident((None, d, ncol), lambda bi, i: (layer, 0, 0)),
                  _resident((None, 4 * HEAD_DIM, TM), lambda bi, i: (layer, 0, 0)),
                  pl.BlockSpec((HEAD_DIM, TM), lambda bi, i: (0, i))],
        out_specs=[q8_spec, q8_spec, k8_spec, k8_spec,
                   tr(KV_HEADS * VT_ROWS), tr(KV_HEADS * VT_ROWS), tok(cw), tok(cw)],
        out_shape=[jax.ShapeDtypeStruct((b, l // TM, Q_HEADS * F8_ROWS, TM), F8),
                   jax.ShapeDtypeStruct((b, l // TM, Q_HEADS * F8_ROWS, TM), F8),
                   jax.ShapeDtypeStruct((b, KV_HEADS, l, F8_ROWS), F8),
                   jax.ShapeDtypeStruct((b, KV_HEADS, l, F8_ROWS), F8),
                   jax.ShapeDtypeStruct((b, KV_HEADS * VT_ROWS, l), BF16),
                   jax.ShapeDtypeStruct((b, KV_HEADS * VT_ROWS, l), BF16),
                   jax.ShapeDtypeStruct((b, l, cw), F32),
                   jax.ShapeDtypeStruct((b, l, cw), F32)],
        compiler_params=_params(2),
        name="inproj",
    )(xs, mod, norm_g, w_in, qkg, rope)


def _score_chunk(q8_ref, k8_ref, un_ref, start, ch, h):
    s = jnp.dot(k8_ref[h // GROUP, pl.ds(start, ch), :], q8_ref[h * F8_ROWS:(h + 1) * F8_ROWS, :],
                preferred_element_type=F32)
    return s * un_ref[...]


def _attend(q8_ref, k8_ref, un_ref, vt_ref, acc_ref, m_ref, al_ref, smax_ref, s_bufs, p_bufs, *,
            n_keys, tk):
    n_kt = n_keys // tk
    arows = lambda h: slice(h * VT_ROWS, (h + 1) * VT_ROWS)
    m_ref[...] = jnp.full(m_ref.shape, NEG_INF, F32)
    acc_ref[...] = jnp.zeros(acc_ref.shape, F32)

    ch = min(tk, CHUNK)
    n_ch = tk // ch

    def key_start(t, c):
        return t * tk + c * ch if isinstance(t, int) else pl.multiple_of(t * tk + c * ch, ch)

    def step(sc=None, sm=None, va=None):
        if sm is not None:
            m = m_ref[sm:sm + 1, :]
            m_new = jnp.maximum(m, smax_ref[sm:sm + 1, :])
            m_ref[sm:sm + 1, :] = m_new
            al_ref[sm:sm + 1, :] = jnp.exp2(m - m_new)
        smax = pv = None
        for c in range(n_ch):
            rows = slice(c * ch, (c + 1) * ch)
            if sc is not None:
                t, h = sc
                s = _score_chunk(q8_ref, k8_ref, un_ref, key_start(t, c), ch, h)
                s_bufs[h % 2][rows, :] = s
                cmax = jnp.max(s, axis=0, keepdims=True)
                smax = cmax if smax is None else jnp.maximum(smax, cmax)
            if sm is not None:
                p_bufs[sm % 2][rows, :] = jnp.exp2(s_bufs[sm % 2][rows, :] - m_new).astype(BF16)
            if va is not None:
                t, h = va
                g = h // GROUP
                vt = vt_ref[g * VT_ROWS:(g + 1) * VT_ROWS, pl.ds(key_start(t, c), ch)]
                d = jnp.dot(vt, p_bufs[h % 2][rows, :], preferred_element_type=F32)
                pv = d if pv is None else pv + d
        if sc is not None:
            smax_ref[sc[1]:sc[1] + 1, :] = smax
        if va is not None:
            h = va[1]
            acc_ref[arows(h), :] = al_ref[h:h + 1, :] * acc_ref[arows(h), :] + pv

    def tile_steps(t, last):
        for h in range(Q_HEADS):
            if h + 2 < Q_HEADS:
                sc = (t, h + 2)
            else:
                sc = None if last else (t + 1, h + 2 - Q_HEADS)
            if h + 1 < Q_HEADS:
                sm = h + 1
            else:
                sm = None if last else 0
            step(sc, sm, (t, h))

    step(sc=(0, 0))
    step(sc=(0, 1), sm=0)
    if n_kt > 1:
        def body(t, carry):
            tile_steps(t, False)
            return carry
        lax.fori_loop(0, n_kt - 1, body, 0)
    tile_steps(n_kt - 1, True)


def _attend_bounded(q8_ref, k8_ref, un_ref, vt_ref, acc_ref, p_bufs, *, n_keys, tk):
    n_kt = n_keys // tk
    n_buf = len(p_bufs)
    arows = lambda h: slice(h * VT_ROWS, (h + 1) * VT_ROWS)
    acc_ref[...] = jnp.zeros(acc_ref.shape, F32)
    ch = min(tk, CHUNK)
    n_ch = tk // ch

    def key_start(t, c):
        return t * tk + c * ch if isinstance(t, int) else pl.multiple_of(t * tk + c * ch, ch)

    def step(sc=None, va=None):
        if va is not None:
            t, h = va
            g = h // GROUP
            vt = vt_ref[g * VT_ROWS:(g + 1) * VT_ROWS, pl.ds(key_start(t, 0), tk)]
            acc_ref[arows(h), :] = acc_ref[arows(h), :] + jnp.dot(
                vt, p_bufs[h % n_buf][0:tk, :], preferred_element_type=F32)
        if sc is not None:
            t, h = sc
            for c in range(n_ch):
                s = _score_chunk(q8_ref, k8_ref, un_ref, key_start(t, c), ch, h)
                p_bufs[h % n_buf][c * ch:(c + 1) * ch, :] = jnp.exp2(s).astype(BF16)

    def tile_steps(t, last):
        for h in range(Q_HEADS):
            if h + 1 < Q_HEADS:
                sc = (t, h + 1)
            else:
                sc = None if last else (t + 1, 0)
            step(sc, (t, h))

    step(sc=(0, 0))
    n_trips, n_rest = divmod(n_kt - 1, UNROLL)
    if n_trips > 0:
        def body(t, carry):
            for u in range(UNROLL):
                tile_steps(t * UNROLL + u, False)
            return carry
        lax.fori_loop(0, n_trips, body, 0)
    for u in range(n_rest):
        tile_steps(n_trips * UNROLL + u, False)
    tile_steps(n_kt - 1, True)


def _global_attn_kernel(q8_ref, k8_ref, un_ref, vt_ref, o_ref, acc_ref, *scratch, tk, bounded):
    i = pl.program_id(1)
    if bounded:
        attend = functools.partial(_attend_bounded, q8_ref, k8_ref, un_ref, vt_ref, acc_ref, scratch)
    else:
        m_ref, al_ref, smax_ref, s0_ref, s1_ref, p0_ref, p1_ref = scratch
        attend = functools.partial(_attend, q8_ref, k8_ref, un_ref, vt_ref, acc_ref, m_ref, al_ref,
                                   smax_ref, (s0_ref, s1_ref), (p0_ref, p1_ref))

    @pl.when(i == 0)
    def _():
        attend(n_keys=TM, tk=TM)

    @pl.when(i > 0)
    def _():
        attend(n_keys=k8_ref.shape[1], tk=tk)

    outs = []
    for h in range(Q_HEADS):
        a = acc_ref[h * VT_ROWS:(h + 1) * VT_ROWS, :]
        outs.append(a[0:HEAD_DIM, :] / a[HEAD_DIM:HEAD_DIM + 1, :])
    o_ref[...] = jnp.concatenate(outs, axis=0).T.astype(o_ref.dtype)


def _key_tile(l):
    return next(t for t in (1280, 1024, 768, 512, 256) if l % t == 0)


def _global_attn(q8, k8, unscale, vt, score_bound):
    un = jnp.broadcast_to(unscale.astype(F32), (1, TM))
    return lax.cond(score_bound <= SCORE_BOUND,
                    functools.partial(_global_attn_call, bounded=True),
                    functools.partial(_global_attn_call, bounded=False), q8, k8, un, vt)


def _global_attn_call(q8, k8, un, vt, *, bounded):
    b, _, l, _ = k8.shape
    tk = _key_tile(l)
    stat = pltpu.VMEM((Q_HEADS, TM), F32)
    p_buf = pltpu.VMEM((tk, TM), BF16)
    s_buf = pltpu.VMEM((tk, TM), F32)
    scratch = [p_buf] * P_BUFS if bounded else [stat, stat, stat, s_buf, s_buf, p_buf, p_buf]
    return pl.pallas_call(
        functools.partial(_global_attn_kernel, tk=tk, bounded=bounded),
        grid=(b, l // TM),
        in_specs=[pl.BlockSpec((None, None, Q_HEADS * F8_ROWS, TM), lambda bi, i: (bi, i, 0, 0)),
                  pl.BlockSpec((None, KV_HEADS, l, F8_ROWS), lambda bi, i: (bi, 0, 0, 0)),
                  pl.BlockSpec((1, TM), lambda bi, i: (0, 0)),
                  pl.BlockSpec((None, KV_HEADS * VT_ROWS, l), lambda bi, i: (bi, 0, 0))],
        out_specs=pl.BlockSpec((None, TM, Q_COLS), lambda bi, i: (bi, i, 0)),
        out_shape=jax.ShapeDtypeStruct((b, l, Q_COLS), BF16),
        scratch_shapes=[pltpu.VMEM((Q_HEADS * VT_ROWS, TM), F32)] + scratch,
        compiler_params=_params(2),
        name="global_attn_bounded" if bounded else "global_attn",
    )(q8, k8, un, vt)


def _window_geometry(i, tq, l_all):
    span = tq + 2 * WINDOW
    start = pl.multiple_of(jnp.clip(i * tq - WINDOW, 0, l_all - span), LANES)
    kpos = start - TM + lax.broadcasted_iota(jnp.int32, (span, 1), 0)
    qbase = jnp.where(i >= 1, i * tq - TM, -(1 << 20))
    qpos = qbase + lax.broadcasted_iota(jnp.int32, (1, tq), 1)
    ok = jnp.logical_and(kpos >= 0, jnp.abs(qpos - kpos) <= WINDOW)
    return start, jnp.where(ok, 0.0, NEG_INF).astype(F32)


def _window_attn_kernel(q8_ref, k8_ref, un_ref, vt_ref, sink_ref, o_ref, acc_ref):
    tq = q8_ref.shape[-1]
    span = tq + 2 * WINDOW
    start, bias = _window_geometry(pl.program_id(1), tq, k8_ref.shape[1])
    for h in range(Q_HEADS):
        g = h // GROUP
        s1 = _score_chunk(q8_ref, k8_ref, un_ref, 0, TM, h)
        s2 = _score_chunk(q8_ref, k8_ref, un_ref, start, span, h) + bias
        sink = sink_ref[h:h + 1, :]
        m = jnp.maximum(jnp.maximum(jnp.max(s1, axis=0, keepdims=True),
                                    jnp.max(s2, axis=0, keepdims=True)), sink)
        p1 = jnp.exp2(s1 - m)
        p2 = jnp.exp2(s2 - m)
        l = (jnp.sum(p1, axis=0, keepdims=True) + jnp.sum(p2, axis=0, keepdims=True)
             + jnp.exp2(sink - m))
        rows = slice(g * VT_ROWS, g * VT_ROWS + HEAD_DIM)
        acc = (jnp.dot(vt_ref[rows, 0:TM], p1.astype(BF16), preferred_element_type=F32)
               + jnp.dot(vt_ref[rows, pl.ds(start, span)], p2.astype(BF16),
                         preferred_element_type=F32))
        acc_ref[h * HEAD_DIM:(h + 1) * HEAD_DIM, :] = acc / l
    o_ref[...] = acc_ref[...].T.astype(o_ref.dtype)


def _window_attn_bounded_kernel(*refs):
    q_refs = refs[:WIN_SUB]
    k8_ref, un_ref, vt_ref, sink_ref, o_ref, bias_ref, acc_ref = refs[WIN_SUB:WIN_SUB + 7]
    p_bufs = refs[WIN_SUB + 7:]
    tq = q_refs[0].shape[-1]
    l_all = k8_ref.shape[1]
    n_buf = len(p_bufs)
    starts = []
    for e in range(WIN_SUB):
        t = jnp.minimum(WIN_SUB * pl.program_id(1) + e, l_all // tq - 1)
        start, bias = _window_geometry(t, tq, l_all)
        bias_ref[e] = bias
        starts.append(start)
    n_win = bias_ref.shape[1] // CHUNK
    items = [(e, h) for h in range(Q_HEADS) for e in range(WIN_SUB)]

    def chunks(e):
        return [(0, None)] + [(pl.multiple_of(starts[e] + c * CHUNK, LANES), c)
                              for c in range(n_win)]

    def step(sc=None, va=None):
        if va is not None:
            e, h = items[va]
            g = h // GROUP
            pv = None
            for ci, (k0, _) in enumerate(chunks(e)):
                vt = vt_ref[g * VT_ROWS:(g + 1) * VT_ROWS, pl.ds(k0, CHUNK)]
                d = jnp.dot(vt, p_bufs[va % n_buf][ci * CHUNK:(ci + 1) * CHUNK, :],
                            preferred_element_type=F32)
                pv = d if pv is None else pv + d
            acc_ref[va * VT_ROWS:(va + 1) * VT_ROWS, :] = pv
        if sc is not None:
            e, h = items[sc]
            for ci, (k0, c) in enumerate(chunks(e)):
                s = _score_chunk(q_refs[e], k8_ref, un_ref, k0, CHUNK, h)
                if c is not None:
                    s = s + bias_ref[e, c * CHUNK:(c + 1) * CHUNK, :]
                p_bufs[sc % n_buf][ci * CHUNK:(ci + 1) * CHUNK, :] = jnp.exp2(s).astype(BF16)

    for n in range(WIN_LAG):
        step(sc=n)
    for n in range(len(items)):
        step(sc=n + WIN_LAG if n + WIN_LAG < len(items) else None, va=n)
    for e in range(WIN_SUB):
        outs = []
        for h in range(Q_HEADS):
            n = items.index((e, h))
            a = acc_ref[n * VT_ROWS:(n + 1) * VT_ROWS, :]
            l = a[HEAD_DIM:HEAD_DIM + 1, :] + jnp.exp2(sink_ref[h:h + 1, :])
            outs.append(a[0:HEAD_DIM, :] / l)
        o_ref[e * tq:(e + 1) * tq, :] = jnp.concatenate(outs, axis=0).T.astype(o_ref.dtype)


def _window_attn(q8, k8, unscale, vt, sink, score_bound):
    sink2 = sink.astype(F32) * LOG2E
    ok = jnp.logical_and(score_bound <= SCORE_BOUND, jnp.max(jnp.abs(sink2)) <= SCORE_BOUND)
    sink_b = jnp.broadcast_to(sink2[:, None], (Q_HEADS, TM))
    un = jnp.broadcast_to(unscale.astype(F32), (1, TM))
    return lax.cond(ok, functools.partial(_window_attn_call, bounded=True),
                    functools.partial(_window_attn_call, bounded=False), q8, k8, un, vt, sink_b)


def _window_attn_call(q8, k8, un, vt, sink_b, *, bounded):
    b, _, l, _ = k8.shape
    n_tiles = l // TM
    span = TM + 2 * WINDOW
    n_sub = WIN_SUB if bounded else 1
    q_spec = lambda e: pl.BlockSpec(
        (None, None, Q_HEADS * F8_ROWS, TM),
        lambda bi, i: (bi, jnp.minimum(n_sub * i + e, n_tiles - 1), 0, 0))
    if bounded:
        scratch = [pltpu.VMEM((WIN_SUB, span, TM), F32),
                   pltpu.VMEM((WIN_SUB * Q_HEADS * VT_ROWS, TM), F32)]
        scratch += [pltpu.VMEM((TM + span, TM), BF16)] * (2 * WIN_LAG)
    else:
        scratch = [pltpu.VMEM((Q_COLS, TM), F32)]
    return pl.pallas_call(
        _window_attn_bounded_kernel if bounded else _window_attn_kernel,
        grid=(b, pl.cdiv(n_tiles, n_sub)),
        in_specs=[q_spec(e) for e in range(n_sub)] + [
            pl.BlockSpec((None, KV_HEADS, l, F8_ROWS), lambda bi, i: (bi, 0, 0, 0)),
            pl.BlockSpec((1, TM), lambda bi, i: (0, 0)),
            pl.BlockSpec((None, KV_HEADS * VT_ROWS, l), lambda bi, i: (bi, 0, 0)),
            pl.BlockSpec((Q_HEADS, TM), lambda bi, i: (0, 0))],
        out_specs=pl.BlockSpec((None, n_sub * TM, Q_COLS), lambda bi, i: (bi, i, 0)),
        out_shape=jax.ShapeDtypeStruct((b, l, Q_COLS), BF16),
        scratch_shapes=scratch,
        compiler_params=_params(2),
        name="window_attn_bounded" if bounded else "window_attn",
    )(*([q8] * n_sub), k8, un, vt, sink_b)


def _merge_kernel(x_ref, mod_ref, g_ref, oa_ref, oc_ref, b_ref, cu_ref, cup_ref, cun_ref, cw_ref,
                  wgate_ref, bgate_ref, wpa_ref, wpb_ref, wpc_ref, wo_ref, o_ref):
    i = pl.program_id(1)
    n_tiles = pl.num_programs(1)
    x = x_ref[...]
    tm, d = x.shape
    h = _rms_mod(x, g_ref[...], mod_ref[3:4, :], mod_ref[4:5, :]).astype(BF16)
    gates = _sigmoid(jnp.dot(h, wgate_ref[...], preferred_element_type=F32) + bgate_ref[...])

    cu = cu_ref[...]
    has_prev = (i >= 2).astype(F32)
    has_next = jnp.logical_and(i >= 1, i < n_tiles - 1).astype(F32)
    prev_row = cup_ref[7:8, :] * has_prev
    next_row = cun_ref[0:1, :] * has_next
    row = lax.broadcasted_iota(jnp.int32, (tm, 1), 0)
    cu_dn = jnp.where(row == 0, prev_row, pltpu.roll(cu, 1, 0))
    cu_up = jnp.where(row == tm - 1, next_row, pltpu.roll(cu, tm - 1, 0))
    y = cw_ref[0:1, :] * cu_dn + cw_ref[1:2, :] * cu + cw_ref[2:3, :] * cu_up
    o_b = (b_ref[...] * y).astype(BF16)

    pa = jnp.dot(oa_ref[...].astype(BF16), wpa_ref[...], preferred_element_type=F32)
    pb = jnp.dot(o_b, wpb_ref[...], preferred_element_type=F32)
    pc = jnp.dot(oc_ref[...].astype(BF16), wpc_ref[...], preferred_element_type=F32)
    mix = gates[:, 0:d] * pa + gates[:, d:2 * d] * pb + gates[:, 2 * d:3 * d] * pc
    out = jnp.dot(mix.astype(BF16), wo_ref[...], preferred_element_type=F32)
    o_ref[...] = x + mod_ref[5:6, :] * out


def _merge(xs, mod, norm_g, o_a, o_c, bb, cu, conv_w, w_gate, b_gate, w_pa, w_pb, w_pc, w_o,
           *, layer):
    b, l, d = xs.shape
    cw = cu.shape[-1]
    sub = 8
    tok = lambda w: pl.BlockSpec((None, TM, w), lambda bi, i: (bi, i, 0))
    lay = lambda *s: _resident((None,) + s, lambda bi, i: (layer,) + (0,) * len(s))
    return pl.pallas_call(
        _merge_kernel,
        grid=(b, l // TM),
        in_specs=[tok(d),
                  pl.BlockSpec((None, None, N_MOD, d), lambda bi, i: (bi, jnp.minimum(i, 1), 0, 0)),
                  _resident((None, None, 1, d), lambda bi, i: (layer, 1, 0, 0)),
                  tok(Q_COLS), tok(Q_COLS), tok(cw), tok(cw),
                  pl.BlockSpec((None, sub, cw),
                               lambda bi, i: (bi, jnp.maximum(i * (TM // sub) - 1, 0), 0)),
                  pl.BlockSpec((None, sub, cw),
                               lambda bi, i: (bi, jnp.minimum((i + 1) * (TM // sub), l // sub - 1), 0)),
                  lay(3, cw), lay(d, 3 * d), lay(1, 3 * d), lay(Q_COLS, d), lay(cw, d),
                  lay(Q_COLS, d), lay(d, d)],
        out_specs=tok(d),
        out_shape=jax.ShapeDtypeStruct(xs.shape, F32),
        input_output_aliases={0: 0},
        compiler_params=_params(2),
        name="merge",
    )(xs, mod, norm_g, o_a, o_c, bb, cu, cu, cu, conv_w, w_gate, b_gate, w_pa, w_pb, w_pc, w_o)


def _rope_tables(s, n_ctx):
    rows = s // GRID_W
    row = jnp.repeat(jnp.arange(rows), GRID_W).astype(F32)
    col = jnp.tile(jnp.arange(GRID_W), rows).astype(F32)
    half = HEAD_DIM // 2
    inv = ROPE_THETA ** (-jnp.arange(0, half, 2, dtype=F32) / half)
    ang_r = row[:, None] * inv
    ang_c = col[:, None] * inv
    tabs = jnp.concatenate([jnp.cos(ang_r), jnp.sin(ang_r), jnp.cos(ang_c), jnp.sin(ang_c)],
                           axis=-1)
    one, zero = jnp.ones((n_ctx, HEAD_DIM // 4), F32), jnp.zeros((n_ctx, HEAD_DIM // 4), F32)
    ident = jnp.concatenate([one, zero, one, zero], axis=-1)
    return jnp.concatenate([ident, tabs], axis=0).T


def kernel(x, c, ctx, c_ctx, w_ada, b_ada, norm_g, ffn_w_gate, ffn_w_up, ffn_w_down, w_in, qk_g,
           sink_a, conv_w, w_pa, w_pb, w_pc, w_gate, b_gate, w_o):
    bsz, s, d = x.shape
    n_ctx = ctx.shape[1]
    depth = w_ada.shape[0]
    assert n_ctx == TM and s % TM == 0 and s % GRID_W == 0 and bsz < 8

    cvec = jnp.zeros((8, d), F32).at[:bsz].set(c).at[bsz].set(c_ctx)
    mods = _ada(cvec, w_ada, b_ada).reshape(depth, 8, N_MOD, d)
    lat = mods[:, :bsz]
    con = jnp.broadcast_to(mods[:, bsz:bsz + 1], lat.shape)
    mod_all = jnp.stack([con, lat], axis=2)

    rope = _rope_tables(s, n_ctx)
    gmax = jnp.max(jnp.abs(qk_g.astype(F32)), axis=-1)
    vmax = gmax * (HEAD_DIM ** 0.5) * jnp.array([Q_SCALE, 1.0, Q_SCALE, 1.0], F32)
    pow2 = jnp.exp2(jnp.clip(jnp.floor(jnp.log2(0.98 * F8_MAX / vmax)), -20.0, 20.0))
    unscale = 1.0 / (pow2[:, 0::2] * pow2[:, 1::2])
    fold = (pow2 * jnp.array([Q_SCALE, 1.0, Q_SCALE, 1.0], F32))[:, :, None]
    qkg = jnp.broadcast_to((qk_g.astype(F32) * fold).reshape(depth, 4 * HEAD_DIM, 1),
                           (depth, 4 * HEAD_DIM, TM))
    ng = norm_g.astype(F32)[:, :, None, :]
    wg, wu, wd = (w.astype(BF16) for w in (ffn_w_gate, ffn_w_up, ffn_w_down))
    w_in_b, w_gate_b, w_pa_b, w_pb_b, w_pc_b, w_o_b = (
        w.astype(BF16) for w in (w_in, w_gate, w_pa, w_pb, w_pc, w_o))
    b_gate3 = b_gate.astype(F32)[:, None, :]

    xs = jnp.concatenate([ctx, x], axis=1)
    for i in range(depth):
        mod = mod_all[i]
        xs = _ffn(xs, mod, ng, wg, wu, wd, layer=i, which=0, j0=0)
        qa8, qc8, ka8, kc8, vta, vtc, bb, cu = _inproj(xs, mod, ng, w_in_b, qkg, rope, layer=i)
        o_a = _window_attn(qa8, ka8, unscale[i, 0], vta, sink_a[i], 1.02 * vmax[i, 0] * vmax[i, 1])
        o_c = _global_attn(qc8, kc8, unscale[i, 1], vtc, 1.02 * vmax[i, 2] * vmax[i, 3])
        xs = _merge(xs, mod, ng, o_a, o_c, bb, cu, conv_w.astype(F32), w_gate_b, b_gate3,
                    w_pa_b, w_pb_b, w_pc_b, w_o_b, layer=i)
        xs = _ffn(xs, mod, ng, wg, wu, wd, layer=i, which=1, j0=6, latents_out=i + 1 == depth)
    return xs
```
